```python
import math
import jax, jax.numpy as jnp
from jax import lax
import numpy as np

D_MODEL = 1024
BATCH = 4
SEQ = 8192
DEPTH = 1

HEAD_DIM = 64
A_HEADS = D_MODEL // (2 * HEAD_DIM)
A_KV_HEADS = 2
A_GROUP = A_HEADS // A_KV_HEADS
A_WIDTH = A_HEADS * HEAD_DIM
A_KV_WIDTH = A_KV_HEADS * HEAD_DIM
WINDOW = 128
B_HEADS = D_MODEL // (2 * HEAD_DIM)
B_WIDTH = B_HEADS * HEAD_DIM
MOBA_BLOCK = 256
MOBA_TOPK = 3
MOBA_CHUNK = 64
N_BUCKETS = 32
MAX_DISTANCE = 128
DN_ALPHA = (2.0 * DEPTH) ** 0.25
DN_BETA = (8.0 * DEPTH) ** -0.25
LN_EPS = 1e-5
NEG = -1e30
ATTN_SCALE = HEAD_DIM ** -0.5

IN_SIZES = (A_WIDTH, A_KV_WIDTH, A_KV_WIDTH, A_WIDTH,
            B_WIDTH, B_WIDTH, B_WIDTH, B_WIDTH,
            D_MODEL, D_MODEL)
IN_TOTAL = sum(IN_SIZES)

kernel_name = "hybrid_swa_sink_moba_gated_deepnorm"


def _split_cols(h):
    parts, start = [], 0
    for size in IN_SIZES:
        parts.append(h[..., start:start + size])
        start += size
    return parts


def t5_bucket(dist):
    max_exact = N_BUCKETS // 2
    n = jnp.maximum(dist, 0)
    nf = jnp.maximum(n, 1).astype(jnp.float32)
    large = max_exact + (jnp.log(nf / max_exact) / math.log(MAX_DISTANCE / max_exact)
                         * (N_BUCKETS - max_exact)).astype(jnp.int32)
    large = jnp.minimum(large, N_BUCKETS - 1)
    return jnp.where(n < max_exact, n, large)


def layer_norm(x, g, b):
    xf = x.astype(jnp.float32)
    mu = jnp.mean(xf, axis=-1, keepdims=True)
    var = jnp.mean(jnp.square(xf - mu), axis=-1, keepdims=True)
    return ((xf - mu) * lax.rsqrt(var + LN_EPS) * g.astype(jnp.float32)
            + b.astype(jnp.float32)).astype(x.dtype)


def swa_sink_attention(q, k, v, sinks, rel_table):
    B_, S_, _ = q.shape
    nb = S_ // WINDOW
    qb = q.reshape(B_, nb, WINDOW, A_KV_HEADS, A_GROUP, HEAD_DIM)
    kb = k.reshape(B_, nb, WINDOW, A_KV_HEADS, HEAD_DIM)
    vb = v.reshape(B_, nb, WINDOW, A_KV_HEADS, HEAD_DIM)
    pad = jnp.zeros_like(kb[:, :1])
    k2 = jnp.concatenate([jnp.concatenate([pad, kb[:, :-1]], axis=1), kb], axis=2)
    v2 = jnp.concatenate([jnp.concatenate([pad, vb[:, :-1]], axis=1), vb], axis=2)
    logits = jnp.einsum('bnqkgd,bnskd->bnkgqs', qb, k2,
                        preferred_element_type=jnp.float32) * ATTN_SCALE
    i = jnp.arange(WINDOW)[:, None]
    j = jnp.arange(2 * WINDOW)[None, :]
    dist = WINDOW + i - j
    bias = rel_table.astype(jnp.float32)[t5_bucket(dist)]
    bias = bias.transpose(2, 0, 1).reshape(A_KV_HEADS, A_GROUP, WINDOW, 2 * WINDOW)
    valid = (dist >= 0) & (dist < WINDOW)
    first = (jnp.arange(nb) == 0)[:, None, None] & (j < WINDOW)[None]
    mask = valid[None] & ~first
    logits = jnp.where(mask[None, :, None, None], logits + bias[None, None], NEG)
    sink = jnp.broadcast_to(sinks.astype(jnp.float32).reshape(A_KV_HEADS, A_GROUP)[None, None, :, :, None, None],
                            logits.shape[:-1] + (1,))
    p = jax.nn.softmax(jnp.concatenate([logits, sink], axis=-1), axis=-1)[..., :-1]
    o = jnp.einsum('bnkgqs,bnskd->bnqkgd', p.astype(v.dtype), v2)
    return o.reshape(B_, S_, A_WIDTH)


def moba_attention(q, k, v, rel_table):
    B_, S_, _ = q.shape
    s_pad = -(-S_ // MOBA_BLOCK) * MOBA_BLOCK
    padw = ((0, 0), (0, s_pad - S_), (0, 0))
    q, k, v = [jnp.pad(t, padw).reshape(B_, s_pad, B_HEADS, HEAD_DIM).transpose(0, 2, 1, 3)
               for t in (q, k, v)]
    nblk = s_pad // MOBA_BLOCK
    k_eff = min(MOBA_TOPK, nblk)
    kblk = k.reshape(B_, B_HEADS, nblk, MOBA_BLOCK, HEAD_DIM)
    vblk = v.reshape(B_, B_HEADS, nblk, MOBA_BLOCK, HEAD_DIM)
    kmean = jnp.mean(kblk.astype(jnp.float32), axis=3)
    gate = jnp.einsum('bhtd,bhnd->bhtn', q.astype(jnp.float32), kmean)
    q_blk = jnp.arange(s_pad) // MOBA_BLOCK
    past = jnp.arange(nblk)[None, :] < q_blk[:, None]
    gate = jnp.where(past, gate, NEG)
    _, sel = lax.top_k(gate, k_eff)
    n_chunks = s_pad // MOBA_CHUNK
    qc = q.reshape(B_, B_HEADS, n_chunks, MOBA_CHUNK, HEAD_DIM).transpose(2, 0, 1, 3, 4)
    selc = sel.reshape(B_, B_HEADS, n_chunks, MOBA_CHUNK, k_eff).transpose(2, 0, 1, 3, 4)
    bi = jnp.arange(B_)[:, None, None, None]
    hi = jnp.arange(B_HEADS)[None, :, None, None]
    hi5 = jnp.arange(B_HEADS)[None, :, None, None, None]
    table_t = rel_table.astype(jnp.float32).T
    blk_ar = jnp.arange(MOBA_BLOCK)

    def chunk_fn(args):
        c, q_c, sel_c = args
        t = c * MOBA_CHUNK + jnp.arange(MOBA_CHUNK)
        own = t[0] // MOBA_BLOCK
        k_own = lax.dynamic_index_in_dim(kblk, own, axis=2, keepdims=False)
        v_own = lax.dynamic_index_in_dim(vblk, own, axis=2, keepdims=False)
        k_sel = kblk[bi, hi, sel_c]
        v_sel = vblk[bi, hi, sel_c]
        d_own = t[:, None] - (own * MOBA_BLOCK + blk_ar)[None, :]
        l_own = (jnp.einsum('bhcd,bhsd->bhcs', q_c, k_own, preferred_element_type=jnp.float32) * ATTN_SCALE
                 + table_t[:, t5_bucket(d_own)][None])
        l_own = jnp.where(d_own >= 0, l_own, NEG)
        d_sel = t[None, None, :, None, None] - (sel_c[..., None] * MOBA_BLOCK + blk_ar)
        l_sel = (jnp.einsum('bhcd,bhcksd->bhcks', q_c, k_sel, preferred_element_type=jnp.float32) * ATTN_SCALE
                 + table_t[hi5, t5_bucket(d_sel)])
        valid_sel = jnp.arange(k_eff)[None, :] < (t // MOBA_BLOCK)[:, None]
        l_sel = jnp.where(valid_sel[:, :, None], l_sel, NEG)
        B2, H2, C2 = q_c.shape[:3]
        logits = jnp.concatenate([l_own, l_sel.reshape(B2, H2, C2, k_eff * MOBA_BLOCK)], axis=-1)
        p = jax.nn.softmax(logits, axis=-1).astype(v.dtype)
        p_own = p[..., :MOBA_BLOCK]
        p_sel = p[..., MOBA_BLOCK:].reshape(B2, H2, C2, k_eff, MOBA_BLOCK)
        return (jnp.einsum('bhcs,bhsd->bhcd', p_own, v_own)
                + jnp.einsum('bhcks,bhcksd->bhcd', p_sel, v_sel))

    out = lax.map(chunk_fn, (jnp.arange(n_chunks), qc, selc))
    out = out.transpose(1, 0, 3, 2, 4).reshape(B_, s_pad, B_WIDTH)
    return out[:, :S_]


def setup_inputs(seed: int = 0) -> dict:
    key = jax.random.key(seed)
    ks = jax.random.split(key, 20)
    x = jax.random.normal(ks[0], (BATCH, SEQ, D_MODEL), jnp.float32)
    col_scale = (1.0, 1.0, DN_BETA, 1.0, 1.0, 1.0, DN_BETA, 1.0, 1.0, 1.0)
    blocks = [jax.random.normal(ks[1 + i], (DEPTH, D_MODEL, n), jnp.float32) * (D_MODEL ** -0.5 * s)
              for i, (n, s) in enumerate(zip(IN_SIZES, col_scale))]
    w_in = jnp.concatenate(blocks, axis=-1)
    b_gate = 0.1 * jax.random.normal(ks[11], (DEPTH, 2 * D_MODEL), jnp.float32)
    sinks = 0.5 * jax.random.normal(ks[12], (DEPTH, A_HEADS), jnp.float32)
    rel_table = 0.5 * jax.random.normal(ks[13], (N_BUCKETS, A_HEADS + B_HEADS), jnp.float32)
    w_out_a = jax.random.normal(ks[14], (DEPTH, A_WIDTH, D_MODEL), jnp.float32) * (A_WIDTH ** -0.5 * DN_BETA)
    w_out_b = jax.random.normal(ks[15], (DEPTH, B_WIDTH, D_MODEL), jnp.float32) * (B_WIDTH ** -0.5 * DN_BETA)
    w_out = jax.random.normal(ks[16], (DEPTH, D_MODEL, D_MODEL), jnp.float32) * (D_MODEL ** -0.5 * DN_BETA)
    ln_gamma = 1.0 + 0.01 * jax.random.normal(ks[17], (DEPTH, D_MODEL), jnp.float32)
    ln_beta = 0.01 * jax.random.normal(ks[18], (DEPTH, D_MODEL), jnp.float32)
    return {"x": x, "w_in": w_in, "b_gate": b_gate, "sinks": sinks, "rel_table": rel_table,
            "w_out_a": w_out_a, "w_out_b": w_out_b, "w_out": w_out,
            "ln_gamma": ln_gamma, "ln_beta": ln_beta}


def reference(x, w_in, b_gate, sinks, rel_table, w_out_a, w_out_b, w_out, ln_gamma, ln_beta):
    table_a = rel_table[:, :A_HEADS]
    table_b = rel_table[:, A_HEADS:]
    for l in range(DEPTH):
        h = jnp.einsum('bsd,de->bse', x, w_in[l])
        qa, ka, va, za, qb, kb, vb, zb, ga, gb = _split_cols(h)
        ya = swa_sink_attention(qa, ka, va, sinks[l], table_a) * jax.nn.silu(za)
        yb = moba_attention(qb, kb, vb, table_b) * jax.nn.silu(zb)
        ya = jnp.einsum('bse,ed->bsd', ya, w_out_a[l])
        yb = jnp.einsum('bse,ed->bsd', yb, w_out_b[l])
        gate_a = jax.nn.sigmoid(ga + b_gate[l, :D_MODEL])
        gate_b = jax.nn.sigmoid(gb + b_gate[l, D_MODEL:])
        merged = gate_a * ya + gate_b * yb
        out = jnp.einsum('bsd,de->bse', merged, w_out[l])
        x = layer_norm(DN_ALPHA * x + out, ln_gamma[l], ln_beta[l])
    return x
```

```python
import functools
import math

import numpy as np
import jax
import jax.numpy as jnp
from jax import lax
from jax.experimental import pallas as pl
from jax.experimental.pallas import tpu as pltpu

D_MODEL = 1024
HEAD_DIM = 64
A_HEADS = 8
A_KV_HEADS = 2
A_GROUP = A_HEADS // A_KV_HEADS
A_WIDTH = A_HEADS * HEAD_DIM
A_KV_WIDTH = A_KV_HEADS * HEAD_DIM
WINDOW = 128
B_HEADS = 8
B_WIDTH = B_HEADS * HEAD_DIM
MOBA_BLOCK = 256
MOBA_TOPK = 3
N_BUCKETS = 32
MAX_DISTANCE = 128
DEPTH = 1
DN_ALPHA = (2.0 * DEPTH) ** 0.25
LN_EPS = 1e-5
NEG = -1e30
ATTN_SCALE = HEAD_DIM ** -0.5

DM_QA, DM_VA, DM_QB, DM_VB = 0, A_WIDTH, A_WIDTH + A_KV_WIDTH, A_WIDTH + A_KV_WIDTH + B_WIDTH
DM_ROWS = DM_VB + B_WIDTH
RM_COLS = A_KV_WIDTH + B_WIDTH
LANE = 128
PROJ_TOKENS = 512
SWA_TOKENS = 256
FINAL_TOKENS = 256
VMEM_LIMIT = 56 * 1024 * 1024


def _t5_bucket_np(dist):
    max_exact = N_BUCKETS // 2
    n = np.maximum(dist, 0)
    nf = np.maximum(n, 1).astype(np.float32)
    large = max_exact + (np.log(nf / max_exact) / math.log(MAX_DISTANCE / max_exact)
                         * (N_BUCKETS - max_exact)).astype(np.int32)
    large = np.minimum(large, N_BUCKETS - 1)
    return np.where(n < max_exact, n, large).astype(np.int32)


def _proj_kernel(x_ref, wrm_ref, wdm_ref, krm_ref, dm_ref, kmean_ref):
    xb = x_ref[0].astype(jnp.bfloat16)
    rm = jnp.dot(xb, wrm_ref[...], preferred_element_type=jnp.float32)
    krm_ref[0] = rm.astype(jnp.bfloat16)
    for c in range(PROJ_TOKENS // MOBA_BLOCK):
        kb = rm[c * MOBA_BLOCK:(c + 1) * MOBA_BLOCK, A_KV_WIDTH:]
        kmean_ref[0, 0, c:c + 1, :] = jnp.sum(kb, axis=0, keepdims=True) * (1.0 / MOBA_BLOCK)
    dm = lax.dot_general(wdm_ref[...], xb, (((1,), (1,)), ((), ())),
                         preferred_element_type=jnp.float32)
    dmb = dm.astype(jnp.bfloat16)
    for c in range(PROJ_TOKENS // MOBA_BLOCK):
        dm_ref[0, c] = dmb[:, c * MOBA_BLOCK:(c + 1) * MOBA_BLOCK]


def _proj(x, w_rm, w_dm_t):
    B, S, D = x.shape
    nt = S // PROJ_TOKENS
    cpt = PROJ_TOKENS // MOBA_BLOCK
    return pl.pallas_call(
        _proj_kernel,
        grid=(B, nt),
        in_specs=[
            pl.BlockSpec((1, PROJ_TOKENS, D), lambda b, t: (b, t, 0)),
            pl.BlockSpec((D, RM_COLS), lambda b, t: (0, 0)),
            pl.BlockSpec((DM_ROWS, D), lambda b, t: (0, 0)),
        ],
        out_specs=[
            pl.BlockSpec((1, PROJ_TOKENS, RM_COLS), lambda b, t: (b, t, 0)),
            pl.BlockSpec((1, cpt, DM_ROWS, MOBA_BLOCK), lambda b, t: (b, t, 0, 0)),
            pl.BlockSpec((1, 1, cpt, B_WIDTH), lambda b, t: (b, t, 0, 0)),
        ],
        out_shape=[
            jax.ShapeDtypeStruct((B, S, RM_COLS), jnp.bfloat16),
            jax.ShapeDtypeStruct((B, S // MOBA_BLOCK, DM_ROWS, MOBA_BLOCK), jnp.bfloat16),
            jax.ShapeDtypeStruct((B, nt, cpt, B_WIDTH), jnp.float32),
        ],
        compiler_params=pltpu.CompilerParams(
            dimension_semantics=("parallel", "parallel"), vmem_limit_bytes=VMEM_LIMIT),
        name="proj",
    )(x, w_rm, w_dm_t)


def _swa_kernel(qt_ref, vcur_ref, vprev_ref, kcur_ref, kprev_ref, bias_ref, sink_ref, o_ref):
    t = pl.program_id(1)
    qt = qt_ref[0, 0]
    vcat = jnp.concatenate([vprev_ref[0, 0][:, WINDOW:], vcur_ref[0, 0]], axis=1)
    kcat = jnp.concatenate([kprev_ref[0], kcur_ref[0]], axis=0)
    kidx = lax.broadcasted_iota(jnp.int32, (2 * WINDOW, A_GROUP * WINDOW), 0)
    zeros_q = jnp.zeros((HEAD_DIM, A_GROUP * WINDOW), jnp.bfloat16)
    for w in range(SWA_TOKENS // WINDOW):
        k2 = kcat[w * WINDOW:(w + 2) * WINDOW, :]
        v2 = vcat[:, w * WINDOW:(w + 2) * WINDOW]
        outs = []
        for g in range(A_KV_HEADS):
            q4 = jnp.concatenate(
                [qt[(g * A_GROUP + i) * HEAD_DIM:(g * A_GROUP + i + 1) * HEAD_DIM,
                    w * WINDOW:(w + 1) * WINDOW] for i in range(A_GROUP)], axis=1)
            qpad = jnp.concatenate([q4, zeros_q] if g == 0 else [zeros_q, q4], axis=0)
            s = jnp.dot(k2, qpad, preferred_element_type=jnp.float32) + bias_ref[g]
            if w == 0:
                s = jnp.where((t == 0) & (kidx < WINDOW), NEG, s)
            sink = sink_ref[g]
            m = jnp.maximum(jnp.max(s, axis=0, keepdims=True), sink)
            p = jnp.exp(s - m)
            l = jnp.sum(p, axis=0, keepdims=True) + jnp.exp(sink - m)
            o = jnp.dot(v2[g * HEAD_DIM:(g + 1) * HEAD_DIM, :], p.astype(jnp.bfloat16),
                        preferred_element_type=jnp.float32)
            o = o / l
            outs += [o[:, i * WINDOW:(i + 1) * WINDOW] for i in range(A_GROUP)]
        ot = jnp.concatenate(outs, axis=0)
        o_ref[0, w * WINDOW:(w + 1) * WINDOW, :] = ot.T.astype(jnp.bfloat16)


def _swa(dm, krm, bias_a, sink_a):
    B, NB = dm.shape[0], dm.shape[1]
    S = NB * MOBA_BLOCK
    va_blk = DM_VA // A_KV_WIDTH
    return pl.pallas_call(
        _swa_kernel,
        grid=(B, NB),
        in_specs=[
            pl.BlockSpec((1, 1, A_WIDTH, SWA_TOKENS), lambda b, t: (b, t, 0, 0)),
            pl.BlockSpec((1, 1, A_KV_WIDTH, SWA_TOKENS), lambda b, t: (b, t, va_blk, 0)),
            pl.BlockSpec((1, 1, A_KV_WIDTH, SWA_TOKENS),
                         lambda b, t: (b, jnp.maximum(t - 1, 0), va_blk, 0)),
            pl.BlockSpec((1, SWA_TOKENS, A_KV_WIDTH), lambda b, t: (b, t, 0)),
            pl.BlockSpec((1, WINDOW, A_KV_WIDTH),
                         lambda b, t: (b, jnp.maximum(2 * t - 1, 0), 0)),
            pl.BlockSpec((A_KV_HEADS, 2 * WINDOW, A_GROUP * WINDOW), lambda b, t: (0, 0, 0)),
            pl.BlockSpec((A_KV_HEADS, 1, A_GROUP * WINDOW), lambda b, t: (0, 0, 0)),
        ],
        out_specs=pl.BlockSpec((1, SWA_TOKENS, A_WIDTH), lambda b, t: (b, t, 0)),
        out_shape=jax.ShapeDtypeStruct((B, S, A_WIDTH), jnp.bfloat16),
        compiler_params=pltpu.CompilerParams(
            dimension_semantics=("parallel", "parallel"), vmem_limit_bytes=VMEM_LIMIT),
        name="swa",
    )(dm, dm, dm, krm, krm, bias_a, sink_a)


def _moba_kernel(c31_ref, qt_ref, k_ref, vt_ref, kmean_ref, bown_ref, bprev_ref, o_ref, sel_ref):
    pr = pl.program_id(1)
    n = pl.program_id(2)
    nblk = kmean_ref.shape[1]
    qt = qt_ref[0, 0]
    row_head = lax.broadcasted_iota(jnp.int32, qt.shape, 0) // HEAD_DIM
    blk = lax.broadcasted_iota(jnp.int32, (nblk, MOBA_BLOCK), 0)
    km = kmean_ref[0].astype(jnp.bfloat16)
    outs = []
    for r in range(2):
        c31 = c31_ref[2 * pr + r]
        qm = jnp.where(row_head == r, qt, jnp.zeros_like(qt))

        gate = jnp.dot(km, qm, preferred_element_type=jnp.float32)
        avail = blk < n
        sel = jnp.zeros((nblk, MOBA_BLOCK), jnp.float32)
        for _ in range(MOBA_TOPK):
            gm = jnp.where(avail, gate, -jnp.inf)
            mx = jnp.max(gm, axis=0, keepdims=True)
            cand = jnp.where(avail & (gm == mx), blk, nblk)
            pick = blk == jnp.min(cand, axis=0, keepdims=True)
            sel = jnp.where(pick, 1.0, sel)
            avail = avail & jnp.logical_not(pick)
        sel_ref[r] = sel

        def vt_blk(j):
            return vt_ref[0, j, r * HEAD_DIM:(r + 1) * HEAD_DIM, :]

        def k_blk(j):
            return k_ref[0, pl.ds(pl.multiple_of(j * MOBA_BLOCK, MOBA_BLOCK), MOBA_BLOCK), :]

        s = jnp.dot(k_blk(n), qm, preferred_element_type=jnp.float32) + bown_ref[r]
        m = jnp.max(s, axis=0, keepdims=True)
        p = jnp.exp(s - m)
        l = jnp.sum(p, axis=0, keepdims=True)
        acc = jnp.dot(vt_blk(n), p.astype(jnp.bfloat16), preferred_element_type=jnp.float32)

        def masked_step(j, s, carry):
            m, l, acc = carry
            on = sel_ref[r, pl.ds(j, 1), :] > 0.5
            bm = jnp.max(s, axis=0, keepdims=True)
            m_new = jnp.maximum(m, jnp.where(on, bm, -jnp.inf))
            alpha = jnp.exp(m - m_new)
            p = jnp.exp(s - jnp.where(on, m_new, -NEG))
            l = alpha * l + jnp.sum(p, axis=0, keepdims=True)
            acc = alpha * acc + jnp.dot(vt_blk(j), p.astype(jnp.bfloat16),
                                        preferred_element_type=jnp.float32)
            return m_new, l, acc

        jp = jnp.maximum(n - 1, 0)
        s = jnp.dot(k_blk(jp), qm, preferred_element_type=jnp.float32) + bprev_ref[r]
        m, l, acc = masked_step(jp, s, (m, l, acc))

        def far_body(j, carry):
            s = jnp.dot(k_blk(j), qm, preferred_element_type=jnp.float32) + c31
            return masked_step(j, s, carry)

        m, l, acc = lax.fori_loop(0, jnp.maximum(n - 1, 0), far_body, (m, l, acc))
        outs.append(acc / l)
    ot = jnp.concatenate(outs, axis=0)
    o_ref[0] = ot.T.astype(jnp.bfloat16)


def _moba(dm, krm, kmean, bias_own, bias_prev, c31):
    B, NB = dm.shape[0], dm.shape[1]
    S = NB * MOBA_BLOCK
    pairs = B_HEADS // 2
    qb_blk = DM_QB // LANE
    vb_blk = DM_VB // LANE
    kb_blk = A_KV_WIDTH // LANE
    return pl.pallas_call(
        _moba_kernel,
        grid=(B, pairs, NB),
        in_specs=[
            pl.BlockSpec(memory_space=pltpu.SMEM),
            pl.BlockSpec((1, 1, LANE, MOBA_BLOCK), lambda b, p, n: (b, n, qb_blk + p, 0)),
            pl.BlockSpec((1, S, LANE), lambda b, p, n: (b, 0, kb_blk + p)),
            pl.BlockSpec((1, NB, LANE, MOBA_BLOCK), lambda b, p, n: (b, 0, vb_blk + p, 0)),
            pl.BlockSpec((1, NB, LANE), lambda b, p, n: (b, 0, p)),
            pl.BlockSpec((2, MOBA_BLOCK, MOBA_BLOCK), lambda b, p, n: (p, 0, 0)),
            pl.BlockSpec((2, MOBA_BLOCK, MOBA_BLOCK), lambda b, p, n: (p, 0, 0)),
        ],
        out_specs=pl.BlockSpec((1, MOBA_BLOCK, LANE), lambda b, p, n: (b, n, p)),
        out_shape=jax.ShapeDtypeStruct((B, S, B_WIDTH), jnp.bfloat16),
        scratch_shapes=[pltpu.VMEM((2, NB, MOBA_BLOCK), jnp.float32)],
        compiler_params=pltpu.CompilerParams(
            dimension_semantics=("parallel", "parallel", "arbitrary"),
            vmem_limit_bytes=VMEM_LIMIT),
        name="moba",
    )(c31, dm, krm, dm, kmean, bias_own, bias_prev)


def _final_kernel(x_ref, oa_ref, ob_ref, wzg_ref, bg_ref, woa_ref, wob_ref, wout_ref,
                  gamma_ref, beta_ref, y_ref):
    x = x_ref[...]
    zg = jnp.dot(x.astype(jnp.bfloat16), wzg_ref[...], preferred_element_type=jnp.float32)
    za = zg[:, :A_WIDTH]
    zb = zg[:, A_WIDTH:A_WIDTH + B_WIDTH]
    g = zg[:, A_WIDTH + B_WIDTH:] + bg_ref[...]
    ua = oa_ref[...].astype(jnp.float32) * (za * jax.nn.sigmoid(za))
    ub = ob_ref[...].astype(jnp.float32) * (zb * jax.nn.sigmoid(zb))
    ya = jnp.dot(ua.astype(jnp.bfloat16), woa_ref[...], preferred_element_type=jnp.float32)
    yb = jnp.dot(ub.astype(jnp.bfloat16), wob_ref[...], preferred_element_type=jnp.float32)
    gs = jax.nn.sigmoid(g)
    merged = gs[:, :D_MODEL] * ya + gs[:, D_MODEL:] * yb
    out = jnp.dot(merged.astype(jnp.bfloat16), wout_ref[...], preferred_element_type=jnp.float32)
    r = DN_ALPHA * x + out
    mu = jnp.mean(r, axis=-1, keepdims=True)
    rc = r - mu
    var = jnp.mean(rc * rc, axis=-1, keepdims=True)
    y_ref[...] = rc * lax.rsqrt(var + LN_EPS) * gamma_ref[...] + beta_ref[...]


def _final(x2, oa2, ob2, w_zg, b_gate, w_oa, w_ob, w_out, gamma, beta):
    N, D = x2.shape
    T = FINAL_TOKENS
    full = lambda a: pl.BlockSpec(a.shape, lambda i: (0,) * a.ndim)
    return pl.pallas_call(
        _final_kernel,
        grid=(N // T,),
        in_specs=[
            pl.BlockSpec((T, D), lambda i: (i, 0)),
            pl.BlockSpec((T, A_WIDTH), lambda i: (i, 0)),
            pl.BlockSpec((T, B_WIDTH), lambda i: (i, 0)),
            full(w_zg), full(b_gate), full(w_oa), full(w_ob), full(w_out), full(gamma), full(beta),
        ],
        out_specs=pl.BlockSpec((T, D), lambda i: (i, 0)),
        out_shape=jax.ShapeDtypeStruct((N, D), jnp.float32),
        compiler_params=pltpu.CompilerParams(
            dimension_semantics=("parallel",), vmem_limit_bytes=VMEM_LIMIT),
        name="final",
    )(x2, oa2, ob2, w_zg, b_gate, w_oa, w_ob, w_out, gamma, beta)


def _bias_tables(rel_table, sinks):
    table_a = rel_table[:, :A_HEADS].astype(jnp.float32)
    table_b = rel_table[:, A_HEADS:].astype(jnp.float32)
    kk = np.arange(2 * WINDOW)[:, None]
    qq = np.arange(WINDOW)[None, :]
    dist = WINDOW + qq - kk
    valid = (dist >= 0) & (dist < WINDOW)
    ba = jnp.where(valid[:, :, None], table_a[_t5_bucket_np(dist)], NEG)
    ba = ba.transpose(2, 0, 1).reshape(A_KV_HEADS, A_GROUP, 2 * WINDOW, WINDOW)
    ba = ba.transpose(0, 2, 1, 3).reshape(A_KV_HEADS, 2 * WINDOW, A_GROUP * WINDOW)
    sink = jnp.broadcast_to(sinks.astype(jnp.float32).reshape(A_KV_HEADS, 1, A_GROUP, 1),
                            (A_KV_HEADS, 1, A_GROUP, WINDOW)).reshape(A_KV_HEADS, 1, A_GROUP * WINDOW)
    kb = np.arange(MOBA_BLOCK)[:, None]
    qb = np.arange(MOBA_BLOCK)[None, :]
    d_own = qb - kb
    b_own = jnp.where((d_own >= 0)[:, :, None], table_b[_t5_bucket_np(d_own)], NEG).transpose(2, 0, 1)
    b_prev = table_b[_t5_bucket_np(d_own + MOBA_BLOCK)].transpose(2, 0, 1)
    c31 = table_b[N_BUCKETS - 1]
    return ba, sink, b_own, b_prev, c31


def kernel(x, w_in, b_gate, sinks, rel_table, w_out_a, w_out_b, w_out, ln_gamma, ln_beta):
    B, S, D = x.shape
    assert (D, w_in.shape[0]) == (D_MODEL, DEPTH) and S % PROJ_TOKENS == 0
    w = w_in[0]
    sizes = (A_WIDTH, A_KV_WIDTH, A_KV_WIDTH, A_WIDTH, B_WIDTH, B_WIDTH, B_WIDTH, B_WIDTH,
             D_MODEL, D_MODEL)
    offs = np.concatenate([[0], np.cumsum(sizes)])
    w_qa, w_ka, w_va, w_za, w_qb, w_kb, w_vb, w_zb, w_ga, w_gb = [
        w[:, offs[i]:offs[i + 1]] for i in range(len(sizes))]
    bf = jnp.bfloat16
    w_rm = jnp.concatenate([w_ka, w_kb], axis=1).astype(bf)
    w_dm_t = jnp.concatenate([w_qa * ATTN_SCALE, w_va, w_qb * ATTN_SCALE, w_vb], axis=1).T.astype(bf)
    w_zg = jnp.concatenate([w_za, w_zb, w_ga, w_gb], axis=1).astype(bf)
    bias_a, sink_a, b_own, b_prev, c31 = _bias_tables(rel_table, sinks[0])

    krm, dm, kmean = _proj(x, w_rm, w_dm_t)
    kmean = kmean.reshape(B, S // MOBA_BLOCK, B_WIDTH)
    oa = _swa(dm, krm, bias_a, sink_a)
    ob = _moba(dm, krm, kmean, b_own, b_prev, c31)
    y = _final(x.reshape(B * S, D), oa.reshape(B * S, A_WIDTH), ob.reshape(B * S, B_WIDTH),
               w_zg, b_gate[0][None, :], w_out_a[0].astype(bf), w_out_b[0].astype(bf),
               w_out[0].astype(bf), ln_gamma[0][None, :], ln_beta[0][None, :])
    return y.reshape(B, S, D)
```

```python
import functools
import math

import numpy as np
import jax
import jax.numpy as jnp
from jax import lax
from jax.experimental import pallas as pl
from jax.experimental.pallas import tpu as pltpu

D_MODEL = 1024
HEAD_DIM = 64
A_HEADS = 8
A_KV_HEADS = 2
A_GROUP = A_HEADS // A_KV_HEADS
A_WIDTH = A_HEADS * HEAD_DIM
A_KV_WIDTH = A_KV_HEADS * HEAD_DIM
WINDOW = 128
B_HEADS = 8
B_WIDTH = B_HEADS * HEAD_DIM
MOBA_BLOCK = 256
MOBA_TOPK = 3
N_BUCKETS = 32
MAX_DISTANCE = 128
DEPTH = 1
DN_ALPHA = (2.0 * DEPTH) ** 0.25
LN_EPS = 1e-5
NEG = -1e30
ATTN_SCALE = HEAD_DIM ** -0.5

DM_QA, DM_VA, DM_QB, DM_VB = 0, A_WIDTH, A_WIDTH + A_KV_WIDTH, A_WIDTH + A_KV_WIDTH + B_WIDTH
DM_ROWS = DM_VB + B_WIDTH
RM_COLS = A_KV_WIDTH + B_WIDTH
LANE = 128
PROJ_TOKENS = 512
SWA_TOKENS = 256
FINAL_TOKENS = 256
FAR_CHUNK = 4
VMEM_LIMIT = 56 * 1024 * 1024


def _t5_bucket_np(dist):
    max_exact = N_BUCKETS // 2
    n = np.maximum(dist, 0)
    nf = np.maximum(n, 1).astype(np.float32)
    large = max_exact + (np.log(nf / max_exact) / math.log(MAX_DISTANCE / max_exact)
                         * (N_BUCKETS - max_exact)).astype(np.int32)
    large = np.minimum(large, N_BUCKETS - 1)
    return np.where(n < max_exact, n, large).astype(np.int32)


def _proj_kernel(x_ref, wrm_ref, wdm_ref, krm_ref, dm_ref, kmean_ref):
    xb = x_ref[0].astype(jnp.bfloat16)
    rm = jnp.dot(xb, wrm_ref[...], preferred_element_type=jnp.float32)
    krm_ref[0] = rm.astype(jnp.bfloat16)
    for c in range(PROJ_TOKENS // MOBA_BLOCK):
        kb = rm[c * MOBA_BLOCK:(c + 1) * MOBA_BLOCK, A_KV_WIDTH:]
        kmean_ref[0, 0, c:c + 1, :] = jnp.sum(kb, axis=0, keepdims=True) * (1.0 / MOBA_BLOCK)
    dm = lax.dot_general(wdm_ref[...], xb, (((1,), (1,)), ((), ())),
                         preferred_element_type=jnp.float32)
    dmb = dm.astype(jnp.bfloat16)
    for c in range(PROJ_TOKENS // MOBA_BLOCK):
        dm_ref[0, c] = dmb[:, c * MOBA_BLOCK:(c + 1) * MOBA_BLOCK]


def _proj(x, w_rm, w_dm_t):
    B, S, D = x.shape
    nt = S // PROJ_TOKENS
    cpt = PROJ_TOKENS // MOBA_BLOCK
    return pl.pallas_call(
        _proj_kernel,
        grid=(B, nt),
        in_specs=[
            pl.BlockSpec((1, PROJ_TOKENS, D), lambda b, t: (b, t, 0)),
            pl.BlockSpec((D, RM_COLS), lambda b, t: (0, 0)),
            pl.BlockSpec((DM_ROWS, D), lambda b, t: (0, 0)),
        ],
        out_specs=[
            pl.BlockSpec((1, PROJ_TOKENS, RM_COLS), lambda b, t: (b, t, 0)),
            pl.BlockSpec((1, cpt, DM_ROWS, MOBA_BLOCK), lambda b, t: (b, t, 0, 0)),
            pl.BlockSpec((1, 1, cpt, B_WIDTH), lambda b, t: (b, t, 0, 0)),
        ],
        out_shape=[
            jax.ShapeDtypeStruct((B, S, RM_COLS), jnp.bfloat16),
            jax.ShapeDtypeStruct((B, S // MOBA_BLOCK, DM_ROWS, MOBA_BLOCK), jnp.bfloat16),
            jax.ShapeDtypeStruct((B, nt, cpt, B_WIDTH), jnp.float32),
        ],
        compiler_params=pltpu.CompilerParams(
            dimension_semantics=("parallel", "parallel"), vmem_limit_bytes=VMEM_LIMIT),
        name="proj",
    )(x, w_rm, w_dm_t)


def _swa_kernel(qt_ref, vcur_ref, vprev_ref, kcur_ref, kprev_ref, bias_ref, sink_ref, o_ref):
    t = pl.program_id(1)
    qt = qt_ref[0, 0]
    vcat = jnp.concatenate([vprev_ref[0, 0][:, WINDOW:], vcur_ref[0, 0]], axis=1)
    kcat = jnp.concatenate([kprev_ref[0], kcur_ref[0]], axis=0)
    kidx = lax.broadcasted_iota(jnp.int32, (2 * WINDOW, A_GROUP * WINDOW), 0)
    zeros_q = jnp.zeros((HEAD_DIM, A_GROUP * WINDOW), jnp.bfloat16)
    for w in range(SWA_TOKENS // WINDOW):
        k2 = kcat[w * WINDOW:(w + 2) * WINDOW, :]
        v2 = vcat[:, w * WINDOW:(w + 2) * WINDOW]
        outs = []
        for g in range(A_KV_HEADS):
            q4 = jnp.concatenate(
                [qt[(g * A_GROUP + i) * HEAD_DIM:(g * A_GROUP + i + 1) * HEAD_DIM,
                    w * WINDOW:(w + 1) * WINDOW] for i in range(A_GROUP)], axis=1)
            qpad = jnp.concatenate([q4, zeros_q] if g == 0 else [zeros_q, q4], axis=0)
            s = jnp.dot(k2, qpad, preferred_element_type=jnp.float32) + bias_ref[g]
            if w == 0:
                s = jnp.where((t == 0) & (kidx < WINDOW), NEG, s)
            sink = sink_ref[g]
            m = jnp.maximum(jnp.max(s, axis=0, keepdims=True), sink)
            p = jnp.exp(s - m)
            l = jnp.sum(p, axis=0, keepdims=True) + jnp.exp(sink - m)
            o = jnp.dot(v2[g * HEAD_DIM:(g + 1) * HEAD_DIM, :], p.astype(jnp.bfloat16),
                        preferred_element_type=jnp.float32)
            o = o / l
            outs += [o[:, i * WINDOW:(i + 1) * WINDOW] for i in range(A_GROUP)]
        ot = jnp.concatenate(outs, axis=0)
        o_ref[0, w * WINDOW:(w + 1) * WINDOW, :] = ot.T.astype(jnp.bfloat16)


def _swa(dm, krm, bias_a, sink_a):
    B, NB = dm.shape[0], dm.shape[1]
    S = NB * MOBA_BLOCK
    va_blk = DM_VA // A_KV_WIDTH
    return pl.pallas_call(
        _swa_kernel,
        grid=(B, NB),
        in_specs=[
            pl.BlockSpec((1, 1, A_WIDTH, SWA_TOKENS), lambda b, t: (b, t, 0, 0)),
            pl.BlockSpec((1, 1, A_KV_WIDTH, SWA_TOKENS), lambda b, t: (b, t, va_blk, 0)),
            pl.BlockSpec((1, 1, A_KV_WIDTH, SWA_TOKENS),
                         lambda b, t: (b, jnp.maximum(t - 1, 0), va_blk, 0)),
            pl.BlockSpec((1, SWA_TOKENS, A_KV_WIDTH), lambda b, t: (b, t, 0)),
            pl.BlockSpec((1, WINDOW, A_KV_WIDTH),
                         lambda b, t: (b, jnp.maximum(2 * t - 1, 0), 0)),
            pl.BlockSpec((A_KV_HEADS, 2 * WINDOW, A_GROUP * WINDOW), lambda b, t: (0, 0, 0)),
            pl.BlockSpec((A_KV_HEADS, 1, A_GROUP * WINDOW), lambda b, t: (0, 0, 0)),
        ],
        out_specs=pl.BlockSpec((1, SWA_TOKENS, A_WIDTH), lambda b, t: (b, t, 0)),
        out_shape=jax.ShapeDtypeStruct((B, S, A_WIDTH), jnp.bfloat16),
        compiler_params=pltpu.CompilerParams(
            dimension_semantics=("parallel", "parallel"), vmem_limit_bytes=VMEM_LIMIT),
        name="swa",
    )(dm, dm, dm, krm, krm, bias_a, sink_a)


def _moba_kernel(c31_ref, qt_ref, k_ref, vt_ref, kmean_ref, bnear_ref, o_ref,
                 sel_ref, bmax_ref, s_ref):
    pr = pl.program_id(1)
    n = pl.program_id(2)
    nblk = kmean_ref.shape[1]
    heads = (0, 1)
    qt = qt_ref[0, 0]
    row_head = lax.broadcasted_iota(jnp.int32, qt.shape, 0) // HEAD_DIM
    blk = lax.broadcasted_iota(jnp.int32, (nblk, MOBA_BLOCK), 0)
    km = kmean_ref[0].astype(jnp.bfloat16)
    c31 = [c31_ref[2 * pr + r] for r in heads]
    qm = [jnp.where(row_head == r, qt, jnp.zeros_like(qt)) for r in heads]

    def k_blk(j, nb=1):
        return k_ref[0, pl.ds(pl.multiple_of(j * MOBA_BLOCK, MOBA_BLOCK), nb * MOBA_BLOCK), :]

    def vt_blk(j, r):
        return vt_ref[0, j, r * HEAD_DIM:(r + 1) * HEAD_DIM, :]

    on_prev = []
    for r in heads:
        gate = jnp.dot(km, qm[r], preferred_element_type=jnp.float32)
        avail = blk < n
        sel = jnp.zeros((nblk, MOBA_BLOCK), jnp.float32)
        for _ in range(MOBA_TOPK):
            gm = jnp.where(avail, gate, -jnp.inf)
            mx = jnp.max(gm, axis=0, keepdims=True)
            cand = jnp.where(avail & (gm == mx), blk, nblk)
            pick = blk == jnp.min(cand, axis=0, keepdims=True)
            sel = jnp.where(pick, 1.0, sel)
            avail = avail & jnp.logical_not(pick)
        sel_ref[r] = jnp.where(blk < n - 1, sel, 0.0)
        on_prev.append(jnp.max(jnp.where(blk == n - 1, sel, 0.0), axis=0, keepdims=True) > 0.5)

    jp = jnp.maximum(n - 1, 0)
    k_near = jnp.concatenate([k_blk(jp), k_blk(n)], axis=0)
    s_near, m_near = [], []
    for r in heads:
        s = jnp.dot(k_near, qm[r], preferred_element_type=jnp.float32) + bnear_ref[r]
        s_near.append(s)
        m_near.append(jnp.maximum(
            jnp.max(s[MOBA_BLOCK:], axis=0, keepdims=True),
            jnp.where(on_prev[r], jnp.max(s[:MOBA_BLOCK], axis=0, keepdims=True), -jnp.inf)))
        bmax_ref[r] = jnp.full((nblk, MOBA_BLOCK), -jnp.inf, jnp.float32)

    trips = (jnp.maximum(n - 1, 0) + FAR_CHUNK - 1) // FAR_CHUNK

    def chunk_rows(it):
        return pl.ds(pl.multiple_of(it * (FAR_CHUNK * MOBA_BLOCK), MOBA_BLOCK), FAR_CHUNK * MOBA_BLOCK)

    def logits_body(it, _):
        kc = k_blk(it * FAR_CHUNK, FAR_CHUNK)
        for r in heads:
            s = jnp.dot(kc, qm[r], preferred_element_type=jnp.float32)
            s_ref[r, chunk_rows(it), :] = s
            for i in range(FAR_CHUNK):
                bmax_ref[r, pl.ds(it * FAR_CHUNK + i, 1), :] = jnp.max(
                    s[i * MOBA_BLOCK:(i + 1) * MOBA_BLOCK], axis=0, keepdims=True)
        return 0

    lax.fori_loop(0, trips, logits_body, 0)

    carry, shift_far = [], []
    for r in heads:
        m_far = jnp.max(jnp.where(sel_ref[r] > 0.5, bmax_ref[r] + c31[r], -jnp.inf),
                        axis=0, keepdims=True)
        m = jnp.maximum(m_near[r], m_far)
        shift_far.append(m - c31[r])
        s = s_near[r]
        p_prev = jnp.exp(s[:MOBA_BLOCK] - jnp.where(on_prev[r], m, -NEG))
        p_own = jnp.exp(s[MOBA_BLOCK:] - m)
        l = jnp.sum(p_prev, axis=0, keepdims=True) + jnp.sum(p_own, axis=0, keepdims=True)
        p = jnp.concatenate([p_prev, p_own], axis=0).astype(jnp.bfloat16)
        vt = jnp.concatenate([vt_blk(jp, r), vt_blk(n, r)], axis=1)
        acc = jnp.dot(vt, p, preferred_element_type=jnp.float32)
        carry += [l, acc]

    def pv_body(it, carry):
        out = []
        for r in heads:
            l, acc = carry[2 * r:2 * r + 2]
            s = s_ref[r, chunk_rows(it), :]
            ps = []
            for i in range(FAR_CHUNK):
                on = sel_ref[r, pl.ds(it * FAR_CHUNK + i, 1), :] > 0.5
                p = jnp.exp(s[i * MOBA_BLOCK:(i + 1) * MOBA_BLOCK] - jnp.where(on, shift_far[r], -NEG))
                l = l + jnp.sum(p, axis=0, keepdims=True)
                ps.append(p.astype(jnp.bfloat16))
            vt = jnp.concatenate([vt_blk(it * FAR_CHUNK + i, r) for i in range(FAR_CHUNK)], axis=1)
            acc = acc + jnp.dot(vt, jnp.concatenate(ps, axis=0), preferred_element_type=jnp.float32)
            out += [l, acc]
        return tuple(out)

    carry = lax.fori_loop(0, trips, pv_body, tuple(carry))
    ot = jnp.concatenate([carry[2 * r + 1] / carry[2 * r] for r in heads], axis=0)
    o_ref[0] = ot.T.astype(jnp.bfloat16)


def _moba(dm, krm, kmean, bias_near, c31):
    B, NB = dm.shape[0], dm.shape[1]
    S = NB * MOBA_BLOCK
    pairs = B_HEADS // 2
    qb_blk = DM_QB // LANE
    vb_blk = DM_VB // LANE
    kb_blk = A_KV_WIDTH // LANE
    return pl.pallas_call(
        _moba_kernel,
        grid=(B, pairs, NB),
        in_specs=[
            pl.BlockSpec(memory_space=pltpu.SMEM),
            pl.BlockSpec((1, 1, LANE, MOBA_BLOCK), lambda b, p, n: (b, n, qb_blk + p, 0)),
            pl.BlockSpec((1, S, LANE), lambda b, p, n: (b, 0, kb_blk + p)),
            pl.BlockSpec((1, NB, LANE, MOBA_BLOCK), lambda b, p, n: (b, 0, vb_blk + p, 0)),
            pl.BlockSpec((1, NB, LANE), lambda b, p, n: (b, 0, p)),
            pl.BlockSpec((2, 2 * MOBA_BLOCK, MOBA_BLOCK), lambda b, p, n: (p, 0, 0)),
        ],
        out_specs=pl.BlockSpec((1, MOBA_BLOCK, LANE), lambda b, p, n: (b, n, p)),
        out_shape=jax.ShapeDtypeStruct((B, S, B_WIDTH), jnp.bfloat16),
        scratch_shapes=[pltpu.VMEM((2, NB, MOBA_BLOCK), jnp.float32),
                        pltpu.VMEM((2, NB, MOBA_BLOCK), jnp.float32),
                        pltpu.VMEM((2, S, MOBA_BLOCK), jnp.float32)],
        compiler_params=pltpu.CompilerParams(
            dimension_semantics=("parallel", "parallel", "arbitrary"),
            vmem_limit_bytes=VMEM_LIMIT),
        name="moba",
    )(c31, dm, krm, dm, kmean, bias_near)


def _final_kernel(x_ref, oa_ref, ob_ref, wzg_ref, bg_ref, woa_ref, wob_ref, wout_ref,
                  gamma_ref, beta_ref, y_ref):
    x = x_ref[...]
    zg = jnp.dot(x.astype(jnp.bfloat16), wzg_ref[...], preferred_element_type=jnp.float32)
    za = zg[:, :A_WIDTH]
    zb = zg[:, A_WIDTH:A_WIDTH + B_WIDTH]
    g = zg[:, A_WIDTH + B_WIDTH:] + bg_ref[...]
    ua = oa_ref[...].astype(jnp.float32) * (za * jax.nn.sigmoid(za))
    ub = ob_ref[...].astype(jnp.float32) * (zb * jax.nn.sigmoid(zb))
    ya = jnp.dot(ua.astype(jnp.bfloat16), woa_ref[...], preferred_element_type=jnp.float32)
    yb = jnp.dot(ub.astype(jnp.bfloat16), wob_ref[...], preferred_element_type=jnp.float32)
    gs = jax.nn.sigmoid(g)
    merged = gs[:, :D_MODEL] * ya + gs[:, D_MODEL:] * yb
    out = jnp.dot(merged.astype(jnp.bfloat16), wout_ref[...], preferred_element_type=jnp.float32)
    r = DN_ALPHA * x + out
    mu = jnp.mean(r, axis=-1, keepdims=True)
    rc = r - mu
    var = jnp.mean(rc * rc, axis=-1, keepdims=True)
    y_ref[...] = rc * lax.rsqrt(var + LN_EPS) * gamma_ref[...] + beta_ref[...]


def _final(x2, oa2, ob2, w_zg, b_gate, w_oa, w_ob, w_out, gamma, beta):
    N, D = x2.shape
    T = FINAL_TOKENS
    full = lambda a: pl.BlockSpec(a.shape, lambda i: (0,) * a.ndim)
    return pl.pallas_call(
        _final_kernel,
        grid=(N // T,),
        in_specs=[
            pl.BlockSpec((T, D), lambda i: (i, 0)),
            pl.BlockSpec((T, A_WIDTH), lambda i: (i, 0)),
            pl.BlockSpec((T, B_WIDTH), lambda i: (i, 0)),
            full(w_zg), full(b_gate), full(w_oa), full(w_ob), full(w_out), full(gamma), full(beta),
        ],
        out_specs=pl.BlockSpec((T, D), lambda i: (i, 0)),
        out_shape=jax.ShapeDtypeStruct((N, D), jnp.float32),
        compiler_params=pltpu.CompilerParams(
            dimension_semantics=("parallel",), vmem_limit_bytes=VMEM_LIMIT),
        name="final",
    )(x2, oa2, ob2, w_zg, b_gate, w_oa, w_ob, w_out, gamma, beta)


def _toeplitz(v, rows, col0, ncols):
    H, L = v.shape
    assert col0 >= rows - 1 and col0 + ncols <= L - 1
    t = jnp.tile(v, (1, rows))[:, :rows * (L - 1)].reshape(H, rows, L - 1)
    return t[:, :, col0:col0 + ncols]


def _bias_tables(rel_table, sinks):
    table_a = rel_table[:, :A_HEADS].astype(jnp.float32)
    table_b = rel_table[:, A_HEADS:].astype(jnp.float32)
    d_a = np.arange(-2 * WINDOW, 3 * WINDOW)
    ok_a = (d_a >= 0) & (d_a < WINDOW)
    v_a = jnp.where(ok_a[:, None], table_a[_t5_bucket_np(d_a)], NEG).T
    ba = _toeplitz(v_a, 2 * WINDOW, 3 * WINDOW, WINDOW)
    ba = ba.reshape(A_KV_HEADS, A_GROUP, 2 * WINDOW, WINDOW)
    ba = ba.transpose(0, 2, 1, 3).reshape(A_KV_HEADS, 2 * WINDOW, A_GROUP * WINDOW)
    sink = jnp.broadcast_to(sinks.astype(jnp.float32).reshape(A_KV_HEADS, 1, A_GROUP, 1),
                            (A_KV_HEADS, 1, A_GROUP, WINDOW)).reshape(A_KV_HEADS, 1, A_GROUP * WINDOW)
    d_b = np.arange(-MOBA_BLOCK, 3 * MOBA_BLOCK)
    v_b = jnp.where((d_b >= 0)[:, None], table_b[_t5_bucket_np(d_b)], NEG).T
    b_own = _toeplitz(v_b, MOBA_BLOCK, MOBA_BLOCK, MOBA_BLOCK)
    b_prev = _toeplitz(v_b, MOBA_BLOCK, 2 * MOBA_BLOCK, MOBA_BLOCK)
    b_near = jnp.concatenate([b_prev, b_own], axis=1)
    c31 = table_b[N_BUCKETS - 1]
    return ba, sink, b_near, c31


def kernel(x, w_in, b_gate, sinks, rel_table, w_out_a, w_out_b, w_out, ln_gamma, ln_beta):
    B, S, D = x.shape
    assert (D, w_in.shape[0]) == (D_MODEL, DEPTH) and S % PROJ_TOKENS == 0
    w = w_in[0]
    sizes = (A_WIDTH, A_KV_WIDTH, A_KV_WIDTH, A_WIDTH, B_WIDTH, B_WIDTH, B_WIDTH, B_WIDTH,
             D_MODEL, D_MODEL)
    offs = np.concatenate([[0], np.cumsum(sizes)])
    w_qa, w_ka, w_va, w_za, w_qb, w_kb, w_vb, w_zb, w_ga, w_gb = [
        w[:, offs[i]:offs[i + 1]] for i in range(len(sizes))]
    bf = jnp.bfloat16
    w_rm = jnp.concatenate([w_ka, w_kb], axis=1).astype(bf)
    w_dm_t = jnp.concatenate([w_qa * ATTN_SCALE, w_va, w_qb * ATTN_SCALE, w_vb], axis=1).T.astype(bf)
    w_zg = jnp.concatenate([w_za, w_zb, w_ga, w_gb], axis=1).astype(bf)
    bias_a, sink_a, b_near, c31 = _bias_tables(rel_table, sinks[0])

    krm, dm, kmean = _proj(x, w_rm, w_dm_t)
    kmean = kmean.reshape(B, S // MOBA_BLOCK, B_WIDTH)
    oa = _swa(dm, krm, bias_a, sink_a)
    ob = _moba(dm, krm, kmean, b_near, c31)
    y = _final(x.reshape(B * S, D), oa.reshape(B * S, A_WIDTH), ob.reshape(B * S, B_WIDTH),
               w_zg, b_gate[0][None, :], w_out_a[0].astype(bf), w_out_b[0].astype(bf),
               w_out[0].astype(bf), ln_gamma[0][None, :], ln_beta[0][None, :])
    return y.reshape(B, S, D)
```

```python
import functools
import math

import numpy as np
import jax
import jax.numpy as jnp
from jax import lax
from jax.experimental import pallas as pl
from jax.experimental.pallas import tpu as pltpu

D_MODEL = 1024
HEAD_DIM = 64
A_HEADS = 8
A_KV_HEADS = 2
A_GROUP = A_HEADS // A_KV_HEADS
A_WIDTH = A_HEADS * HEAD_DIM
A_KV_WIDTH = A_KV_HEADS * HEAD_DIM
WINDOW = 128
B_HEADS = 8
B_WIDTH = B_HEADS * HEAD_DIM
MOBA_BLOCK = 256
MOBA_TOPK = 3
N_BUCKETS = 32
MAX_DISTANCE = 128
DEPTH = 1
DN_ALPHA = (2.0 * DEPTH) ** 0.25
LN_EPS = 1e-5
NEG = -1e30
ATTN_SCALE = HEAD_DIM ** -0.5

DM_QA, DM_VA, DM_QB, DM_VB = 0, A_WIDTH, A_WIDTH + A_KV_WIDTH, A_WIDTH + A_KV_WIDTH + B_WIDTH
DM_ROWS = DM_VB + B_WIDTH
RM_COLS = A_KV_WIDTH + B_WIDTH
LANE = 128
PROJ_TOKENS = 512
SWA_TOKENS = 256
FINAL_TOKENS = 256
FAR_CHUNK = 2
ONES_ROWS = 16
LOG2E = math.log2(math.e)
VMEM_LIMIT = 56 * 1024 * 1024


def _t5_bucket_np(dist):
    max_exact = N_BUCKETS // 2
    n = np.maximum(dist, 0)
    nf = np.maximum(n, 1).astype(np.float32)
    large = max_exact + (np.log(nf / max_exact) / math.log(MAX_DISTANCE / max_exact)
                         * (N_BUCKETS - max_exact)).astype(np.int32)
    large = np.minimum(large, N_BUCKETS - 1)
    return np.where(n < max_exact, n, large).astype(np.int32)


def _proj_kernel(x_ref, wrm_ref, wdm_ref, krm_ref, dm_ref, kmean_ref):
    xb = x_ref[0].astype(jnp.bfloat16)
    rm = jnp.dot(xb, wrm_ref[...], preferred_element_type=jnp.float32)
    krm_ref[0] = rm.astype(jnp.bfloat16)
    for c in range(PROJ_TOKENS // MOBA_BLOCK):
        kb = rm[c * MOBA_BLOCK:(c + 1) * MOBA_BLOCK, A_KV_WIDTH:]
        kmean_ref[0, 0, c:c + 1, :] = jnp.sum(kb, axis=0, keepdims=True) * (1.0 / MOBA_BLOCK)
    dm = lax.dot_general(wdm_ref[...], xb, (((1,), (1,)), ((), ())),
                         preferred_element_type=jnp.float32)
    dmb = dm.astype(jnp.bfloat16)
    for c in range(PROJ_TOKENS // MOBA_BLOCK):
        dm_ref[0, c] = dmb[:, c * MOBA_BLOCK:(c + 1) * MOBA_BLOCK]


def _proj(x, w_rm, w_dm_t):
    B, S, D = x.shape
    nt = S // PROJ_TOKENS
    cpt = PROJ_TOKENS // MOBA_BLOCK
    return pl.pallas_call(
        _proj_kernel,
        grid=(B, nt),
        in_specs=[
            pl.BlockSpec((1, PROJ_TOKENS, D), lambda b, t: (b, t, 0)),
            pl.BlockSpec((D, RM_COLS), lambda b, t: (0, 0)),
            pl.BlockSpec((DM_ROWS, D), lambda b, t: (0, 0)),
        ],
        out_specs=[
            pl.BlockSpec((1, PROJ_TOKENS, RM_COLS), lambda b, t: (b, t, 0)),
            pl.BlockSpec((1, cpt, DM_ROWS, MOBA_BLOCK), lambda b, t: (b, t, 0, 0)),
            pl.BlockSpec((1, 1, cpt, B_WIDTH), lambda b, t: (b, t, 0, 0)),
        ],
        out_shape=[
            jax.ShapeDtypeStruct((B, S, RM_COLS), jnp.bfloat16),
            jax.ShapeDtypeStruct((B, S // MOBA_BLOCK, DM_ROWS, MOBA_BLOCK), jnp.bfloat16),
            jax.ShapeDtypeStruct((B, nt, cpt, B_WIDTH), jnp.float32),
        ],
        compiler_params=pltpu.CompilerParams(
            dimension_semantics=("parallel", "parallel"), vmem_limit_bytes=VMEM_LIMIT),
        name="proj",
    )(x, w_rm, w_dm_t)


def _swa_kernel(qt_ref, vcur_ref, vprev_ref, kcur_ref, kprev_ref, bias_ref, sink_ref, o_ref):
    t = pl.program_id(1)
    qt = qt_ref[0, 0]
    vcat = jnp.concatenate([vprev_ref[0, 0][:, WINDOW:], vcur_ref[0, 0]], axis=1)
    kcat = jnp.concatenate([kprev_ref[0], kcur_ref[0]], axis=0)
    kidx = lax.broadcasted_iota(jnp.int32, (2 * WINDOW, A_GROUP * WINDOW), 0)
    zeros_q = jnp.zeros((HEAD_DIM, A_GROUP * WINDOW), jnp.bfloat16)
    for w in range(SWA_TOKENS // WINDOW):
        k2 = kcat[w * WINDOW:(w + 2) * WINDOW, :]
        v2 = vcat[:, w * WINDOW:(w + 2) * WINDOW]
        outs = []
        for g in range(A_KV_HEADS):
            q4 = jnp.concatenate(
                [qt[(g * A_GROUP + i) * HEAD_DIM:(g * A_GROUP + i + 1) * HEAD_DIM,
                    w * WINDOW:(w + 1) * WINDOW] for i in range(A_GROUP)], axis=1)
            qpad = jnp.concatenate([q4, zeros_q] if g == 0 else [zeros_q, q4], axis=0)
            s = jnp.dot(k2, qpad, preferred_element_type=jnp.float32) + bias_ref[g]
            if w == 0:
                s = jnp.where((t == 0) & (kidx < WINDOW), NEG, s)
            sink = sink_ref[g]
            m = jnp.maximum(jnp.max(s, axis=0, keepdims=True), sink)
            p = jnp.exp2(s - m)
            l = jnp.sum(p, axis=0, keepdims=True) + jnp.exp2(sink - m)
            o = jnp.dot(v2[g * HEAD_DIM:(g + 1) * HEAD_DIM, :], p.astype(jnp.bfloat16),
                        preferred_element_type=jnp.float32)
            o = o / l
            outs += [o[:, i * WINDOW:(i + 1) * WINDOW] for i in range(A_GROUP)]
        ot = jnp.concatenate(outs, axis=0)
        o_ref[0, w * WINDOW:(w + 1) * WINDOW, :] = ot.T.astype(jnp.bfloat16)


def _swa(dm, krm, bias_a, sink_a):
    B, NB = dm.shape[0], dm.shape[1]
    S = NB * MOBA_BLOCK
    va_blk = DM_VA // A_KV_WIDTH
    return pl.pallas_call(
        _swa_kernel,
        grid=(B, NB),
        in_specs=[
            pl.BlockSpec((1, 1, A_WIDTH, SWA_TOKENS), lambda b, t: (b, t, 0, 0)),
            pl.BlockSpec((1, 1, A_KV_WIDTH, SWA_TOKENS), lambda b, t: (b, t, va_blk, 0)),
            pl.BlockSpec((1, 1, A_KV_WIDTH, SWA_TOKENS),
                         lambda b, t: (b, jnp.maximum(t - 1, 0), va_blk, 0)),
            pl.BlockSpec((1, SWA_TOKENS, A_KV_WIDTH), lambda b, t: (b, t, 0)),
            pl.BlockSpec((1, WINDOW, A_KV_WIDTH),
                         lambda b, t: (b, jnp.maximum(2 * t - 1, 0), 0)),
            pl.BlockSpec((A_KV_HEADS, 2 * WINDOW, A_GROUP * WINDOW), lambda b, t: (0, 0, 0)),
            pl.BlockSpec((A_KV_HEADS, 1, A_GROUP * WINDOW), lambda b, t: (0, 0, 0)),
        ],
        out_specs=pl.BlockSpec((1, SWA_TOKENS, A_WIDTH), lambda b, t: (b, t, 0)),
        out_shape=jax.ShapeDtypeStruct((B, S, A_WIDTH), jnp.bfloat16),
        compiler_params=pltpu.CompilerParams(
            dimension_semantics=("parallel", "parallel"), vmem_limit_bytes=VMEM_LIMIT),
        name="swa",
    )(dm, dm, dm, krm, krm, bias_a, sink_a)


def _moba_kernel(c31_ref, qt_ref, k_ref, vt_ref, kmean_ref, bnear_ref, o_ref,
                 sel_ref, sa_ref, sb_ref):
    pr = pl.program_id(1)
    n = pl.program_id(2)
    nblk = kmean_ref.shape[1]
    heads = (0, 1)
    qt = qt_ref[0, 0]
    row_head = lax.broadcasted_iota(jnp.int32, qt.shape, 0) // HEAD_DIM
    blk = lax.broadcasted_iota(jnp.int32, (nblk, MOBA_BLOCK), 0)
    km = kmean_ref[0].astype(jnp.bfloat16)
    c31 = [c31_ref[2 * pr + r] for r in heads]
    qm = [jnp.where(row_head == r, qt, jnp.zeros_like(qt)) for r in heads]

    def k_blk(j, nb=1):
        return k_ref[0, pl.ds(pl.multiple_of(j * MOBA_BLOCK, MOBA_BLOCK), nb * MOBA_BLOCK), :]

    def vt_blk(j, r):
        return vt_ref[0, j, r * HEAD_DIM:(r + 1) * HEAD_DIM, :]

    on_prev = []
    for r in heads:
        gate = jnp.dot(km, qm[r], preferred_element_type=jnp.float32)
        avail = blk < n
        sel = jnp.zeros((nblk, MOBA_BLOCK), jnp.float32)
        for _ in range(MOBA_TOPK):
            gm = jnp.where(avail, gate, -jnp.inf)
            mx = jnp.max(gm, axis=0, keepdims=True)
            cand = jnp.where(avail & (gm == mx), blk, nblk)
            pick = blk == jnp.min(cand, axis=0, keepdims=True)
            sel = jnp.where(pick, 1.0, sel)
            avail = avail & jnp.logical_not(pick)
        sel_ref[r] = jnp.where(blk < n - 1, sel, 0.0)
        on_prev.append(jnp.max(jnp.where(blk == n - 1, sel, 0.0), axis=0, keepdims=True) > 0.5)

    jp = jnp.maximum(n - 1, 0)
    k_near = jnp.concatenate([k_blk(jp), k_blk(n)], axis=0)
    ones_rows = jnp.ones((ONES_ROWS, MOBA_BLOCK), jnp.bfloat16)

    def vt_ext(blocks, r):
        return jnp.concatenate(
            [jnp.concatenate([vt_blk(j, r), ones_rows], axis=0) for j in blocks], axis=1)

    def far_logits(c, buf_ref):
        kc = k_blk(c * FAR_CHUNK, FAR_CHUNK)
        out = []
        for r in heads:
            s = jnp.dot(kc, qm[r], preferred_element_type=jnp.float32)
            buf_ref[r] = s
            mc = jnp.full((1, MOBA_BLOCK), -jnp.inf, jnp.float32)
            for i in range(FAR_CHUNK):
                on = sel_ref[r, pl.ds(c * FAR_CHUNK + i, 1), :] > 0.5
                bm = jnp.max(s[i * MOBA_BLOCK:(i + 1) * MOBA_BLOCK], axis=0, keepdims=True)
                mc = jnp.maximum(mc, jnp.where(on, bm + c31[r], -jnp.inf))
            out.append(mc)
        return out

    def far_softmax(c, buf_ref, state, mcs):
        out = []
        for r in heads:
            m, acc = state[2 * r:2 * r + 2]
            m_new = jnp.maximum(m, mcs[r])
            alpha = jnp.exp2(m - m_new)
            shift = m_new - c31[r]
            s = buf_ref[r]
            ps = []
            for i in range(FAR_CHUNK):
                on = sel_ref[r, pl.ds(c * FAR_CHUNK + i, 1), :] > 0.5
                p = jnp.exp2(s[i * MOBA_BLOCK:(i + 1) * MOBA_BLOCK] - jnp.where(on, shift, -NEG))
                ps.append(p.astype(jnp.bfloat16))
            vt = vt_ext([c * FAR_CHUNK + i for i in range(FAR_CHUNK)], r)
            acc = alpha * acc + jnp.dot(vt, jnp.concatenate(ps, axis=0),
                                        preferred_element_type=jnp.float32)
            out += [m_new, acc]
        return out

    mc0 = far_logits(0, sa_ref)
    carry = []
    for r in heads:
        s = jnp.dot(k_near, qm[r], preferred_element_type=jnp.float32) + bnear_ref[r]
        m = jnp.maximum(
            jnp.max(s[MOBA_BLOCK:], axis=0, keepdims=True),
            jnp.where(on_prev[r], jnp.max(s[:MOBA_BLOCK], axis=0, keepdims=True), -jnp.inf))
        p_prev = jnp.exp2(s[:MOBA_BLOCK] - jnp.where(on_prev[r], m, -NEG))
        p_own = jnp.exp2(s[MOBA_BLOCK:] - m)
        p = jnp.concatenate([p_prev, p_own], axis=0).astype(jnp.bfloat16)
        acc = jnp.dot(vt_ext((jp, n), r), p, preferred_element_type=jnp.float32)
        carry += [m, acc]
    carry += mc0

    trips = (jnp.maximum(n - 1, 0) + 2 * FAR_CHUNK - 1) // (2 * FAR_CHUNK)
    last_chunk = nblk // FAR_CHUNK - 1

    def far_body(it, carry):
        state, mc_a = list(carry[:4]), list(carry[4:])
        c = 2 * it
        mc_b = far_logits(c + 1, sb_ref)
        state = far_softmax(c, sa_ref, state, mc_a)
        mc_a = far_logits(jnp.minimum(c + 2, last_chunk), sa_ref)
        state = far_softmax(c + 1, sb_ref, state, mc_b)
        return tuple(state + mc_a)

    carry = lax.fori_loop(0, trips, far_body, tuple(carry))
    ot = jnp.concatenate([carry[2 * r + 1][:HEAD_DIM] / carry[2 * r + 1][HEAD_DIM:HEAD_DIM + 1]
                          for r in heads], axis=0)
    o_ref[0] = ot.T.astype(jnp.bfloat16)


def _moba(dm, krm, kmean, bias_near, c31):
    B, NB = dm.shape[0], dm.shape[1]
    S = NB * MOBA_BLOCK
    pairs = B_HEADS // 2
    qb_blk = DM_QB // LANE
    vb_blk = DM_VB // LANE
    kb_blk = A_KV_WIDTH // LANE
    return pl.pallas_call(
        _moba_kernel,
        grid=(B, pairs, NB),
        in_specs=[
            pl.BlockSpec(memory_space=pltpu.SMEM),
            pl.BlockSpec((1, 1, LANE, MOBA_BLOCK), lambda b, p, n: (b, n, qb_blk + p, 0)),
            pl.BlockSpec((1, S, LANE), lambda b, p, n: (b, 0, kb_blk + p)),
            pl.BlockSpec((1, NB, LANE, MOBA_BLOCK), lambda b, p, n: (b, 0, vb_blk + p, 0)),
            pl.BlockSpec((1, NB, LANE), lambda b, p, n: (b, 0, p)),
            pl.BlockSpec((2, 2 * MOBA_BLOCK, MOBA_BLOCK), lambda b, p, n: (p, 0, 0)),
        ],
        out_specs=pl.BlockSpec((1, MOBA_BLOCK, LANE), lambda b, p, n: (b, n, p)),
        out_shape=jax.ShapeDtypeStruct((B, S, B_WIDTH), jnp.bfloat16),
        scratch_shapes=[pltpu.VMEM((2, NB, MOBA_BLOCK), jnp.float32),
                        pltpu.VMEM((2, FAR_CHUNK * MOBA_BLOCK, MOBA_BLOCK), jnp.float32),
                        pltpu.VMEM((2, FAR_CHUNK * MOBA_BLOCK, MOBA_BLOCK), jnp.float32)],
        compiler_params=pltpu.CompilerParams(
            dimension_semantics=("parallel", "parallel", "arbitrary"),
            vmem_limit_bytes=VMEM_LIMIT),
        name="moba",
    )(c31, dm, krm, dm, kmean, bias_near)


def _final_kernel(x_ref, oa_ref, ob_ref, wzg_ref, bg_ref, woa_ref, wob_ref, wout_ref,
                  gamma_ref, beta_ref, y_ref):
    x = x_ref[...]
    zg = jnp.dot(x.astype(jnp.bfloat16), wzg_ref[...], preferred_element_type=jnp.float32)
    za = zg[:, :A_WIDTH]
    zb = zg[:, A_WIDTH:A_WIDTH + B_WIDTH]
    g = zg[:, A_WIDTH + B_WIDTH:] + bg_ref[...]
    ua = oa_ref[...].astype(jnp.float32) * (za * jax.nn.sigmoid(za))
    ub = ob_ref[...].astype(jnp.float32) * (zb * jax.nn.sigmoid(zb))
    ya = jnp.dot(ua.astype(jnp.bfloat16), woa_ref[...], preferred_element_type=jnp.float32)
    yb = jnp.dot(ub.astype(jnp.bfloat16), wob_ref[...], preferred_element_type=jnp.float32)
    gs = jax.nn.sigmoid(g)
    merged = gs[:, :D_MODEL] * ya + gs[:, D_MODEL:] * yb
    out = jnp.dot(merged.astype(jnp.bfloat16), wout_ref[...], preferred_element_type=jnp.float32)
    r = DN_ALPHA * x + out
    mu = jnp.mean(r, axis=-1, keepdims=True)
    rc = r - mu
    var = jnp.mean(rc * rc, axis=-1, keepdims=True)
    y_ref[...] = rc * lax.rsqrt(var + LN_EPS) * gamma_ref[...] + beta_ref[...]


def _final(x2, oa2, ob2, w_zg, b_gate, w_oa, w_ob, w_out, gamma, beta):
    N, D = x2.shape
    T = FINAL_TOKENS
    full = lambda a: pl.BlockSpec(a.shape, lambda i: (0,) * a.ndim)
    return pl.pallas_call(
        _final_kernel,
        grid=(N // T,),
        in_specs=[
            pl.BlockSpec((T, D), lambda i: (i, 0)),
            pl.BlockSpec((T, A_WIDTH), lambda i: (i, 0)),
            pl.BlockSpec((T, B_WIDTH), lambda i: (i, 0)),
            full(w_zg), full(b_gate), full(w_oa), full(w_ob), full(w_out), full(gamma), full(beta),
        ],
        out_specs=pl.BlockSpec((T, D), lambda i: (i, 0)),
        out_shape=jax.ShapeDtypeStruct((N, D), jnp.float32),
        compiler_params=pltpu.CompilerParams(
            dimension_semantics=("parallel",), vmem_limit_bytes=VMEM_LIMIT),
        name="final",
    )(x2, oa2, ob2, w_zg, b_gate, w_oa, w_ob, w_out, gamma, beta)


def _toeplitz(v, rows, col0, ncols):
    H, L = v.shape
    assert col0 >= rows - 1 and col0 + ncols <= L - 1
    t = jnp.tile(v, (1, rows))[:, :rows * (L - 1)].reshape(H, rows, L - 1)
    return t[:, :, col0:col0 + ncols]


def _bias_tables(rel_table, sinks):
    table_a = rel_table[:, :A_HEADS].astype(jnp.float32) * LOG2E
    table_b = rel_table[:, A_HEADS:].astype(jnp.float32) * LOG2E
    sinks = sinks.astype(jnp.float32) * LOG2E
    d_a = np.arange(-2 * WINDOW, 3 * WINDOW)
    ok_a = (d_a >= 0) & (d_a < WINDOW)
    v_a = jnp.where(ok_a[:, None], table_a[_t5_bucket_np(d_a)], NEG).T
    ba = _toeplitz(v_a, 2 * WINDOW, 3 * WINDOW, WINDOW)
    ba = ba.reshape(A_KV_HEADS, A_GROUP, 2 * WINDOW, WINDOW)
    ba = ba.transpose(0, 2, 1, 3).reshape(A_KV_HEADS, 2 * WINDOW, A_GROUP * WINDOW)
    sink = jnp.broadcast_to(sinks.astype(jnp.float32).reshape(A_KV_HEADS, 1, A_GROUP, 1),
                            (A_KV_HEADS, 1, A_GROUP, WINDOW)).reshape(A_KV_HEADS, 1, A_GROUP * WINDOW)
    d_b = np.arange(-MOBA_BLOCK, 3 * MOBA_BLOCK)
    v_b = jnp.where((d_b >= 0)[:, None], table_b[_t5_bucket_np(d_b)], NEG).T
    b_own = _toeplitz(v_b, MOBA_BLOCK, MOBA_BLOCK, MOBA_BLOCK)
    b_prev = _toeplitz(v_b, MOBA_BLOCK, 2 * MOBA_BLOCK, MOBA_BLOCK)
    b_near = jnp.concatenate([b_prev, b_own], axis=1)
    c31 = table_b[N_BUCKETS - 1]
    return ba, sink, b_near, c31


def kernel(x, w_in, b_gate, sinks, rel_table, w_out_a, w_out_b, w_out, ln_gamma, ln_beta):
    B, S, D = x.shape
    assert (D, w_in.shape[0]) == (D_MODEL, DEPTH) and S % PROJ_TOKENS == 0
    w = w_in[0]
    sizes = (A_WIDTH, A_KV_WIDTH, A_KV_WIDTH, A_WIDTH, B_WIDTH, B_WIDTH, B_WIDTH, B_WIDTH,
             D_MODEL, D_MODEL)
    offs = np.concatenate([[0], np.cumsum(sizes)])
    w_qa, w_ka, w_va, w_za, w_qb, w_kb, w_vb, w_zb, w_ga, w_gb = [
        w[:, offs[i]:offs[i + 1]] for i in range(len(sizes))]
    bf = jnp.bfloat16
    q_scale = ATTN_SCALE * LOG2E
    w_rm = jnp.concatenate([w_ka, w_kb], axis=1).astype(bf)
    w_dm_t = jnp.concatenate([w_qa * q_scale, w_va, w_qb * q_scale, w_vb], axis=1).T.astype(bf)
    w_zg = jnp.concatenate([w_za, w_zb, w_ga, w_gb], axis=1).astype(bf)
    bias_a, sink_a, b_near, c31 = _bias_tables(rel_table, sinks[0])

    krm, dm, kmean = _proj(x, w_rm, w_dm_t)
    kmean = kmean.reshape(B, S // MOBA_BLOCK, B_WIDTH)
    oa = _swa(dm, krm, bias_a, sink_a)
    ob = _moba(dm, krm, kmean, b_near, c31)
    y = _final(x.reshape(B * S, D), oa.reshape(B * S, A_WIDTH), ob.reshape(B * S, B_WIDTH),
               w_zg, b_gate[0][None, :], w_out_a[0].astype(bf), w_out_b[0].astype(bf),
               w_out[0].astype(bf), ln_gamma[0][None, :], ln_beta[0][None, :])
    return y.reshape(B, S, D)
```

```python
import math

import numpy as np
import jax
import jax.numpy as jnp
from jax import lax
from jax.experimental import pallas as pl
from jax.experimental.pallas import tpu as pltpu

D_MODEL = 1024
HEAD_DIM = 64
A_HEADS = 8
A_KV_HEADS = 2
A_GROUP = A_HEADS // A_KV_HEADS
A_WIDTH = A_HEADS * HEAD_DIM
A_KV_WIDTH = A_KV_HEADS * HEAD_DIM
WINDOW = 128
B_HEADS = 8
B_WIDTH = B_HEADS * HEAD_DIM
MOBA_BLOCK = 256
MOBA_TOPK = 3
N_BUCKETS = 32
MAX_DISTANCE = 128
DEPTH = 1
DN_ALPHA = (2.0 * DEPTH) ** 0.25
LN_EPS = 1e-5
NEG = -1e30
ATTN_SCALE = HEAD_DIM ** -0.5
LOG2E = math.log2(math.e)

LANE = 128
HEAD_PAD = 2 * HEAD_DIM
W_QB, W_QA, W_VB, W_VA = 0, B_WIDTH, B_WIDTH + A_WIDTH, 2 * B_WIDTH + A_WIDTH
W_ROWS = W_VA + A_KV_WIDTH
DM_QB, DM_QA = 0, B_HEADS * HEAD_PAD
DM_VB = DM_QA + A_WIDTH
DM_VA = DM_VB + B_WIDTH
DM_ROWS = DM_VA + A_KV_WIDTH
RM_COLS = A_KV_WIDTH + B_WIDTH
PROJ_TOKENS = 512
SWA_TOKENS = 256
FINAL_TOKENS = 256
FAR_CHUNK = 2
ONES_ROWS = 16
SEL_ROWS = 48
ROW_PREV, ROW_OWN = 32, 33
SEL_BLOCKS = 4
VMEM_LIMIT = 56 * 1024 * 1024


def _t5_bucket_np(dist):
    max_exact = N_BUCKETS // 2
    n = np.maximum(dist, 0)
    nf = np.maximum(n, 1).astype(np.float32)
    large = max_exact + (np.log(nf / max_exact) / math.log(MAX_DISTANCE / max_exact)
                         * (N_BUCKETS - max_exact)).astype(np.int32)
    large = np.minimum(large, N_BUCKETS - 1)
    return np.where(n < max_exact, n, large).astype(np.int32)


def _proj_kernel(x_ref, wrm_ref, wdm_ref, krm_ref, dm_ref, kmean_ref):
    xb = x_ref[0].astype(jnp.bfloat16)
    rm = jnp.dot(xb, wrm_ref[...], preferred_element_type=jnp.float32)
    krm_ref[0] = rm.astype(jnp.bfloat16)
    for c in range(PROJ_TOKENS // MOBA_BLOCK):
        kb = rm[c * MOBA_BLOCK:(c + 1) * MOBA_BLOCK, A_KV_WIDTH:]
        kmean_ref[0, 0, c:c + 1, :] = jnp.sum(kb, axis=0, keepdims=True) * (1.0 / MOBA_BLOCK)
    dm = lax.dot_general(wdm_ref[...], xb, (((1,), (1,)), ((), ())),
                         preferred_element_type=jnp.float32)
    dmb = dm.astype(jnp.bfloat16)
    zeros = jnp.zeros((HEAD_DIM, MOBA_BLOCK), jnp.bfloat16)
    for c in range(PROJ_TOKENS // MOBA_BLOCK):
        piece = dmb[:, c * MOBA_BLOCK:(c + 1) * MOBA_BLOCK]
        rows = []
        for h in range(B_HEADS):
            q = piece[W_QB + h * HEAD_DIM:W_QB + (h + 1) * HEAD_DIM]
            rows += [q, zeros] if h % 2 == 0 else [zeros, q]
        dm_ref[0, c] = jnp.concatenate(rows + [piece[W_QA:]], axis=0)


def _proj(x, w_rm, w_dm_t):
    B, S, D = x.shape
    nt = S // PROJ_TOKENS
    cpt = PROJ_TOKENS // MOBA_BLOCK
    return pl.pallas_call(
        _proj_kernel,
        grid=(B, nt),
        in_specs=[
            pl.BlockSpec((1, PROJ_TOKENS, D), lambda b, t: (b, t, 0)),
            pl.BlockSpec((D, RM_COLS), lambda b, t: (0, 0)),
            pl.BlockSpec((W_ROWS, D), lambda b, t: (0, 0)),
        ],
        out_specs=[
            pl.BlockSpec((1, PROJ_TOKENS, RM_COLS), lambda b, t: (b, t, 0)),
            pl.BlockSpec((1, cpt, DM_ROWS, MOBA_BLOCK), lambda b, t: (b, t, 0, 0)),
            pl.BlockSpec((1, 1, cpt, B_WIDTH), lambda b, t: (b, t, 0, 0)),
        ],
        out_shape=[
            jax.ShapeDtypeStruct((B, S, RM_COLS), jnp.bfloat16),
            jax.ShapeDtypeStruct((B, S // MOBA_BLOCK, DM_ROWS, MOBA_BLOCK), jnp.bfloat16),
            jax.ShapeDtypeStruct((B, nt, cpt, B_WIDTH), jnp.float32),
        ],
        compiler_params=pltpu.CompilerParams(
            dimension_semantics=("parallel", "parallel"), vmem_limit_bytes=VMEM_LIMIT),
        name="proj",
    )(x, w_rm, w_dm_t)


def _swa_kernel(qt_ref, vcur_ref, vprev_ref, kcur_ref, kprev_ref, bias_ref, sink_ref, o_ref):
    t = pl.program_id(1)
    qt = qt_ref[0, 0]
    vcat = jnp.concatenate([vprev_ref[0, 0][:, WINDOW:], vcur_ref[0, 0]], axis=1)
    kcat = jnp.concatenate([kprev_ref[0], kcur_ref[0]], axis=0)
    kidx = lax.broadcasted_iota(jnp.int32, (2 * WINDOW, A_GROUP * WINDOW), 0)
    zeros_q = jnp.zeros((HEAD_DIM, A_GROUP * WINDOW), jnp.bfloat16)
    for w in range(SWA_TOKENS // WINDOW):
        k2 = kcat[w * WINDOW:(w + 2) * WINDOW, :]
        v2 = vcat[:, w * WINDOW:(w + 2) * WINDOW]
        outs = []
        for g in range(A_KV_HEADS):
            q4 = jnp.concatenate(
                [qt[(g * A_GROUP + i) * HEAD_DIM:(g * A_GROUP + i + 1) * HEAD_DIM,
                    w * WINDOW:(w + 1) * WINDOW] for i in range(A_GROUP)], axis=1)
            qpad = jnp.concatenate([q4, zeros_q] if g == 0 else [zeros_q, q4], axis=0)
            s = jnp.dot(k2, qpad, preferred_element_type=jnp.float32) + bias_ref[g]
            if w == 0:
                s = jnp.where((t == 0) & (kidx < WINDOW), NEG, s)
            sink = sink_ref[g]
            m = jnp.maximum(jnp.max(s, axis=0, keepdims=True), sink)
            p = jnp.exp2(s - m)
            l = jnp.sum(p, axis=0, keepdims=True) + jnp.exp2(sink - m)
            o = jnp.dot(v2[g * HEAD_DIM:(g + 1) * HEAD_DIM, :], p.astype(jnp.bfloat16),
                        preferred_element_type=jnp.float32)
            o = o / l
            outs += [o[:, i * WINDOW:(i + 1) * WINDOW] for i in range(A_GROUP)]
        ot = jnp.concatenate(outs, axis=0)
        o_ref[0, w * WINDOW:(w + 1) * WINDOW, :] = ot.T.astype(jnp.bfloat16)


def _swa(dm, krm, bias_a, sink_a):
    B, NB = dm.shape[0], dm.shape[1]
    S = NB * MOBA_BLOCK
    qa_blk = DM_QA // A_WIDTH
    va_blk = DM_VA // A_KV_WIDTH
    return pl.pallas_call(
        _swa_kernel,
        grid=(B, NB),
        in_specs=[
            pl.BlockSpec((1, 1, A_WIDTH, SWA_TOKENS), lambda b, t: (b, t, qa_blk, 0)),
            pl.BlockSpec((1, 1, A_KV_WIDTH, SWA_TOKENS), lambda b, t: (b, t, va_blk, 0)),
            pl.BlockSpec((1, 1, A_KV_WIDTH, SWA_TOKENS),
                         lambda b, t: (b, jnp.maximum(t - 1, 0), va_blk, 0)),
            pl.BlockSpec((1, SWA_TOKENS, A_KV_WIDTH), lambda b, t: (b, t, 0)),
            pl.BlockSpec((1, WINDOW, A_KV_WIDTH),
                         lambda b, t: (b, jnp.maximum(2 * t - 1, 0), 0)),
            pl.BlockSpec((A_KV_HEADS, 2 * WINDOW, A_GROUP * WINDOW), lambda b, t: (0, 0, 0)),
            pl.BlockSpec((A_KV_HEADS, 1, A_GROUP * WINDOW), lambda b, t: (0, 0, 0)),
        ],
        out_specs=pl.BlockSpec((1, SWA_TOKENS, A_WIDTH), lambda b, t: (b, t, 0)),
        out_shape=jax.ShapeDtypeStruct((B, S, A_WIDTH), jnp.bfloat16),
        compiler_params=pltpu.CompilerParams(
            dimension_semantics=("parallel", "parallel"), vmem_limit_bytes=VMEM_LIMIT),
        name="swa",
    )(dm, dm, dm, krm, krm, bias_a, sink_a)


def _sel_kernel(q_ref, kmean_ref, sel_ref):
    g = pl.program_id(2)
    nblk = kmean_ref.shape[1]
    km = jnp.concatenate([kmean_ref[0], jnp.zeros((SEL_ROWS - nblk, LANE), jnp.float32)],
                         axis=0).astype(jnp.bfloat16)
    blk = lax.broadcasted_iota(jnp.int32, (SEL_ROWS, MOBA_BLOCK), 0)
    for i in range(SEL_BLOCKS):
        n = g * SEL_BLOCKS + i
        for r in range(2):
            qpad = q_ref[0, i, r * HEAD_PAD:(r + 1) * HEAD_PAD, :]
            gate = jnp.dot(km, qpad, preferred_element_type=jnp.float32)
            avail = blk < n
            sel = jnp.zeros((SEL_ROWS, MOBA_BLOCK), jnp.float32)
            for _ in range(MOBA_TOPK):
                gm = jnp.where(avail, gate, -jnp.inf)
                mx = jnp.max(gm, axis=0, keepdims=True)
                cand = jnp.where(avail & (gm == mx), blk, SEL_ROWS)
                pick = blk == jnp.min(cand, axis=0, keepdims=True)
                sel = jnp.where(pick, 1.0, sel)
                avail = avail & jnp.logical_not(pick)
            on_prev = jnp.max(jnp.where(blk == n - 1, sel, 0.0), axis=0, keepdims=True)
            rows = jnp.where(blk < n - 1, sel, 0.0)
            rows = jnp.where(blk == ROW_PREV, on_prev, rows)
            rows = jnp.where(blk == ROW_OWN, 1.0, rows)
            sel_ref[0, r, i] = rows


def _sel(dm, kmean):
    B, NB = dm.shape[0], dm.shape[1]
    pairs = B_HEADS // 2
    return pl.pallas_call(
        _sel_kernel,
        grid=(B, pairs, NB // SEL_BLOCKS),
        in_specs=[
            pl.BlockSpec((1, SEL_BLOCKS, 2 * HEAD_PAD, MOBA_BLOCK), lambda b, p, g: (b, g, p, 0)),
            pl.BlockSpec((1, NB, LANE), lambda b, p, g: (b, 0, p)),
        ],
        out_specs=pl.BlockSpec((1, 2, SEL_BLOCKS, SEL_ROWS, MOBA_BLOCK),
                               lambda b, p, g: (b, p, g, 0, 0)),
        out_shape=jax.ShapeDtypeStruct((B, B_HEADS, NB, SEL_ROWS, MOBA_BLOCK), jnp.float32),
        compiler_params=pltpu.CompilerParams(
            dimension_semantics=("parallel", "parallel", "parallel"), vmem_limit_bytes=VMEM_LIMIT),
        name="sel",
    )(dm, kmean)


def _far_schedule(nblk):
    items = [(n, c) for n in range(nblk) for c in range((max(n - 1, 0) + FAR_CHUNK - 1) // FAR_CHUNK)]
    assert len(items) % 2 == 0
    return (np.array([i[0] for i in items], np.int32), np.array([i[1] for i in items], np.int32))


def _moba_kernel(nof_ref, cof_ref, c31_ref, q_ref, k_ref, vt_ref, sel_ref, bnear_ref, o_ref,
                 sa_ref, sb_ref, m_ref, acc_ref):
    pr = pl.program_id(1)
    nblk = q_ref.shape[1]
    nitems = nof_ref.shape[0]
    heads = (0, 1)
    c31 = [c31_ref[2 * pr + r] for r in heads]
    ones_rows = jnp.ones((ONES_ROWS, MOBA_BLOCK), jnp.bfloat16)

    def qpad(n, r):
        return q_ref[0, n, r * HEAD_PAD:(r + 1) * HEAD_PAD, :]

    def k_blk(j, nb=1):
        return k_ref[0, pl.ds(pl.multiple_of(j * MOBA_BLOCK, MOBA_BLOCK), nb * MOBA_BLOCK), :]

    def vt_ext(blocks, r):
        return jnp.concatenate(
            [jnp.concatenate([vt_ref[0, j, r * HEAD_DIM:(r + 1) * HEAD_DIM, :], ones_rows], axis=0)
             for j in blocks], axis=1)

    def on_row(n, r, row):
        return sel_ref[0, r, n, pl.ds(row, 1), :] > 0.5

    def near_body(it, _):
        for u in range(2):
            n = 2 * it + u
            jp = jnp.maximum(n - 1, 0)
            kc = jnp.concatenate([k_blk(jp), k_blk(n)], axis=0)
            for r in heads:
                s = jnp.dot(kc, qpad(n, r), preferred_element_type=jnp.float32) + bnear_ref[r]
                on_prev = on_row(n, r, ROW_PREV)
                m = jnp.maximum(
                    jnp.max(s[MOBA_BLOCK:], axis=0, keepdims=True),
                    jnp.where(on_prev, jnp.max(s[:MOBA_BLOCK], axis=0, keepdims=True), -jnp.inf))
                p_prev = jnp.exp2(s[:MOBA_BLOCK] - jnp.where(on_prev, m, -NEG))
                p_own = jnp.exp2(s[MOBA_BLOCK:] - m)
                p = jnp.concatenate([p_prev, p_own], axis=0).astype(jnp.bfloat16)
                m_ref[n, r] = m
                acc_ref[n, r] = jnp.dot(vt_ext((jp, n), r), p, preferred_element_type=jnp.float32)
        return 0

    lax.fori_loop(0, nblk // 2, near_body, 0)

    def far_logits(k, buf_ref):
        n, c = nof_ref[k], cof_ref[k]
        kc = k_blk(c * FAR_CHUNK, FAR_CHUNK)
        out = []
        for r in heads:
            s = jnp.dot(kc, qpad(n, r), preferred_element_type=jnp.float32)
            buf_ref[r] = s
            mc = jnp.full((1, MOBA_BLOCK), -jnp.inf, jnp.float32)
            for i in range(FAR_CHUNK):
                bm = jnp.max(s[i * MOBA_BLOCK:(i + 1) * MOBA_BLOCK], axis=0, keepdims=True)
                mc = jnp.maximum(mc, jnp.where(on_row(n, r, c * FAR_CHUNK + i), bm + c31[r], -jnp.inf))
            out.append(mc)
        return out

    def far_softmax(k, buf_ref, mcs):
        n, c = nof_ref[k], cof_ref[k]
        for r in heads:
            m = m_ref[n, r]
            m_new = jnp.maximum(m, mcs[r])
            alpha = jnp.exp2(m - m_new)
            shift = m_new - c31[r]
            s = buf_ref[r]
            ps = []
            for i in range(FAR_CHUNK):
                on = on_row(n, r, c * FAR_CHUNK + i)
                p = jnp.exp2(s[i * MOBA_BLOCK:(i + 1) * MOBA_BLOCK] - jnp.where(on, shift, -NEG))
                ps.append(p.astype(jnp.bfloat16))
            vt = vt_ext([c * FAR_CHUNK + i for i in range(FAR_CHUNK)], r)
            m_ref[n, r] = m_new
            acc_ref[n, r] = alpha * acc_ref[n, r] + jnp.dot(
                vt, jnp.concatenate(ps, axis=0), preferred_element_type=jnp.float32)

    def far_body(it, mc_a):
        k = 2 * it
        mc_b = far_logits(k + 1, sb_ref)
        far_softmax(k, sa_ref, mc_a)
        mc_a = far_logits(jnp.minimum(k + 2, nitems - 1), sa_ref)
        far_softmax(k + 1, sb_ref, mc_b)
        return tuple(mc_a)

    lax.fori_loop(0, nitems // 2, far_body, tuple(far_logits(0, sa_ref)))

    def out_body(it, _):
        for u in range(2):
            n = 2 * it + u
            ot = jnp.concatenate([acc_ref[n, r, :HEAD_DIM, :] / acc_ref[n, r, HEAD_DIM:HEAD_DIM + 1, :]
                                  for r in heads], axis=0)
            o_ref[0, pl.ds(pl.multiple_of(n * MOBA_BLOCK, MOBA_BLOCK), MOBA_BLOCK), :] = (
                ot.T.astype(jnp.bfloat16))
        return 0

    lax.fori_loop(0, nblk // 2, out_body, 0)


def _moba(dm, krm, sel, bias_near, c31):
    B, NB = dm.shape[0], dm.shape[1]
    S = NB * MOBA_BLOCK
    pairs = B_HEADS // 2
    vb_blk = DM_VB // LANE
    kb_blk = A_KV_WIDTH // LANE
    n_of, c_of = _far_schedule(NB)
    smem = pl.BlockSpec(memory_space=pltpu.SMEM)
    return pl.pallas_call(
        _moba_kernel,
        grid=(B, pairs),
        in_specs=[
            smem, smem, smem,
            pl.BlockSpec((1, NB, 2 * HEAD_PAD, MOBA_BLOCK), lambda b, p: (b, 0, p, 0)),
            pl.BlockSpec((1, S, LANE), lambda b, p: (b, 0, kb_blk + p)),
            pl.BlockSpec((1, NB, LANE, MOBA_BLOCK), lambda b, p: (b, 0, vb_blk + p, 0)),
            pl.BlockSpec((1, 2, NB, SEL_ROWS, MOBA_BLOCK), lambda b, p: (b, p, 0, 0, 0)),
            pl.BlockSpec((2, 2 * MOBA_BLOCK, MOBA_BLOCK), lambda b, p: (p, 0, 0)),
        ],
        out_specs=pl.BlockSpec((1, S, LANE), lambda b, p: (b, 0, p)),
        out_shape=jax.ShapeDtypeStruct((B, S, B_WIDTH), jnp.bfloat16),
        scratch_shapes=[
            pltpu.VMEM((2, FAR_CHUNK * MOBA_BLOCK, MOBA_BLOCK), jnp.float32),
            pltpu.VMEM((2, FAR_CHUNK * MOBA_BLOCK, MOBA_BLOCK), jnp.float32),
            pltpu.VMEM((NB, 2, 1, MOBA_BLOCK), jnp.float32),
            pltpu.VMEM((NB, 2, HEAD_DIM + ONES_ROWS, MOBA_BLOCK), jnp.float32),
        ],
        compiler_params=pltpu.CompilerParams(
            dimension_semantics=("parallel", "parallel"), vmem_limit_bytes=VMEM_LIMIT),
        name="moba",
    )(jnp.asarray(n_of), jnp.asarray(c_of), c31, dm, krm, dm, sel, bias_near)


def _final_kernel(x_ref, oa_ref, ob_ref, wzg_ref, bg_ref, woa_ref, wob_ref, wout_ref,
                  gamma_ref, beta_ref, y_ref):
    x = x_ref[...]
    zg = jnp.dot(x.astype(jnp.bfloat16), wzg_ref[...], preferred_element_type=jnp.float32)
    za = zg[:, :A_WIDTH]
    zb = zg[:, A_WIDTH:A_WIDTH + B_WIDTH]
    g = zg[:, A_WIDTH + B_WIDTH:] + bg_ref[...]
    ua = oa_ref[...].astype(jnp.float32) * (za * jax.nn.sigmoid(za))
    ub = ob_ref[...].astype(jnp.float32) * (zb * jax.nn.sigmoid(zb))
    ya = jnp.dot(ua.astype(jnp.bfloat16), woa_ref[...], preferred_element_type=jnp.float32)
    yb = jnp.dot(ub.astype(jnp.bfloat16), wob_ref[...], preferred_element_type=jnp.float32)
    gs = jax.nn.sigmoid(g)
    merged = gs[:, :D_MODEL] * ya + gs[:, D_MODEL:] * yb
    out = jnp.dot(merged.astype(jnp.bfloat16), wout_ref[...], preferred_element_type=jnp.float32)
    r = DN_ALPHA * x + out
    mu = jnp.mean(r, axis=-1, keepdims=True)
    rc = r - mu
    var = jnp.mean(rc * rc, axis=-1, keepdims=True)
    y_ref[...] = rc * lax.rsqrt(var + LN_EPS) * gamma_ref[...] + beta_ref[...]


def _final(x2, oa2, ob2, w_zg, b_gate, w_oa, w_ob, w_out, gamma, beta):
    N, D = x2.shape
    T = FINAL_TOKENS
    full = lambda a: pl.BlockSpec(a.shape, lambda i: (0,) * a.ndim)
    return pl.pallas_call(
        _final_kernel,
        grid=(N // T,),
        in_specs=[
            pl.BlockSpec((T, D), lambda i: (i, 0)),
            pl.BlockSpec((T, A_WIDTH), lambda i: (i, 0)),
            pl.BlockSpec((T, B_WIDTH), lambda i: (i, 0)),
            full(w_zg), full(b_gate), full(w_oa), full(w_ob), full(w_out), full(gamma), full(beta),
        ],
        out_specs=pl.BlockSpec((T, D), lambda i: (i, 0)),
        out_shape=jax.ShapeDtypeStruct((N, D), jnp.float32),
        compiler_params=pltpu.CompilerParams(
            dimension_semantics=("parallel",), vmem_limit_bytes=VMEM_LIMIT),
        name="final",
    )(x2, oa2, ob2, w_zg, b_gate, w_oa, w_ob, w_out, gamma, beta)


def _toeplitz(v, rows, col0, ncols):
    H, L = v.shape
    assert col0 >= rows - 1 and col0 + ncols <= L - 1
    t = jnp.tile(v, (1, rows))[:, :rows * (L - 1)].reshape(H, rows, L - 1)
    return t[:, :, col0:col0 + ncols]


def _bias_tables(rel_table, sinks):
    table_a = rel_table[:, :A_HEADS].astype(jnp.float32) * LOG2E
    table_b = rel_table[:, A_HEADS:].astype(jnp.float32) * LOG2E
    sinks = sinks.astype(jnp.float32) * LOG2E
    d_a = np.arange(-2 * WINDOW, 3 * WINDOW)
    ok_a = (d_a >= 0) & (d_a < WINDOW)
    v_a = jnp.where(ok_a[:, None], table_a[_t5_bucket_np(d_a)], NEG).T
    ba = _toeplitz(v_a, 2 * WINDOW, 3 * WINDOW, WINDOW)
    ba = ba.reshape(A_KV_HEADS, A_GROUP, 2 * WINDOW, WINDOW)
    ba = ba.transpose(0, 2, 1, 3).reshape(A_KV_HEADS, 2 * WINDOW, A_GROUP * WINDOW)
    sink = jnp.broadcast_to(sinks.reshape(A_KV_HEADS, 1, A_GROUP, 1),
                            (A_KV_HEADS, 1, A_GROUP, WINDOW)).reshape(A_KV_HEADS, 1, A_GROUP * WINDOW)
    d_b = np.arange(-MOBA_BLOCK, 3 * MOBA_BLOCK)
    v_b = jnp.where((d_b >= 0)[:, None], table_b[_t5_bucket_np(d_b)], NEG).T
    b_own = _toeplitz(v_b, MOBA_BLOCK, MOBA_BLOCK, MOBA_BLOCK)
    b_prev = _toeplitz(v_b, MOBA_BLOCK, 2 * MOBA_BLOCK, MOBA_BLOCK)
    b_near = jnp.concatenate([b_prev, b_own], axis=1)
    c31 = table_b[N_BUCKETS - 1]
    return ba, sink, b_near, c31


def kernel(x, w_in, b_gate, sinks, rel_table, w_out_a, w_out_b, w_out, ln_gamma, ln_beta):
    B, S, D = x.shape
    assert (D, w_in.shape[0]) == (D_MODEL, DEPTH) and S % PROJ_TOKENS == 0
    w = w_in[0]
    sizes = (A_WIDTH, A_KV_WIDTH, A_KV_WIDTH, A_WIDTH, B_WIDTH, B_WIDTH, B_WIDTH, B_WIDTH,
             D_MODEL, D_MODEL)
    offs = np.concatenate([[0], np.cumsum(sizes)])
    w_qa, w_ka, w_va, w_za, w_qb, w_kb, w_vb, w_zb, w_ga, w_gb = [
        w[:, offs[i]:offs[i + 1]] for i in range(len(sizes))]
    bf = jnp.bfloat16
    q_scale = ATTN_SCALE * LOG2E
    w_rm = jnp.concatenate([w_ka, w_kb], axis=1).astype(bf)
    w_dm_t = jnp.concatenate([w_qb * q_scale, w_qa * q_scale, w_vb, w_va], axis=1).T.astype(bf)
    w_zg = jnp.concatenate([w_za, w_zb, w_ga, w_gb], axis=1).astype(bf)
    bias_a, sink_a, b_near, c31 = _bias_tables(rel_table, sinks[0])

    krm, dm, kmean = _proj(x, w_rm, w_dm_t)
    kmean = kmean.reshape(B, S // MOBA_BLOCK, B_WIDTH)
    oa = _swa(dm, krm, bias_a, sink_a)
    sel = _sel(dm, kmean)
    ob = _moba(dm, krm, sel, b_near, c31)
    y = _final(x.reshape(B * S, D), oa.reshape(B * S, A_WIDTH), ob.reshape(B * S, B_WIDTH),
               w_zg, b_gate[0][None, :], w_out_a[0].astype(bf), w_out_b[0].astype(bf),
               w_out[0].astype(bf), ln_gamma[0][None, :], ln_beta[0][None, :])
    return y.reshape(B, S, D)
```

```python
import math

import numpy as np
import jax
import jax.numpy as jnp
from jax import lax
from jax.experimental import pallas as pl
from jax.experimental.pallas import tpu as pltpu

D_MODEL = 1024
HEAD_DIM = 64
A_HEADS = 8
A_KV_HEADS = 2
A_GROUP = A_HEADS // A_KV_HEADS
A_WIDTH = A_HEADS * HEAD_DIM
A_KV_WIDTH = A_KV_HEADS * HEAD_DIM
WINDOW = 128
B_HEADS = 8
B_WIDTH = B_HEADS * HEAD_DIM
MOBA_BLOCK = 256
MOBA_TOPK = 3
N_BUCKETS = 32
MAX_DISTANCE = 128
DEPTH = 1
DN_ALPHA = (2.0 * DEPTH) ** 0.25
LN_EPS = 1e-5
NEG = -1e30
ATTN_SCALE = HEAD_DIM ** -0.5
LOG2E = math.log2(math.e)

LANE = 128
HEAD_PAD = 2 * HEAD_DIM
W_QB, W_QA, W_VB, W_VA = 0, B_WIDTH, B_WIDTH + A_WIDTH, 2 * B_WIDTH + A_WIDTH
W_ROWS = W_VA + A_KV_WIDTH
DM_QB, DM_QA = 0, B_HEADS * HEAD_PAD
DM_VB = DM_QA + A_WIDTH
DM_VA = DM_VB + B_WIDTH
DM_ROWS = DM_VA + A_KV_WIDTH
RM_COLS = A_KV_WIDTH + B_WIDTH
PROJ_TOKENS = 512
SWA_TOKENS = 256
FINAL_TOKENS = 256
FAR_CHUNK = 2
FAR_UNROLL = 8
ONES_ROWS = 16
SEL_ROWS = 48
ROW_PREV, ROW_OWN = 32, 33
SEL_BLOCKS = 4
VMEM_LIMIT = 56 * 1024 * 1024


def _t5_bucket_np(dist):
    max_exact = N_BUCKETS // 2
    n = np.maximum(dist, 0)
    nf = np.maximum(n, 1).astype(np.float32)
    large = max_exact + (np.log(nf / max_exact) / math.log(MAX_DISTANCE / max_exact)
                         * (N_BUCKETS - max_exact)).astype(np.int32)
    large = np.minimum(large, N_BUCKETS - 1)
    return np.where(n < max_exact, n, large).astype(np.int32)


def _proj_kernel(x_ref, wrm_ref, wdm_ref, krm_ref, dm_ref, kmean_ref):
    xb = x_ref[0].astype(jnp.bfloat16)
    rm = jnp.dot(xb, wrm_ref[...], preferred_element_type=jnp.float32)
    krm_ref[0] = rm.astype(jnp.bfloat16)
    for c in range(PROJ_TOKENS // MOBA_BLOCK):
        kb = rm[c * MOBA_BLOCK:(c + 1) * MOBA_BLOCK, A_KV_WIDTH:]
        kmean_ref[0, 0, c:c + 1, :] = jnp.sum(kb, axis=0, keepdims=True) * (1.0 / MOBA_BLOCK)
    dm = lax.dot_general(wdm_ref[...], xb, (((1,), (1,)), ((), ())),
                         preferred_element_type=jnp.float32)
    dmb = dm.astype(jnp.bfloat16)
    zeros = jnp.zeros((HEAD_DIM, MOBA_BLOCK), jnp.bfloat16)
    for c in range(PROJ_TOKENS // MOBA_BLOCK):
        piece = dmb[:, c * MOBA_BLOCK:(c + 1) * MOBA_BLOCK]
        rows = []
        for h in range(B_HEADS):
            q = piece[W_QB + h * HEAD_DIM:W_QB + (h + 1) * HEAD_DIM]
            rows += [q, zeros] if h % 2 == 0 else [zeros, q]
        dm_ref[0, c] = jnp.concatenate(rows + [piece[W_QA:]], axis=0)


def _proj(x, w_rm, w_dm_t):
    B, S, D = x.shape
    nt = S // PROJ_TOKENS
    cpt = PROJ_TOKENS // MOBA_BLOCK
    return pl.pallas_call(
        _proj_kernel,
        grid=(B, nt),
        in_specs=[
            pl.BlockSpec((1, PROJ_TOKENS, D), lambda b, t: (b, t, 0)),
            pl.BlockSpec((D, RM_COLS), lambda b, t: (0, 0)),
            pl.BlockSpec((W_ROWS, D), lambda b, t: (0, 0)),
        ],
        out_specs=[
            pl.BlockSpec((1, PROJ_TOKENS, RM_COLS), lambda b, t: (b, t, 0)),
            pl.BlockSpec((1, cpt, DM_ROWS, MOBA_BLOCK), lambda b, t: (b, t, 0, 0)),
            pl.BlockSpec((1, 1, cpt, B_WIDTH), lambda b, t: (b, t, 0, 0)),
        ],
        out_shape=[
            jax.ShapeDtypeStruct((B, S, RM_COLS), jnp.bfloat16),
            jax.ShapeDtypeStruct((B, S // MOBA_BLOCK, DM_ROWS, MOBA_BLOCK), jnp.bfloat16),
            jax.ShapeDtypeStruct((B, nt, cpt, B_WIDTH), jnp.float32),
        ],
        compiler_params=pltpu.CompilerParams(
            dimension_semantics=("parallel", "parallel"), vmem_limit_bytes=VMEM_LIMIT),
        name="proj",
    )(x, w_rm, w_dm_t)


def _swa_kernel(qt_ref, vcur_ref, vprev_ref, kcur_ref, kprev_ref, bias_ref, sink_ref, o_ref):
    t = pl.program_id(1)
    qt = qt_ref[0, 0]
    vcat = jnp.concatenate([vprev_ref[0, 0][:, WINDOW:], vcur_ref[0, 0]], axis=1)
    kcat = jnp.concatenate([kprev_ref[0], kcur_ref[0]], axis=0)
    kidx = lax.broadcasted_iota(jnp.int32, (2 * WINDOW, A_GROUP * WINDOW), 0)
    zeros_q = jnp.zeros((HEAD_DIM, A_GROUP * WINDOW), jnp.bfloat16)
    for w in range(SWA_TOKENS // WINDOW):
        k2 = kcat[w * WINDOW:(w + 2) * WINDOW, :]
        v2 = vcat[:, w * WINDOW:(w + 2) * WINDOW]
        outs = []
        for g in range(A_KV_HEADS):
            q4 = jnp.concatenate(
                [qt[(g * A_GROUP + i) * HEAD_DIM:(g * A_GROUP + i + 1) * HEAD_DIM,
                    w * WINDOW:(w + 1) * WINDOW] for i in range(A_GROUP)], axis=1)
            qpad = jnp.concatenate([q4, zeros_q] if g == 0 else [zeros_q, q4], axis=0)
            s = jnp.dot(k2, qpad, preferred_element_type=jnp.float32) + bias_ref[g]
            if w == 0:
                s = jnp.where((t == 0) & (kidx < WINDOW), NEG, s)
            sink = sink_ref[g]
            m = jnp.maximum(jnp.max(s, axis=0, keepdims=True), sink)
            p = jnp.exp2(s - m)
            l = jnp.sum(p, axis=0, keepdims=True) + jnp.exp2(sink - m)
            o = jnp.dot(v2[g * HEAD_DIM:(g + 1) * HEAD_DIM, :], p.astype(jnp.bfloat16),
                        preferred_element_type=jnp.float32)
            o = o / l
            outs += [o[:, i * WINDOW:(i + 1) * WINDOW] for i in range(A_GROUP)]
        ot = jnp.concatenate(outs, axis=0)
        o_ref[0, w * WINDOW:(w + 1) * WINDOW, :] = ot.T.astype(jnp.bfloat16)


def _swa(dm, krm, bias_a, sink_a):
    B, NB = dm.shape[0], dm.shape[1]
    S = NB * MOBA_BLOCK
    qa_blk = DM_QA // A_WIDTH
    va_blk = DM_VA // A_KV_WIDTH
    return pl.pallas_call(
        _swa_kernel,
        grid=(B, NB),
        in_specs=[
            pl.BlockSpec((1, 1, A_WIDTH, SWA_TOKENS), lambda b, t: (b, t, qa_blk, 0)),
            pl.BlockSpec((1, 1, A_KV_WIDTH, SWA_TOKENS), lambda b, t: (b, t, va_blk, 0)),
            pl.BlockSpec((1, 1, A_KV_WIDTH, SWA_TOKENS),
                         lambda b, t: (b, jnp.maximum(t - 1, 0), va_blk, 0)),
            pl.BlockSpec((1, SWA_TOKENS, A_KV_WIDTH), lambda b, t: (b, t, 0)),
            pl.BlockSpec((1, WINDOW, A_KV_WIDTH),
                         lambda b, t: (b, jnp.maximum(2 * t - 1, 0), 0)),
            pl.BlockSpec((A_KV_HEADS, 2 * WINDOW, A_GROUP * WINDOW), lambda b, t: (0, 0, 0)),
            pl.BlockSpec((A_KV_HEADS, 1, A_GROUP * WINDOW), lambda b, t: (0, 0, 0)),
        ],
        out_specs=pl.BlockSpec((1, SWA_TOKENS, A_WIDTH), lambda b, t: (b, t, 0)),
        out_shape=jax.ShapeDtypeStruct((B, S, A_WIDTH), jnp.bfloat16),
        compiler_params=pltpu.CompilerParams(
            dimension_semantics=("parallel", "parallel"), vmem_limit_bytes=VMEM_LIMIT),
        name="swa",
    )(dm, dm, dm, krm, krm, bias_a, sink_a)


def _sel_kernel(q_ref, kmean_ref, sel_ref):
    g = pl.program_id(2)
    nblk = kmean_ref.shape[1]
    km = jnp.concatenate([kmean_ref[0], jnp.zeros((SEL_ROWS - nblk, LANE), jnp.float32)],
                         axis=0).astype(jnp.bfloat16)
    blk = lax.broadcasted_iota(jnp.int32, (SEL_ROWS, MOBA_BLOCK), 0)
    for i in range(SEL_BLOCKS):
        n = g * SEL_BLOCKS + i
        for r in range(2):
            qpad = q_ref[0, i, r * HEAD_PAD:(r + 1) * HEAD_PAD, :]
            gate = jnp.dot(km, qpad, preferred_element_type=jnp.float32)
            avail = blk < n
            sel = jnp.zeros((SEL_ROWS, MOBA_BLOCK), jnp.float32)
            for _ in range(MOBA_TOPK):
                gm = jnp.where(avail, gate, -jnp.inf)
                mx = jnp.max(gm, axis=0, keepdims=True)
                cand = jnp.where(avail & (gm == mx), blk, SEL_ROWS)
                pick = blk == jnp.min(cand, axis=0, keepdims=True)
                sel = jnp.where(pick, 1.0, sel)
                avail = avail & jnp.logical_not(pick)
            on_prev = jnp.max(jnp.where(blk == n - 1, sel, 0.0), axis=0, keepdims=True)
            rows = jnp.where(blk < n - 1, sel, 0.0)
            rows = jnp.where(blk == ROW_PREV, on_prev, rows)
            rows = jnp.where(blk == ROW_OWN, 1.0, rows)
            sel_ref[0, r, i] = rows


def _sel(dm, kmean):
    B, NB = dm.shape[0], dm.shape[1]
    pairs = B_HEADS // 2
    return pl.pallas_call(
        _sel_kernel,
        grid=(B, pairs, NB // SEL_BLOCKS),
        in_specs=[
            pl.BlockSpec((1, SEL_BLOCKS, 2 * HEAD_PAD, MOBA_BLOCK), lambda b, p, g: (b, g, p, 0)),
            pl.BlockSpec((1, NB, LANE), lambda b, p, g: (b, 0, p)),
        ],
        out_specs=pl.BlockSpec((1, 2, SEL_BLOCKS, SEL_ROWS, MOBA_BLOCK),
                               lambda b, p, g: (b, p, g, 0, 0)),
        out_shape=jax.ShapeDtypeStruct((B, B_HEADS, NB, SEL_ROWS, MOBA_BLOCK), jnp.float32),
        compiler_params=pltpu.CompilerParams(
            dimension_semantics=("parallel", "parallel", "parallel"), vmem_limit_bytes=VMEM_LIMIT),
        name="sel",
    )(dm, kmean)


def _far_schedule(nblk):
    items = [(n, c) for n in range(nblk) for c in range((max(n - 1, 0) + FAR_CHUNK - 1) // FAR_CHUNK)]
    assert len(items) % FAR_UNROLL == 0
    return (np.array([i[0] for i in items], np.int32), np.array([i[1] for i in items], np.int32))


def _moba_kernel(nof_ref, cof_ref, c31_ref, q_ref, k_ref, vt_ref, sel_ref, bnear_ref, o_ref,
                 sa_ref, sb_ref, m_ref, acc_ref):
    pr = pl.program_id(1)
    nblk = q_ref.shape[1]
    nitems = nof_ref.shape[0]
    heads = (0, 1)
    c31 = [c31_ref[2 * pr + r] for r in heads]
    ones_rows = jnp.ones((ONES_ROWS, MOBA_BLOCK), jnp.bfloat16)

    def qpad(n, r):
        return q_ref[0, n, r * HEAD_PAD:(r + 1) * HEAD_PAD, :]

    def k_blk(j, nb=1):
        return k_ref[0, pl.ds(pl.multiple_of(j * MOBA_BLOCK, MOBA_BLOCK), nb * MOBA_BLOCK), :]

    def vt_ext(blocks, r):
        return jnp.concatenate(
            [jnp.concatenate([vt_ref[0, j, r * HEAD_DIM:(r + 1) * HEAD_DIM, :], ones_rows], axis=0)
             for j in blocks], axis=1)

    def on_row(n, r, row):
        return sel_ref[0, r, n, pl.ds(row, 1), :] > 0.5

    def near_logits(n, buf_ref):
        jp = jnp.maximum(n - 1, 0)
        kc = jnp.concatenate([k_blk(jp), k_blk(n)], axis=0)
        out = []
        for r in heads:
            s = jnp.dot(kc, qpad(n, r), preferred_element_type=jnp.float32) + bnear_ref[r]
            buf_ref[r] = s
            out.append(jnp.maximum(
                jnp.max(s[MOBA_BLOCK:], axis=0, keepdims=True),
                jnp.where(on_row(n, r, ROW_PREV),
                          jnp.max(s[:MOBA_BLOCK], axis=0, keepdims=True), -jnp.inf)))
        return out

    def near_softmax(n, buf_ref, ms):
        jp = jnp.maximum(n - 1, 0)
        for r in heads:
            s = buf_ref[r]
            p_prev = jnp.exp2(s[:MOBA_BLOCK] - jnp.where(on_row(n, r, ROW_PREV), ms[r], -NEG))
            p_own = jnp.exp2(s[MOBA_BLOCK:] - ms[r])
            p = jnp.concatenate([p_prev, p_own], axis=0).astype(jnp.bfloat16)
            m_ref[n, r] = ms[r]
            acc_ref[n, r] = jnp.dot(vt_ext((jp, n), r), p, preferred_element_type=jnp.float32)

    def near_body(it, m_a):
        n = 2 * it
        m_b = near_logits(n + 1, sb_ref)
        near_softmax(n, sa_ref, m_a)
        m_a = near_logits(jnp.minimum(n + 2, nblk - 1), sa_ref)
        near_softmax(n + 1, sb_ref, m_b)
        return tuple(m_a)

    lax.fori_loop(0, nblk // 2, near_body, tuple(near_logits(0, sa_ref)))

    def far_logits(k, buf_ref):
        n, c = nof_ref[k], cof_ref[k]
        kc = k_blk(c * FAR_CHUNK, FAR_CHUNK)
        out = []
        for r in heads:
            s = jnp.dot(kc, qpad(n, r), preferred_element_type=jnp.float32)
            buf_ref[r] = s
            mc = jnp.full((1, MOBA_BLOCK), -jnp.inf, jnp.float32)
            for i in range(FAR_CHUNK):
                bm = jnp.max(s[i * MOBA_BLOCK:(i + 1) * MOBA_BLOCK], axis=0, keepdims=True)
                mc = jnp.maximum(mc, jnp.where(on_row(n, r, c * FAR_CHUNK + i), bm + c31[r], -jnp.inf))
            out.append(mc)
        return out

    def far_softmax(k, buf_ref, mcs):
        n, c = nof_ref[k], cof_ref[k]
        for r in heads:
            m = m_ref[n, r]
            m_new = jnp.maximum(m, mcs[r])
            alpha = jnp.exp2(m - m_new)
            shift = m_new - c31[r]
            s = buf_ref[r]
            ps = []
            for i in range(FAR_CHUNK):
                on = on_row(n, r, c * FAR_CHUNK + i)
                p = jnp.exp2(s[i * MOBA_BLOCK:(i + 1) * MOBA_BLOCK] - jnp.where(on, shift, -NEG))
                ps.append(p.astype(jnp.bfloat16))
            vt = vt_ext([c * FAR_CHUNK + i for i in range(FAR_CHUNK)], r)
            m_ref[n, r] = m_new
            acc_ref[n, r] = alpha * acc_ref[n, r] + jnp.dot(
                vt, jnp.concatenate(ps, axis=0), preferred_element_type=jnp.float32)

    def far_body(it, mc_a):
        for u in range(FAR_UNROLL // 2):
            k = FAR_UNROLL * it + 2 * u
            mc_b = far_logits(k + 1, sb_ref)
            far_softmax(k, sa_ref, mc_a)
            mc_a = far_logits(jnp.minimum(k + 2, nitems - 1), sa_ref)
            far_softmax(k + 1, sb_ref, mc_b)
        return tuple(mc_a)

    lax.fori_loop(0, nitems // FAR_UNROLL, far_body, tuple(far_logits(0, sa_ref)))

    def out_body(it, _):
        for u in range(2):
            n = 2 * it + u
            ot = jnp.concatenate([acc_ref[n, r, :HEAD_DIM, :] / acc_ref[n, r, HEAD_DIM:HEAD_DIM + 1, :]
                                  for r in heads], axis=0)
            o_ref[0, pl.ds(pl.multiple_of(n * MOBA_BLOCK, MOBA_BLOCK), MOBA_BLOCK), :] = (
                ot.T.astype(jnp.bfloat16))
        return 0

    lax.fori_loop(0, nblk // 2, out_body, 0)


def _moba(dm, krm, sel, bias_near, c31):
    B, NB = dm.shape[0], dm.shape[1]
    S = NB * MOBA_BLOCK
    pairs = B_HEADS // 2
    vb_blk = DM_VB // LANE
    kb_blk = A_KV_WIDTH // LANE
    n_of, c_of = _far_schedule(NB)
    smem = pl.BlockSpec(memory_space=pltpu.SMEM)
    return pl.pallas_call(
        _moba_kernel,
        grid=(B, pairs),
        in_specs=[
            smem, smem, smem,
            pl.BlockSpec((1, NB, 2 * HEAD_PAD, MOBA_BLOCK), lambda b, p: (b, 0, p, 0)),
            pl.BlockSpec((1, S, LANE), lambda b, p: (b, 0, kb_blk + p)),
            pl.BlockSpec((1, NB, LANE, MOBA_BLOCK), lambda b, p: (b, 0, vb_blk + p, 0)),
            pl.BlockSpec((1, 2, NB, SEL_ROWS, MOBA_BLOCK), lambda b, p: (b, p, 0, 0, 0)),
            pl.BlockSpec((2, 2 * MOBA_BLOCK, MOBA_BLOCK), lambda b, p: (p, 0, 0)),
        ],
        out_specs=pl.BlockSpec((1, S, LANE), lambda b, p: (b, 0, p)),
        out_shape=jax.ShapeDtypeStruct((B, S, B_WIDTH), jnp.bfloat16),
        scratch_shapes=[
            pltpu.VMEM((2, FAR_CHUNK * MOBA_BLOCK, MOBA_BLOCK), jnp.float32),
            pltpu.VMEM((2, FAR_CHUNK * MOBA_BLOCK, MOBA_BLOCK), jnp.float32),
            pltpu.VMEM((NB, 2, 1, MOBA_BLOCK), jnp.float32),
            pltpu.VMEM((NB, 2, HEAD_DIM + ONES_ROWS, MOBA_BLOCK), jnp.float32),
        ],
        compiler_params=pltpu.CompilerParams(
            dimension_semantics=("parallel", "parallel"), vmem_limit_bytes=VMEM_LIMIT),
        name="moba",
    )(jnp.asarray(n_of), jnp.asarray(c_of), c31, dm, krm, dm, sel, bias_near)


def _final_kernel(x_ref, oa_ref, ob_ref, wzg_ref, bg_ref, woa_ref, wob_ref, wout_ref,
                  gamma_ref, beta_ref, y_ref):
    x = x_ref[...]
    zg = jnp.dot(x.astype(jnp.bfloat16), wzg_ref[...], preferred_element_type=jnp.float32)
    za = zg[:, :A_WIDTH]
    zb = zg[:, A_WIDTH:A_WIDTH + B_WIDTH]
    g = zg[:, A_WIDTH + B_WIDTH:] + bg_ref[...]
    ua = oa_ref[...].astype(jnp.float32) * (za * jax.nn.sigmoid(za))
    ub = ob_ref[...].astype(jnp.float32) * (zb * jax.nn.sigmoid(zb))
    ya = jnp.dot(ua.astype(jnp.bfloat16), woa_ref[...], preferred_element_type=jnp.float32)
    yb = jnp.dot(ub.astype(jnp.bfloat16), wob_ref[...], preferred_element_type=jnp.float32)
    gs = jax.nn.sigmoid(g)
    merged = gs[:, :D_MODEL] * ya + gs[:, D_MODEL:] * yb
    out = jnp.dot(merged.astype(jnp.bfloat16), wout_ref[...], preferred_element_type=jnp.float32)
    r = DN_ALPHA * x + out
    mu = jnp.mean(r, axis=-1, keepdims=True)
    rc = r - mu
    var = jnp.mean(rc * rc, axis=-1, keepdims=True)
    y_ref[...] = rc * lax.rsqrt(var + LN_EPS) * gamma_ref[...] + beta_ref[...]


def _final(x2, oa2, ob2, w_zg, b_gate, w_oa, w_ob, w_out, gamma, beta):
    N, D = x2.shape
    T = FINAL_TOKENS
    full = lambda a: pl.BlockSpec(a.shape, lambda i: (0,) * a.ndim)
    return pl.pallas_call(
        _final_kernel,
        grid=(N // T,),
        in_specs=[
            pl.BlockSpec((T, D), lambda i: (i, 0)),
            pl.BlockSpec((T, A_WIDTH), lambda i: (i, 0)),
            pl.BlockSpec((T, B_WIDTH), lambda i: (i, 0)),
            full(w_zg), full(b_gate), full(w_oa), full(w_ob), full(w_out), full(gamma), full(beta),
        ],
        out_specs=pl.BlockSpec((T, D), lambda i: (i, 0)),
        out_shape=jax.ShapeDtypeStruct((N, D), jnp.float32),
        compiler_params=pltpu.CompilerParams(
            dimension_semantics=("parallel",), vmem_limit_bytes=VMEM_LIMIT),
        name="final",
    )(x2, oa2, ob2, w_zg, b_gate, w_oa, w_ob, w_out, gamma, beta)


def _toeplitz(v, rows, col0, ncols):
    H, L = v.shape
    assert col0 >= rows - 1 and col0 + ncols <= L - 1
    t = jnp.tile(v, (1, rows))[:, :rows * (L - 1)].reshape(H, rows, L - 1)
    return t[:, :, col0:col0 + ncols]


def _bias_tables(rel_table, sinks):
    table_a = rel_table[:, :A_HEADS].astype(jnp.float32) * LOG2E
    table_b = rel_table[:, A_HEADS:].astype(jnp.float32) * LOG2E
    sinks = sinks.astype(jnp.float32) * LOG2E
    d_a = np.arange(-2 * WINDOW, 3 * WINDOW)
    ok_a = (d_a >= 0) & (d_a < WINDOW)
    v_a = jnp.where(ok_a[:, None], table_a[_t5_bucket_np(d_a)], NEG).T
    ba = _toeplitz(v_a, 2 * WINDOW, 3 * WINDOW, WINDOW)
    ba = ba.reshape(A_KV_HEADS, A_GROUP, 2 * WINDOW, WINDOW)
    ba = ba.transpose(0, 2, 1, 3).reshape(A_KV_HEADS, 2 * WINDOW, A_GROUP * WINDOW)
    sink = jnp.broadcast_to(sinks.reshape(A_KV_HEADS, 1, A_GROUP, 1),
                            (A_KV_HEADS, 1, A_GROUP, WINDOW)).reshape(A_KV_HEADS, 1, A_GROUP * WINDOW)
    d_b = np.arange(-MOBA_BLOCK, 3 * MOBA_BLOCK)
    v_b = jnp.where((d_b >= 0)[:, None], table_b[_t5_bucket_np(d_b)], NEG).T
    b_own = _toeplitz(v_b, MOBA_BLOCK, MOBA_BLOCK, MOBA_BLOCK)
    b_prev = _toeplitz(v_b, MOBA_BLOCK, 2 * MOBA_BLOCK, MOBA_BLOCK)
    b_near = jnp.concatenate([b_prev, b_own], axis=1)
    c31 = table_b[N_BUCKETS - 1]
    return ba, sink, b_near, c31


def kernel(x, w_in, b_gate, sinks, rel_table, w_out_a, w_out_b, w_out, ln_gamma, ln_beta):
    B, S, D = x.shape
    assert (D, w_in.shape[0]) == (D_MODEL, DEPTH) and S % PROJ_TOKENS == 0
    w = w_in[0]
    sizes = (A_WIDTH, A_KV_WIDTH, A_KV_WIDTH, A_WIDTH, B_WIDTH, B_WIDTH, B_WIDTH, B_WIDTH,
             D_MODEL, D_MODEL)
    offs = np.concatenate([[0], np.cumsum(sizes)])
    w_qa, w_ka, w_va, w_za, w_qb, w_kb, w_vb, w_zb, w_ga, w_gb = [
        w[:, offs[i]:offs[i + 1]] for i in range(len(sizes))]
    bf = jnp.bfloat16
    q_scale = ATTN_SCALE * LOG2E
    w_rm = jnp.concatenate([w_ka, w_kb], axis=1).astype(bf)
    w_dm_t = jnp.concatenate([w_qb * q_scale, w_qa * q_scale, w_vb, w_va], axis=1).T.astype(bf)
    w_zg = jnp.concatenate([w_za, w_zb, w_ga, w_gb], axis=1).astype(bf)
    bias_a, sink_a, b_near, c31 = _bias_tables(rel_table, sinks[0])

    krm, dm, kmean = _proj(x, w_rm, w_dm_t)
    kmean = kmean.reshape(B, S // MOBA_BLOCK, B_WIDTH)
    oa = _swa(dm, krm, bias_a, sink_a)
    sel = _sel(dm, kmean)
    ob = _moba(dm, krm, sel, b_near, c31)
    y = _final(x.reshape(B * S, D), oa.reshape(B * S, A_WIDTH), ob.reshape(B * S, B_WIDTH),
               w_zg, b_gate[0][None, :], w_out_a[0].astype(bf), w_out_b[0].astype(bf),
               w_out[0].astype(bf), ln_gamma[0][None, :], ln_beta[0][None, :])
    return y.reshape(B, S, D)
```

```python
import functools
import math

import numpy as np
import jax
import jax.numpy as jnp
from jax import lax
from jax.experimental import pallas as pl
from jax.experimental.pallas import tpu as pltpu

D_MODEL = 1024
HEAD_DIM = 64
A_HEADS = 8
A_KV_HEADS = 2
A_GROUP = A_HEADS // A_KV_HEADS
A_WIDTH = A_HEADS * HEAD_DIM
A_KV_WIDTH = A_KV_HEADS * HEAD_DIM
WINDOW = 128
B_HEADS = 8
B_WIDTH = B_HEADS * HEAD_DIM
MOBA_BLOCK = 256
MOBA_TOPK = 3
N_BUCKETS = 32
MAX_DISTANCE = 128
DEPTH = 1
DN_ALPHA = (2.0 * DEPTH) ** 0.25
LN_EPS = 1e-5
NEG = -1e30
ATTN_SCALE = HEAD_DIM ** -0.5
LOG2E = math.log2(math.e)

LANE = 128
HEAD_PAD = 2 * HEAD_DIM
W_QB, W_QA, W_VB, W_VA = 0, B_WIDTH, B_WIDTH + A_WIDTH, 2 * B_WIDTH + A_WIDTH
W_ROWS = W_VA + A_KV_WIDTH
DM_QB, DM_QA = 0, B_HEADS * HEAD_PAD
DM_VB = DM_QA + A_WIDTH
DM_VA = DM_VB + B_WIDTH
DM_ROWS = DM_VA + A_KV_WIDTH
RM_COLS = A_KV_WIDTH + B_WIDTH
PROJ_TOKENS = 512
SWA_TOKENS = 256
FAR_CHUNK = 2
FAR_UNROLL = 8
ONES_ROWS = 16
SEL_ROWS = 40
ROW_PREV, ROW_OWN = 32, 33
VMEM_LIMIT = 56 * 1024 * 1024


def _t5_bucket_np(dist):
    max_exact = N_BUCKETS // 2
    n = np.maximum(dist, 0)
    nf = np.maximum(n, 1).astype(np.float32)
    large = max_exact + (np.log(nf / max_exact) / math.log(MAX_DISTANCE / max_exact)
                         * (N_BUCKETS - max_exact)).astype(np.int32)
    large = np.minimum(large, N_BUCKETS - 1)
    return np.where(n < max_exact, n, large).astype(np.int32)


def _select_rows(gate, n):
    nblk = gate.shape[0]
    blk = lax.broadcasted_iota(jnp.int32, gate.shape, 0)
    work = jnp.where(blk < n, gate, -jnp.inf)
    sel = jnp.zeros(gate.shape, jnp.float32)
    for _ in range(MOBA_TOPK):
        mx = jnp.max(work, axis=0, keepdims=True)
        idx = jnp.min(jnp.where(work == mx, blk, nblk), axis=0, keepdims=True)
        idx = jnp.where(mx > -jnp.inf, idx, nblk)
        pick = blk == idx
        sel = jnp.where(pick, 1.0, sel)
        work = jnp.where(pick, -jnp.inf, work)
    on_prev = jnp.max(jnp.where(blk == n - 1, sel, 0.0), axis=0, keepdims=True)
    far = jnp.where(blk < n - 1, sel, 0.0)
    row8 = lax.broadcasted_iota(jnp.int32, (SEL_ROWS - nblk, gate.shape[1]), 0)
    near = jnp.where(row8 == 0, on_prev, jnp.where(row8 == 1, 1.0, 0.0))
    return far, near


def _proj_kernel(x_ref, wrm_ref, wdm_ref, krm_ref, dm_ref, sel_ref, kmean_ref):
    t = pl.program_id(1)
    blocks = PROJ_TOKENS // MOBA_BLOCK
    nblk = kmean_ref.shape[0]

    @pl.when(t == 0)
    def _():
        kmean_ref[...] = jnp.zeros(kmean_ref.shape, jnp.float32)

    xb = x_ref[0].astype(jnp.bfloat16)
    rm = jnp.dot(xb, wrm_ref[...], preferred_element_type=jnp.float32)
    krm_ref[0] = rm.astype(jnp.bfloat16)
    for c in range(blocks):
        kb = rm[c * MOBA_BLOCK:(c + 1) * MOBA_BLOCK, A_KV_WIDTH:]
        kmean_ref[pl.ds(t * blocks + c, 1), :] = (
            jnp.sum(kb, axis=0, keepdims=True) * (1.0 / MOBA_BLOCK))
    dm = lax.dot_general(wdm_ref[...], xb, (((1,), (1,)), ((), ())),
                         preferred_element_type=jnp.float32)
    dmb = dm.astype(jnp.bfloat16)
    zeros = jnp.zeros((HEAD_DIM, MOBA_BLOCK), jnp.bfloat16)
    km = kmean_ref[...].astype(jnp.bfloat16)
    for c in range(blocks):
        piece = dmb[:, c * MOBA_BLOCK:(c + 1) * MOBA_BLOCK]
        rows = []
        for h in range(B_HEADS):
            q = piece[W_QB + h * HEAD_DIM:W_QB + (h + 1) * HEAD_DIM]
            qpad = jnp.concatenate([q, zeros] if h % 2 == 0 else [zeros, q], axis=0)
            rows.append(qpad)
            pair = h // 2
            gate = jnp.dot(km[:, pair * HEAD_PAD:(pair + 1) * HEAD_PAD], qpad,
                           preferred_element_type=jnp.float32)
            far, near = _select_rows(gate, t * blocks + c)
            sel_ref[0, h, c, :nblk, :] = far
            sel_ref[0, h, c, nblk:, :] = near
        dm_ref[0, c] = jnp.concatenate(rows + [piece[W_QA:]], axis=0)


def _proj(x, w_rm, w_dm_t):
    B, S, D = x.shape
    nt = S // PROJ_TOKENS
    cpt = PROJ_TOKENS // MOBA_BLOCK
    NB = S // MOBA_BLOCK
    return pl.pallas_call(
        _proj_kernel,
        grid=(B, nt),
        in_specs=[
            pl.BlockSpec((1, PROJ_TOKENS, D), lambda b, t: (b, t, 0)),
            pl.BlockSpec((D, RM_COLS), lambda b, t: (0, 0)),
            pl.BlockSpec((W_ROWS, D), lambda b, t: (0, 0)),
        ],
        out_specs=[
            pl.BlockSpec((1, PROJ_TOKENS, RM_COLS), lambda b, t: (b, t, 0)),
            pl.BlockSpec((1, cpt, DM_ROWS, MOBA_BLOCK), lambda b, t: (b, t, 0, 0)),
            pl.BlockSpec((1, B_HEADS, cpt, SEL_ROWS, MOBA_BLOCK), lambda b, t: (b, 0, t, 0, 0)),
        ],
        out_shape=[
            jax.ShapeDtypeStruct((B, S, RM_COLS), jnp.bfloat16),
            jax.ShapeDtypeStruct((B, NB, DM_ROWS, MOBA_BLOCK), jnp.bfloat16),
            jax.ShapeDtypeStruct((B, B_HEADS, NB, SEL_ROWS, MOBA_BLOCK), jnp.float32),
        ],
        scratch_shapes=[pltpu.VMEM((NB, B_WIDTH), jnp.float32)],
        compiler_params=pltpu.CompilerParams(
            dimension_semantics=("parallel", "arbitrary"), vmem_limit_bytes=VMEM_LIMIT),
        name="proj",
    )(x, w_rm, w_dm_t)


def _swa_logits(t, qt_ref, kcur_ref, kprev_ref, bias_ref, s_ref):
    qt = qt_ref[0, 0]
    kcat = jnp.concatenate([kprev_ref[0], kcur_ref[0]], axis=0)
    kidx = lax.broadcasted_iota(jnp.int32, (2 * WINDOW, A_GROUP * WINDOW), 0)
    zeros_q = jnp.zeros((HEAD_DIM, A_GROUP * WINDOW), jnp.bfloat16)
    for w in range(SWA_TOKENS // WINDOW):
        k2 = kcat[w * WINDOW:(w + 2) * WINDOW, :]
        for g in range(A_KV_HEADS):
            q4 = jnp.concatenate(
                [qt[(g * A_GROUP + i) * HEAD_DIM:(g * A_GROUP + i + 1) * HEAD_DIM,
                    w * WINDOW:(w + 1) * WINDOW] for i in range(A_GROUP)], axis=1)
            qpad = jnp.concatenate([q4, zeros_q] if g == 0 else [zeros_q, q4], axis=0)
            s = jnp.dot(k2, qpad, preferred_element_type=jnp.float32) + bias_ref[g]
            if w == 0:
                s = jnp.where((t == 0) & (kidx < WINDOW), NEG, s)
            s_ref[w, g] = s


def _swa_unit(w, g, s_ref, vcat, sink_ref):
    s = s_ref[w, g]
    sink = sink_ref[g]
    m = jnp.maximum(jnp.max(s, axis=0, keepdims=True), sink)
    p = jnp.exp2(s - m)
    l = jnp.sum(p, axis=0, keepdims=True) + jnp.exp2(sink - m)
    v2 = vcat[g * HEAD_DIM:(g + 1) * HEAD_DIM, w * WINDOW:(w + 2) * WINDOW]
    o = jnp.dot(v2, p.astype(jnp.bfloat16), preferred_element_type=jnp.float32) / l
    return [o[:, i * WINDOW:(i + 1) * WINDOW] for i in range(A_GROUP)]


def _swa_assemble(units):
    return jnp.concatenate(
        [jnp.concatenate([o for g in range(A_KV_HEADS) for o in units[w][g]], axis=0).T
         for w in range(SWA_TOKENS // WINDOW)], axis=0)


def _far_schedule(nblk):
    items = [(n, c) for n in range(nblk) for c in range((max(n - 1, 0) + FAR_CHUNK - 1) // FAR_CHUNK)]
    assert len(items) % FAR_UNROLL == 0
    return (np.array([i[0] for i in items], np.int32), np.array([i[1] for i in items], np.int32))


def _moba_kernel(nof_ref, cof_ref, c31_ref, q_ref, k_ref, vt_ref, sel_ref, bnear_ref, o_ref,
                 sa_ref, sb_ref, m_ref, acc_ref):
    pr = pl.program_id(1)
    nblk = q_ref.shape[1]
    nitems = nof_ref.shape[0]
    heads = (0, 1)
    c31 = [c31_ref[2 * pr + r] for r in heads]
    ones_rows = jnp.ones((ONES_ROWS, MOBA_BLOCK), jnp.bfloat16)

    def qpad(n, r):
        return q_ref[0, n, r * HEAD_PAD:(r + 1) * HEAD_PAD, :]

    def k_blk(j, nb=1):
        return k_ref[0, pl.ds(pl.multiple_of(j * MOBA_BLOCK, MOBA_BLOCK), nb * MOBA_BLOCK), :]

    def vt_ext(blocks, r):
        return jnp.concatenate(
            [jnp.concatenate([vt_ref[0, j, r * HEAD_DIM:(r + 1) * HEAD_DIM, :], ones_rows], axis=0)
             for j in blocks], axis=1)

    def on_row(n, r, row):
        return sel_ref[0, r, n, pl.ds(row, 1), :] > 0.5

    def near_logits(n, buf_ref):
        jp = jnp.maximum(n - 1, 0)
        kc = jnp.concatenate([k_blk(jp), k_blk(n)], axis=0)
        out = []
        for r in heads:
            s = jnp.dot(kc, qpad(n, r), preferred_element_type=jnp.float32) + bnear_ref[r]
            buf_ref[r] = s
            out.append(jnp.maximum(
                jnp.max(s[MOBA_BLOCK:], axis=0, keepdims=True),
                jnp.where(on_row(n, r, ROW_PREV),
                          jnp.max(s[:MOBA_BLOCK], axis=0, keepdims=True), -jnp.inf)))
        return out

    def near_softmax(n, buf_ref, ms):
        jp = jnp.maximum(n - 1, 0)
        for r in heads:
            s = buf_ref[r]
            p_prev = jnp.exp2(s[:MOBA_BLOCK] - jnp.where(on_row(n, r, ROW_PREV), ms[r], -NEG))
            p_own = jnp.exp2(s[MOBA_BLOCK:] - ms[r])
            p = jnp.concatenate([p_prev, p_own], axis=0).astype(jnp.bfloat16)
            m_ref[n, r] = ms[r]
            acc_ref[n, r] = jnp.dot(vt_ext((jp, n), r), p, preferred_element_type=jnp.float32)

    def near_body(it, m_a):
        n = 2 * it
        m_b = near_logits(n + 1, sb_ref)
        near_softmax(n, sa_ref, m_a)
        m_a = near_logits(jnp.minimum(n + 2, nblk - 1), sa_ref)
        near_softmax(n + 1, sb_ref, m_b)
        return tuple(m_a)

    lax.fori_loop(0, nblk // 2, near_body, tuple(near_logits(0, sa_ref)))

    def far_logits(k, buf_ref):
        n, c = nof_ref[k], cof_ref[k]
        kc = k_blk(c * FAR_CHUNK, FAR_CHUNK)
        out = []
        for r in heads:
            s = jnp.dot(kc, qpad(n, r), preferred_element_type=jnp.float32)
            buf_ref[r] = s
            mc = jnp.full((1, MOBA_BLOCK), -jnp.inf, jnp.float32)
            for i in range(FAR_CHUNK):
                bm = jnp.max(s[i * MOBA_BLOCK:(i + 1) * MOBA_BLOCK], axis=0, keepdims=True)
                mc = jnp.maximum(mc, jnp.where(on_row(n, r, c * FAR_CHUNK + i), bm + c31[r], -jnp.inf))
            out.append(mc)
        return out

    def far_softmax(k, buf_ref, mcs):
        n, c = nof_ref[k], cof_ref[k]
        for r in heads:
            m = m_ref[n, r]
            m_new = jnp.maximum(m, mcs[r])
            alpha = jnp.exp2(m - m_new)
            shift = m_new - c31[r]
            s = buf_ref[r]
            ps = []
            for i in range(FAR_CHUNK):
                on = on_row(n, r, c * FAR_CHUNK + i)
                p = jnp.exp2(s[i * MOBA_BLOCK:(i + 1) * MOBA_BLOCK] - jnp.where(on, shift, -NEG))
                ps.append(p.astype(jnp.bfloat16))
            vt = vt_ext([c * FAR_CHUNK + i for i in range(FAR_CHUNK)], r)
            m_ref[n, r] = m_new
            acc_ref[n, r] = alpha * acc_ref[n, r] + jnp.dot(
                vt, jnp.concatenate(ps, axis=0), preferred_element_type=jnp.float32)

    def far_body(it, mc_a):
        for u in range(FAR_UNROLL // 2):
            k = FAR_UNROLL * it + 2 * u
            mc_b = far_logits(k + 1, sb_ref)
            far_softmax(k, sa_ref, mc_a)
            mc_a = far_logits(jnp.minimum(k + 2, nitems - 1), sa_ref)
            far_softmax(k + 1, sb_ref, mc_b)
        return tuple(mc_a)

    lax.fori_loop(0, nitems // FAR_UNROLL, far_body, tuple(far_logits(0, sa_ref)))

    def out_body(it, _):
        for u in range(2):
            n = 2 * it + u
            ot = jnp.concatenate([acc_ref[n, r, :HEAD_DIM, :] / acc_ref[n, r, HEAD_DIM:HEAD_DIM + 1, :]
                                  for r in heads], axis=0)
            o_ref[0, pl.ds(pl.multiple_of(n * MOBA_BLOCK, MOBA_BLOCK), MOBA_BLOCK), :] = (
                ot.T.astype(jnp.bfloat16))
        return 0

    lax.fori_loop(0, nblk // 2, out_body, 0)


def _moba(dm, krm, sel, bias_near, c31):
    B, NB = dm.shape[0], dm.shape[1]
    S = NB * MOBA_BLOCK
    pairs = B_HEADS // 2
    vb_blk = DM_VB // LANE
    kb_blk = A_KV_WIDTH // LANE
    n_of, c_of = _far_schedule(NB)
    smem = pl.BlockSpec(memory_space=pltpu.SMEM)
    return pl.pallas_call(
        _moba_kernel,
        grid=(B, pairs),
        in_specs=[
            smem, smem, smem,
            pl.BlockSpec((1, NB, 2 * HEAD_PAD, MOBA_BLOCK), lambda b, p: (b, 0, p, 0)),
            pl.BlockSpec((1, S, LANE), lambda b, p: (b, 0, kb_blk + p)),
            pl.BlockSpec((1, NB, LANE, MOBA_BLOCK), lambda b, p: (b, 0, vb_blk + p, 0)),
            pl.BlockSpec((1, 2, NB, SEL_ROWS, MOBA_BLOCK), lambda b, p: (b, p, 0, 0, 0)),
            pl.BlockSpec((2, 2 * MOBA_BLOCK, MOBA_BLOCK), lambda b, p: (p, 0, 0)),
        ],
        out_specs=pl.BlockSpec((1, S, LANE), lambda b, p: (b, 0, p)),
        out_shape=jax.ShapeDtypeStruct((B, S, B_WIDTH), jnp.bfloat16),
        scratch_shapes=[
            pltpu.VMEM((2, FAR_CHUNK * MOBA_BLOCK, MOBA_BLOCK), jnp.float32),
            pltpu.VMEM((2, FAR_CHUNK * MOBA_BLOCK, MOBA_BLOCK), jnp.float32),
            pltpu.VMEM((NB, 2, 1, MOBA_BLOCK), jnp.float32),
            pltpu.VMEM((NB, 2, HEAD_DIM + ONES_ROWS, MOBA_BLOCK), jnp.float32),
        ],
        compiler_params=pltpu.CompilerParams(
            dimension_semantics=("parallel", "parallel"), vmem_limit_bytes=VMEM_LIMIT),
        name="moba",
    )(jnp.asarray(n_of), jnp.asarray(c_of), c31, dm, krm, dm, sel, bias_near)


def _final_kernel(x_ref, qt_ref, vcur_ref, vprev_ref, kcur_ref, kprev_ref, bias_ref, sink_ref,
                  ob_ref, wzg_ref, bg_ref, woa_ref, wob_ref, wout_ref, gamma_ref, beta_ref, y_ref,
                  oa_ref, s_ref, *, tiles_per_row, n_tiles):
    s = pl.program_id(0)

    @pl.when(s == 0)
    def _():
        oa_ref[...] = jnp.zeros(oa_ref.shape, jnp.float32)

    t_next = jnp.minimum(s, n_tiles - 1) % tiles_per_row
    _swa_logits(t_next, qt_ref, kcur_ref, kprev_ref, bias_ref, s_ref)
    oa = oa_ref[...]
    vcat = jnp.concatenate([vprev_ref[0, 0][:, WINDOW:], vcur_ref[0, 0]], axis=1)
    units = [[None] * A_KV_HEADS for _ in range(SWA_TOKENS // WINDOW)]
    x = x_ref[0]
    xb = x.astype(jnp.bfloat16)
    zw = A_WIDTH + B_WIDTH
    z = jnp.dot(xb, wzg_ref[:, :zw], preferred_element_type=jnp.float32)
    units[0][0] = _swa_unit(0, 0, s_ref, vcat, sink_ref)
    ga = jnp.dot(xb, wzg_ref[:, zw:zw + D_MODEL],
                 preferred_element_type=jnp.float32) + bg_ref[:, :D_MODEL]
    units[0][1] = _swa_unit(0, 1, s_ref, vcat, sink_ref)
    gb = jnp.dot(xb, wzg_ref[:, zw + D_MODEL:],
                 preferred_element_type=jnp.float32) + bg_ref[:, D_MODEL:]
    units[1][0] = _swa_unit(1, 0, s_ref, vcat, sink_ref)
    za, zb = z[:, :A_WIDTH], z[:, A_WIDTH:]
    ua = oa * (za * jax.nn.sigmoid(za))
    ub = ob_ref[0].astype(jnp.float32) * (zb * jax.nn.sigmoid(zb))
    ya = jnp.dot(ua.astype(jnp.bfloat16), woa_ref[...], preferred_element_type=jnp.float32)
    yb = jnp.dot(ub.astype(jnp.bfloat16), wob_ref[...], preferred_element_type=jnp.float32)
    units[1][1] = _swa_unit(1, 1, s_ref, vcat, sink_ref)
    merged = jax.nn.sigmoid(ga) * ya + jax.nn.sigmoid(gb) * yb
    out = jnp.dot(merged.astype(jnp.bfloat16), wout_ref[...], preferred_element_type=jnp.float32)
    oa_ref[...] = _swa_assemble(units)
    r = DN_ALPHA * x + out
    mu = jnp.mean(r, axis=-1, keepdims=True)
    rc = r - mu
    var = jnp.mean(rc * rc, axis=-1, keepdims=True)
    y_ref[0] = rc * lax.rsqrt(var + LN_EPS) * gamma_ref[...] + beta_ref[...]


def _final(x, dm, krm, bias_a, sink_a, ob, w_zg, b_gate, w_oa, w_ob, w_out, gamma, beta):
    B, S, D = x.shape
    T = SWA_TOKENS
    nt = S // T
    n_tiles = B * nt
    qa_blk = DM_QA // A_WIDTH
    va_blk = DM_VA // A_KV_WIDTH

    def tile(s, lag):
        i = jnp.clip(s - lag, 0, n_tiles - 1)
        return i // nt, i % nt

    def swa_spec(shape, idx):
        return pl.BlockSpec(shape, lambda s: idx(*tile(s, 0)))

    def fin_spec(width, lag):
        return pl.BlockSpec((1, T, width), lambda s: tile(s, lag) + (0,))

    full = lambda a: pl.BlockSpec(a.shape, lambda s: (0,) * a.ndim)
    return pl.pallas_call(
        functools.partial(_final_kernel, tiles_per_row=nt, n_tiles=n_tiles),
        grid=(n_tiles + 1,),
        in_specs=[
            fin_spec(D, 1),
            swa_spec((1, 1, A_WIDTH, T), lambda b, t: (b, t, qa_blk, 0)),
            swa_spec((1, 1, A_KV_WIDTH, T), lambda b, t: (b, t, va_blk, 0)),
            swa_spec((1, 1, A_KV_WIDTH, T), lambda b, t: (b, jnp.maximum(t - 1, 0), va_blk, 0)),
            swa_spec((1, T, A_KV_WIDTH), lambda b, t: (b, t, 0)),
            swa_spec((1, WINDOW, A_KV_WIDTH), lambda b, t: (b, jnp.maximum(2 * t - 1, 0), 0)),
            full(bias_a), full(sink_a),
            fin_spec(B_WIDTH, 1),
            full(w_zg), full(b_gate), full(w_oa), full(w_ob), full(w_out), full(gamma), full(beta),
        ],
        out_specs=fin_spec(D, 1),
        out_shape=jax.ShapeDtypeStruct((B, S, D), jnp.float32),
        scratch_shapes=[
            pltpu.VMEM((T, A_WIDTH), jnp.float32),
            pltpu.VMEM((T // WINDOW, A_KV_HEADS, 2 * WINDOW, A_GROUP * WINDOW), jnp.float32),
        ],
        compiler_params=pltpu.CompilerParams(
            dimension_semantics=("arbitrary",), vmem_limit_bytes=VMEM_LIMIT),
        name="final",
    )(x, dm, dm, dm, krm, krm, bias_a, sink_a, ob, w_zg, b_gate, w_oa, w_ob, w_out, gamma, beta)


def _toeplitz(v, rows, col0, ncols):
    H, L = v.shape
    assert col0 >= rows - 1 and col0 + ncols <= L - 1
    t = jnp.tile(v, (1, rows))[:, :rows * (L - 1)].reshape(H, rows, L - 1)
    return t[:, :, col0:col0 + ncols]


def _bias_tables(rel_table, sinks):
    table_a = rel_table[:, :A_HEADS].astype(jnp.float32) * LOG2E
    table_b = rel_table[:, A_HEADS:].astype(jnp.float32) * LOG2E
    sinks = sinks.astype(jnp.float32) * LOG2E
    d_a = np.arange(-2 * WINDOW, 3 * WINDOW)
    ok_a = (d_a >= 0) & (d_a < WINDOW)
    v_a = jnp.where(ok_a[:, None], table_a[_t5_bucket_np(d_a)], NEG).T
    ba = _toeplitz(v_a, 2 * WINDOW, 3 * WINDOW, WINDOW)
    ba = ba.reshape(A_KV_HEADS, A_GROUP, 2 * WINDOW, WINDOW)
    ba = ba.transpose(0, 2, 1, 3).reshape(A_KV_HEADS, 2 * WINDOW, A_GROUP * WINDOW)
    sink = jnp.broadcast_to(sinks.reshape(A_KV_HEADS, 1, A_GROUP, 1),
                            (A_KV_HEADS, 1, A_GROUP, WINDOW)).reshape(A_KV_HEADS, 1, A_GROUP * WINDOW)
    d_b = np.arange(-MOBA_BLOCK, 3 * MOBA_BLOCK)
    v_b = jnp.where((d_b >= 0)[:, None], table_b[_t5_bucket_np(d_b)], NEG).T
    b_own = _toeplitz(v_b, MOBA_BLOCK, MOBA_BLOCK, MOBA_BLOCK)
    b_prev = _toeplitz(v_b, MOBA_BLOCK, 2 * MOBA_BLOCK, MOBA_BLOCK)
    b_near = jnp.concatenate([b_prev, b_own], axis=1)
    c31 = table_b[N_BUCKETS - 1]
    return ba, sink, b_near, c31


def kernel(x, w_in, b_gate, sinks, rel_table, w_out_a, w_out_b, w_out, ln_gamma, ln_beta):
    B, S, D = x.shape
    assert (D, w_in.shape[0]) == (D_MODEL, DEPTH) and S % PROJ_TOKENS == 0
    w = w_in[0]
    sizes = (A_WIDTH, A_KV_WIDTH, A_KV_WIDTH, A_WIDTH, B_WIDTH, B_WIDTH, B_WIDTH, B_WIDTH,
             D_MODEL, D_MODEL)
    offs = np.concatenate([[0], np.cumsum(sizes)])
    w_qa, w_ka, w_va, w_za, w_qb, w_kb, w_vb, w_zb, w_ga, w_gb = [
        w[:, offs[i]:offs[i + 1]] for i in range(len(sizes))]
    bf = jnp.bfloat16
    q_scale = ATTN_SCALE * LOG2E
    w_rm = jnp.concatenate([w_ka, w_kb], axis=1).astype(bf)
    w_dm_t = jnp.concatenate([w_qb * q_scale, w_qa * q_scale, w_vb, w_va], axis=1).T.astype(bf)
    w_zg = jnp.concatenate([w_za, w_zb, w_ga, w_gb], axis=1).astype(bf)
    bias_a, sink_a, b_near, c31 = _bias_tables(rel_table, sinks[0])

    krm, dm, sel = _proj(x, w_rm, w_dm_t)
    ob = _moba(dm, krm, sel, b_near, c31)
    return _final(x, dm, krm, bias_a, sink_a, ob, w_zg, b_gate[0][None, :],
                  w_out_a[0].astype(bf), w_out_b[0].astype(bf), w_out[0].astype(bf),
                  ln_gamma[0][None, :], ln_beta[0][None, :])
```

```python
import functools
import math

import numpy as np
import jax
import jax.numpy as jnp
from jax import lax
from jax.experimental import pallas as pl
from jax.experimental.pallas import tpu as pltpu

D_MODEL = 1024
HEAD_DIM = 64
A_HEADS = 8
A_KV_HEADS = 2
A_GROUP = A_HEADS // A_KV_HEADS
A_WIDTH = A_HEADS * HEAD_DIM
A_KV_WIDTH = A_KV_HEADS * HEAD_DIM
WINDOW = 128
B_HEADS = 8
B_WIDTH = B_HEADS * HEAD_DIM
MOBA_BLOCK = 256
MOBA_TOPK = 3
N_BUCKETS = 32
MAX_DISTANCE = 128
DEPTH = 1
DN_ALPHA = (2.0 * DEPTH) ** 0.25
LN_EPS = 1e-5
NEG = -1e30
ATTN_SCALE = HEAD_DIM ** -0.5
LOG2E = math.log2(math.e)

LANE = 128
HEAD_PAD = 2 * HEAD_DIM
W_QB, W_QA, W_VB, W_VA = 0, B_WIDTH, B_WIDTH + A_WIDTH, 2 * B_WIDTH + A_WIDTH
W_ROWS = W_VA + A_KV_WIDTH
DM_QB, DM_QA = 0, B_HEADS * HEAD_PAD
DM_VB = DM_QA + A_WIDTH
DM_VA = DM_VB + B_WIDTH
DM_ROWS = DM_VA + A_KV_WIDTH
RM_COLS = A_KV_WIDTH + B_WIDTH
PROJ_TOKENS = 512
SWA_TOKENS = 256
FAR_CHUNK = 2
FAR_UNROLL = 16
ONES_ROWS = 16
SEL_ROWS = 40
ROW_PREV, ROW_OWN = 32, 33
VMEM_LIMIT = 56 * 1024 * 1024


def _t5_bucket_np(dist):
    max_exact = N_BUCKETS // 2
    n = np.maximum(dist, 0)
    nf = np.maximum(n, 1).astype(np.float32)
    large = max_exact + (np.log(nf / max_exact) / math.log(MAX_DISTANCE / max_exact)
                         * (N_BUCKETS - max_exact)).astype(np.int32)
    large = np.minimum(large, N_BUCKETS - 1)
    return np.where(n < max_exact, n, large).astype(np.int32)


def _select_rows(gate, n):
    nblk = gate.shape[0]
    blk = lax.broadcasted_iota(jnp.int32, gate.shape, 0)
    work = jnp.where(blk < n, gate, -jnp.inf)
    sel = jnp.zeros(gate.shape, jnp.float32)
    for _ in range(MOBA_TOPK):
        mx = jnp.max(work, axis=0, keepdims=True)
        idx = jnp.min(jnp.where(work == mx, blk, nblk), axis=0, keepdims=True)
        idx = jnp.where(mx > -jnp.inf, idx, nblk)
        pick = blk == idx
        sel = jnp.where(pick, 1.0, sel)
        work = jnp.where(pick, -jnp.inf, work)
    on_prev = jnp.max(jnp.where(blk == n - 1, sel, 0.0), axis=0, keepdims=True)
    far = jnp.where(blk < n - 1, sel, 0.0)
    row8 = lax.broadcasted_iota(jnp.int32, (SEL_ROWS - nblk, gate.shape[1]), 0)
    near = jnp.where(row8 == 0, on_prev, jnp.where(row8 == 1, 1.0, 0.0))
    return far, near


def _proj_kernel(x_ref, wrm_ref, wdm_ref, krm_ref, dm_ref, sel_ref, kmean_ref):
    t = pl.program_id(1)
    blocks = PROJ_TOKENS // MOBA_BLOCK
    nblk = kmean_ref.shape[0]

    @pl.when(t == 0)
    def _():
        kmean_ref[...] = jnp.zeros(kmean_ref.shape, jnp.float32)

    xb = x_ref[0].astype(jnp.bfloat16)
    rm = jnp.dot(xb, wrm_ref[...], preferred_element_type=jnp.float32)
    krm_ref[0] = rm.astype(jnp.bfloat16)
    for c in range(blocks):
        kb = rm[c * MOBA_BLOCK:(c + 1) * MOBA_BLOCK, A_KV_WIDTH:]
        kmean_ref[pl.ds(t * blocks + c, 1), :] = (
            jnp.sum(kb, axis=0, keepdims=True) * (1.0 / MOBA_BLOCK))
    dm = lax.dot_general(wdm_ref[...], xb, (((1,), (1,)), ((), ())),
                         preferred_element_type=jnp.float32)
    dmb = dm.astype(jnp.bfloat16)
    zeros = jnp.zeros((HEAD_DIM, MOBA_BLOCK), jnp.bfloat16)
    km = kmean_ref[...].astype(jnp.bfloat16)
    for c in range(blocks):
        piece = dmb[:, c * MOBA_BLOCK:(c + 1) * MOBA_BLOCK]
        rows = []
        for h in range(B_HEADS):
            q = piece[W_QB + h * HEAD_DIM:W_QB + (h + 1) * HEAD_DIM]
            qpad = jnp.concatenate([q, zeros] if h % 2 == 0 else [zeros, q], axis=0)
            rows.append(qpad)
            pair = h // 2
            gate = jnp.dot(km[:, pair * HEAD_PAD:(pair + 1) * HEAD_PAD], qpad,
                           preferred_element_type=jnp.float32)
            far, near = _select_rows(gate, t * blocks + c)
            sel_ref[0, h, c, :nblk, :] = far
            sel_ref[0, h, c, nblk:, :] = near
        dm_ref[0, c] = jnp.concatenate(rows + [piece[W_QA:]], axis=0)


def _proj(x, w_rm, w_dm_t):
    B, S, D = x.shape
    nt = S // PROJ_TOKENS
    cpt = PROJ_TOKENS // MOBA_BLOCK
    NB = S // MOBA_BLOCK
    return pl.pallas_call(
        _proj_kernel,
        grid=(B, nt),
        in_specs=[
            pl.BlockSpec((1, PROJ_TOKENS, D), lambda b, t: (b, t, 0)),
            pl.BlockSpec((D, RM_COLS), lambda b, t: (0, 0)),
            pl.BlockSpec((W_ROWS, D), lambda b, t: (0, 0)),
        ],
        out_specs=[
            pl.BlockSpec((1, PROJ_TOKENS, RM_COLS), lambda b, t: (b, t, 0)),
            pl.BlockSpec((1, cpt, DM_ROWS, MOBA_BLOCK), lambda b, t: (b, t, 0, 0)),
            pl.BlockSpec((1, B_HEADS, cpt, SEL_ROWS, MOBA_BLOCK), lambda b, t: (b, 0, t, 0, 0)),
        ],
        out_shape=[
            jax.ShapeDtypeStruct((B, S, RM_COLS), jnp.bfloat16),
            jax.ShapeDtypeStruct((B, NB, DM_ROWS, MOBA_BLOCK), jnp.bfloat16),
            jax.ShapeDtypeStruct((B, B_HEADS, NB, SEL_ROWS, MOBA_BLOCK), jnp.float32),
        ],
        scratch_shapes=[pltpu.VMEM((NB, B_WIDTH), jnp.float32)],
        compiler_params=pltpu.CompilerParams(
            dimension_semantics=("parallel", "arbitrary"), vmem_limit_bytes=VMEM_LIMIT),
        name="proj",
    )(x, w_rm, w_dm_t)


def _swa_logits(t, qt_ref, kcur_ref, kprev_ref, bias_ref, s_ref):
    qt = qt_ref[0, 0]
    kcat = jnp.concatenate([kprev_ref[0], kcur_ref[0]], axis=0)
    kidx = lax.broadcasted_iota(jnp.int32, (2 * WINDOW, A_GROUP * WINDOW), 0)
    zeros_q = jnp.zeros((HEAD_DIM, A_GROUP * WINDOW), jnp.bfloat16)
    for w in range(SWA_TOKENS // WINDOW):
        k2 = kcat[w * WINDOW:(w + 2) * WINDOW, :]
        for g in range(A_KV_HEADS):
            q4 = jnp.concatenate(
                [qt[(g * A_GROUP + i) * HEAD_DIM:(g * A_GROUP + i + 1) * HEAD_DIM,
                    w * WINDOW:(w + 1) * WINDOW] for i in range(A_GROUP)], axis=1)
            qpad = jnp.concatenate([q4, zeros_q] if g == 0 else [zeros_q, q4], axis=0)
            s = jnp.dot(k2, qpad, preferred_element_type=jnp.float32) + bias_ref[g]
            if w == 0:
                s = jnp.where((t == 0) & (kidx < WINDOW), NEG, s)
            s_ref[w, g] = s


def _swa_unit(w, g, s_ref, vcat, sink_ref):
    s = s_ref[w, g]
    sink = sink_ref[g]
    m = jnp.maximum(jnp.max(s, axis=0, keepdims=True), sink)
    p = jnp.exp2(s - m)
    l = jnp.sum(p, axis=0, keepdims=True) + jnp.exp2(sink - m)
    v2 = vcat[g * HEAD_DIM:(g + 1) * HEAD_DIM, w * WINDOW:(w + 2) * WINDOW]
    o = jnp.dot(v2, p.astype(jnp.bfloat16), preferred_element_type=jnp.float32) / l
    return [o[:, i * WINDOW:(i + 1) * WINDOW] for i in range(A_GROUP)]


def _swa_assemble(units):
    return jnp.concatenate(
        [jnp.concatenate([o for g in range(A_KV_HEADS) for o in units[w][g]], axis=0).T
         for w in range(SWA_TOKENS // WINDOW)], axis=0)


def _far_schedule(nblk):
    items = [(n, c) for n in range(nblk) for c in range((max(n - 1, 0) + FAR_CHUNK - 1) // FAR_CHUNK)]
    assert len(items) % FAR_UNROLL == 0
    return (np.array([i[0] for i in items], np.int32), np.array([i[1] for i in items], np.int32))


def _moba_kernel(nof_ref, cof_ref, c31_ref, q_ref, k_ref, vt_ref, sel_ref, bnear_ref, o_ref,
                 sa_ref, sb_ref, m_ref, acc_ref):
    pr = pl.program_id(1)
    nblk = q_ref.shape[1]
    nitems = nof_ref.shape[0]
    heads = (0, 1)
    c31 = [c31_ref[2 * pr + r] for r in heads]
    ones_rows = jnp.ones((ONES_ROWS, MOBA_BLOCK), jnp.bfloat16)

    def qpad(n, r):
        return q_ref[0, n, r * HEAD_PAD:(r + 1) * HEAD_PAD, :]

    def k_blk(j, nb=1):
        return k_ref[0, pl.ds(pl.multiple_of(j * MOBA_BLOCK, MOBA_BLOCK), nb * MOBA_BLOCK), :]

    def vt_ext(blocks, r):
        return jnp.concatenate(
            [jnp.concatenate([vt_ref[0, j, r * HEAD_DIM:(r + 1) * HEAD_DIM, :], ones_rows], axis=0)
             for j in blocks], axis=1)

    def on_row(n, r, row):
        return sel_ref[0, r, n, pl.ds(row, 1), :] > 0.5

    def near_logits(n, buf_ref):
        jp = jnp.maximum(n - 1, 0)
        kc = jnp.concatenate([k_blk(jp), k_blk(n)], axis=0)
        out = []
        for r in heads:
            s = jnp.dot(kc, qpad(n, r), preferred_element_type=jnp.float32) + bnear_ref[r]
            buf_ref[r] = s
            out.append(jnp.maximum(
                jnp.max(s[MOBA_BLOCK:], axis=0, keepdims=True),
                jnp.where(on_row(n, r, ROW_PREV),
                          jnp.max(s[:MOBA_BLOCK], axis=0, keepdims=True), -jnp.inf)))
        return out

    def near_softmax(n, buf_ref, ms):
        jp = jnp.maximum(n - 1, 0)
        for r in heads:
            s = buf_ref[r]
            p_prev = jnp.exp2(s[:MOBA_BLOCK] - jnp.where(on_row(n, r, ROW_PREV), ms[r], -NEG))
            p_own = jnp.exp2(s[MOBA_BLOCK:] - ms[r])
            p = jnp.concatenate([p_prev, p_own], axis=0).astype(jnp.bfloat16)
            m_ref[n, r] = ms[r]
            acc_ref[n, r] = jnp.dot(vt_ext((jp, n), r), p, preferred_element_type=jnp.float32)

    def near_body(it, m_a):
        n = 2 * it
        m_b = near_logits(n + 1, sb_ref)
        near_softmax(n, sa_ref, m_a)
        m_a = near_logits(jnp.minimum(n + 2, nblk - 1), sa_ref)
        near_softmax(n + 1, sb_ref, m_b)
        return tuple(m_a)

    lax.fori_loop(0, nblk // 2, near_body, tuple(near_logits(0, sa_ref)))

    def far_logits(k, buf_ref):
        n, c = nof_ref[k], cof_ref[k]
        kc = k_blk(c * FAR_CHUNK, FAR_CHUNK)
        out = []
        for r in heads:
            s = jnp.dot(kc, qpad(n, r), preferred_element_type=jnp.float32)
            buf_ref[r] = s
            mc = jnp.full((1, MOBA_BLOCK), -jnp.inf, jnp.float32)
            for i in range(FAR_CHUNK):
                bm = jnp.max(s[i * MOBA_BLOCK:(i + 1) * MOBA_BLOCK], axis=0, keepdims=True)
                mc = jnp.maximum(mc, jnp.where(on_row(n, r, c * FAR_CHUNK + i), bm + c31[r], -jnp.inf))
            out.append(mc)
        return out

    def far_softmax(k, buf_ref, mcs):
        n, c = nof_ref[k], cof_ref[k]
        for r in heads:
            m = m_ref[n, r]
            m_new = jnp.maximum(m, mcs[r])
            alpha = jnp.exp2(m - m_new)
            shift = m_new - c31[r]
            s = buf_ref[r]
            ps = []
            for i in range(FAR_CHUNK):
                on = on_row(n, r, c * FAR_CHUNK + i)
                p = jnp.exp2(s[i * MOBA_BLOCK:(i + 1) * MOBA_BLOCK] - jnp.where(on, shift, -NEG))
                ps.append(p.astype(jnp.bfloat16))
            vt = vt_ext([c * FAR_CHUNK + i for i in range(FAR_CHUNK)], r)
            m_ref[n, r] = m_new
            acc_ref[n, r] = alpha * acc_ref[n, r] + jnp.dot(
                vt, jnp.concatenate(ps, axis=0), preferred_element_type=jnp.float32)

    def far_body(it, mc_a):
        for u in range(FAR_UNROLL // 2):
            k = FAR_UNROLL * it + 2 * u
            mc_b = far_logits(k + 1, sb_ref)
            far_softmax(k, sa_ref, mc_a)
            mc_a = far_logits(jnp.minimum(k + 2, nitems - 1), sa_ref)
            far_softmax(k + 1, sb_ref, mc_b)
        return tuple(mc_a)

    lax.fori_loop(0, nitems // FAR_UNROLL, far_body, tuple(far_logits(0, sa_ref)))

    def out_body(it, _):
        for u in range(2):
            n = 2 * it + u
            ot = jnp.concatenate([acc_ref[n, r, :HEAD_DIM, :] / acc_ref[n, r, HEAD_DIM:HEAD_DIM + 1, :]
                                  for r in heads], axis=0)
            o_ref[0, pl.ds(pl.multiple_of(n * MOBA_BLOCK, MOBA_BLOCK), MOBA_BLOCK), :] = (
                ot.T.astype(jnp.bfloat16))
        return 0

    lax.fori_loop(0, nblk // 2, out_body, 0)


def _moba(dm, krm, sel, bias_near, c31):
    B, NB = dm.shape[0], dm.shape[1]
    S = NB * MOBA_BLOCK
    pairs = B_HEADS // 2
    vb_blk = DM_VB // LANE
    kb_blk = A_KV_WIDTH // LANE
    n_of, c_of = _far_schedule(NB)
    smem = pl.BlockSpec(memory_space=pltpu.SMEM)
    return pl.pallas_call(
        _moba_kernel,
        grid=(B, pairs),
        in_specs=[
            smem, smem, smem,
            pl.BlockSpec((1, NB, 2 * HEAD_PAD, MOBA_BLOCK), lambda b, p: (b, 0, p, 0)),
            pl.BlockSpec((1, S, LANE), lambda b, p: (b, 0, kb_blk + p)),
            pl.BlockSpec((1, NB, LANE, MOBA_BLOCK), lambda b, p: (b, 0, vb_blk + p, 0)),
            pl.BlockSpec((1, 2, NB, SEL_ROWS, MOBA_BLOCK), lambda b, p: (b, p, 0, 0, 0)),
            pl.BlockSpec((2, 2 * MOBA_BLOCK, MOBA_BLOCK), lambda b, p: (p, 0, 0)),
        ],
        out_specs=pl.BlockSpec((1, S, LANE), lambda b, p: (b, 0, p)),
        out_shape=jax.ShapeDtypeStruct((B, S, B_WIDTH), jnp.bfloat16),
        scratch_shapes=[
            pltpu.VMEM((2, FAR_CHUNK * MOBA_BLOCK, MOBA_BLOCK), jnp.float32),
            pltpu.VMEM((2, FAR_CHUNK * MOBA_BLOCK, MOBA_BLOCK), jnp.float32),
            pltpu.VMEM((NB, 2, 1, MOBA_BLOCK), jnp.float32),
            pltpu.VMEM((NB, 2, HEAD_DIM + ONES_ROWS, MOBA_BLOCK), jnp.float32),
        ],
        compiler_params=pltpu.CompilerParams(
            dimension_semantics=("parallel", "parallel"), vmem_limit_bytes=VMEM_LIMIT),
        name="moba",
    )(jnp.asarray(n_of), jnp.asarray(c_of), c31, dm, krm, dm, sel, bias_near)


def _final_kernel(x_ref, qt_ref, vcur_ref, vprev_ref, kcur_ref, kprev_ref, bias_ref, sink_ref,
                  ob_ref, wzg_ref, bg_ref, woa_ref, wob_ref, wout_ref, gamma_ref, beta_ref, y_ref,
                  oa_ref, s_ref, *, tiles_per_row, n_tiles):
    s = pl.program_id(0)

    @pl.when(s == 0)
    def _():
        oa_ref[...] = jnp.zeros(oa_ref.shape, jnp.float32)

    t_next = jnp.minimum(s, n_tiles - 1) % tiles_per_row
    _swa_logits(t_next, qt_ref, kcur_ref, kprev_ref, bias_ref, s_ref)
    oa = oa_ref[...]
    vcat = jnp.concatenate([vprev_ref[0, 0][:, WINDOW:], vcur_ref[0, 0]], axis=1)
    units = [[None] * A_KV_HEADS for _ in range(SWA_TOKENS // WINDOW)]
    x = x_ref[0]
    xb = x.astype(jnp.bfloat16)
    zw = A_WIDTH + B_WIDTH
    z = jnp.dot(xb, wzg_ref[:, :zw], preferred_element_type=jnp.float32)
    units[0][0] = _swa_unit(0, 0, s_ref, vcat, sink_ref)
    ga = jnp.dot(xb, wzg_ref[:, zw:zw + D_MODEL],
                 preferred_element_type=jnp.float32) + bg_ref[:, :D_MODEL]
    units[0][1] = _swa_unit(0, 1, s_ref, vcat, sink_ref)
    gb = jnp.dot(xb, wzg_ref[:, zw + D_MODEL:],
                 preferred_element_type=jnp.float32) + bg_ref[:, D_MODEL:]
    units[1][0] = _swa_unit(1, 0, s_ref, vcat, sink_ref)
    za, zb = z[:, :A_WIDTH], z[:, A_WIDTH:]
    ua = oa * (za * jax.nn.sigmoid(za))
    ub = ob_ref[0].astype(jnp.float32) * (zb * jax.nn.sigmoid(zb))
    ya = jnp.dot(ua.astype(jnp.bfloat16), woa_ref[...], preferred_element_type=jnp.float32)
    yb = jnp.dot(ub.astype(jnp.bfloat16), wob_ref[...], preferred_element_type=jnp.float32)
    units[1][1] = _swa_unit(1, 1, s_ref, vcat, sink_ref)
    merged = jax.nn.sigmoid(ga) * ya + jax.nn.sigmoid(gb) * yb
    out = jnp.dot(merged.astype(jnp.bfloat16), wout_ref[...], preferred_element_type=jnp.float32)
    oa_ref[...] = _swa_assemble(units)
    r = DN_ALPHA * x + out
    mu = jnp.mean(r, axis=-1, keepdims=True)
    rc = r - mu
    var = jnp.mean(rc * rc, axis=-1, keepdims=True)
    y_ref[0] = rc * lax.rsqrt(var + LN_EPS) * gamma_ref[...] + beta_ref[...]


def _final(x, dm, krm, bias_a, sink_a, ob, w_zg, b_gate, w_oa, w_ob, w_out, gamma, beta):
    B, S, D = x.shape
    T = SWA_TOKENS
    nt = S // T
    n_tiles = B * nt
    qa_blk = DM_QA // A_WIDTH
    va_blk = DM_VA // A_KV_WIDTH

    def tile(s, lag):
        i = jnp.clip(s - lag, 0, n_tiles - 1)
        return i // nt, i % nt

    def swa_spec(shape, idx):
        return pl.BlockSpec(shape, lambda s: idx(*tile(s, 0)))

    def fin_spec(width, lag):
        return pl.BlockSpec((1, T, width), lambda s: tile(s, lag) + (0,))

    full = lambda a: pl.BlockSpec(a.shape, lambda s: (0,) * a.ndim)
    return pl.pallas_call(
        functools.partial(_final_kernel, tiles_per_row=nt, n_tiles=n_tiles),
        grid=(n_tiles + 1,),
        in_specs=[
            fin_spec(D, 1),
            swa_spec((1, 1, A_WIDTH, T), lambda b, t: (b, t, qa_blk, 0)),
            swa_spec((1, 1, A_KV_WIDTH, T), lambda b, t: (b, t, va_blk, 0)),
            swa_spec((1, 1, A_KV_WIDTH, T), lambda b, t: (b, jnp.maximum(t - 1, 0), va_blk, 0)),
            swa_spec((1, T, A_KV_WIDTH), lambda b, t: (b, t, 0)),
            swa_spec((1, WINDOW, A_KV_WIDTH), lambda b, t: (b, jnp.maximum(2 * t - 1, 0), 0)),
            full(bias_a), full(sink_a),
            fin_spec(B_WIDTH, 1),
            full(w_zg), full(b_gate), full(w_oa), full(w_ob), full(w_out), full(gamma), full(beta),
        ],
        out_specs=fin_spec(D, 1),
        out_shape=jax.ShapeDtypeStruct((B, S, D), jnp.float32),
        scratch_shapes=[
            pltpu.VMEM((T, A_WIDTH), jnp.float32),
            pltpu.VMEM((T // WINDOW, A_KV_HEADS, 2 * WINDOW, A_GROUP * WINDOW), jnp.float32),
        ],
        compiler_params=pltpu.CompilerParams(
            dimension_semantics=("arbitrary",), vmem_limit_bytes=VMEM_LIMIT),
        name="final",
    )(x, dm, dm, dm, krm, krm, bias_a, sink_a, ob, w_zg, b_gate, w_oa, w_ob, w_out, gamma, beta)


def _toeplitz(v, rows, col0, ncols):
    H, L = v.shape
    assert col0 >= rows - 1 and col0 + ncols <= L - 1
    t = jnp.tile(v, (1, rows))[:, :rows * (L - 1)].reshape(H, rows, L - 1)
    return t[:, :, col0:col0 + ncols]


def _bias_tables(rel_table, sinks):
    table_a = rel_table[:, :A_HEADS].astype(jnp.float32) * LOG2E
    table_b = rel_table[:, A_HEADS:].astype(jnp.float32) * LOG2E
    sinks = sinks.astype(jnp.float32) * LOG2E
    d_a = np.arange(-2 * WINDOW, 3 * WINDOW)
    ok_a = (d_a >= 0) & (d_a < WINDOW)
    v_a = jnp.where(ok_a[:, None], table_a[_t5_bucket_np(d_a)], NEG).T
    ba = _toeplitz(v_a, 2 * WINDOW, 3 * WINDOW, WINDOW)
    ba = ba.reshape(A_KV_HEADS, A_GROUP, 2 * WINDOW, WINDOW)
    ba = ba.transpose(0, 2, 1, 3).reshape(A_KV_HEADS, 2 * WINDOW, A_GROUP * WINDOW)
    sink = jnp.broadcast_to(sinks.reshape(A_KV_HEADS, 1, A_GROUP, 1),
                            (A_KV_HEADS, 1, A_GROUP, WINDOW)).reshape(A_KV_HEADS, 1, A_GROUP * WINDOW)
    d_b = np.arange(-MOBA_BLOCK, 3 * MOBA_BLOCK)
    v_b = jnp.where((d_b >= 0)[:, None], table_b[_t5_bucket_np(d_b)], NEG).T
    b_own = _toeplitz(v_b, MOBA_BLOCK, MOBA_BLOCK, MOBA_BLOCK)
    b_prev = _toeplitz(v_b, MOBA_BLOCK, 2 * MOBA_BLOCK, MOBA_BLOCK)
    b_near = jnp.concatenate([b_prev, b_own], axis=1)
    c31 = table_b[N_BUCKETS - 1]
    return ba, sink, b_near, c31


def kernel(x, w_in, b_gate, sinks, rel_table, w_out_a, w_out_b, w_out, ln_gamma, ln_beta):
    B, S, D = x.shape
    assert (D, w_in.shape[0]) == (D_MODEL, DEPTH) and S % PROJ_TOKENS == 0
    w = w_in[0]
    sizes = (A_WIDTH, A_KV_WIDTH, A_KV_WIDTH, A_WIDTH, B_WIDTH, B_WIDTH, B_WIDTH, B_WIDTH,
             D_MODEL, D_MODEL)
    offs = np.concatenate([[0], np.cumsum(sizes)])
    w_qa, w_ka, w_va, w_za, w_qb, w_kb, w_vb, w_zb, w_ga, w_gb = [
        w[:, offs[i]:offs[i + 1]] for i in range(len(sizes))]
    bf = jnp.bfloat16
    q_scale = ATTN_SCALE * LOG2E
    w_rm = jnp.concatenate([w_ka, w_kb], axis=1).astype(bf)
    w_dm_t = jnp.concatenate([w_qb * q_scale, w_qa * q_scale, w_vb, w_va], axis=1).T.astype(bf)
    w_zg = jnp.concatenate([w_za, w_zb, w_ga, w_gb], axis=1).astype(bf)
    bias_a, sink_a, b_near, c31 = _bias_tables(rel_table, sinks[0])

    krm, dm, sel = _proj(x, w_rm, w_dm_t)
    ob = _moba(dm, krm, sel, b_near, c31)
    return _final(x, dm, krm, bias_a, sink_a, ob, w_zg, b_gate[0][None, :],
                  w_out_a[0].astype(bf), w_out_b[0].astype(bf), w_out[0].astype(bf),
                  ln_gamma[0][None, :], ln_beta[0][None, :])
```

```python
import functools
import math

import numpy as np
import jax
import jax.numpy as jnp
from jax import lax
from jax.experimental import pallas as pl
from jax.experimental.pallas import tpu as pltpu

D_MODEL = 1024
HEAD_DIM = 64
A_HEADS = 8
A_KV_HEADS = 2
A_GROUP = A_HEADS // A_KV_HEADS
A_WIDTH = A_HEADS * HEAD_DIM
A_KV_WIDTH = A_KV_HEADS * HEAD_DIM
WINDOW = 128
B_HEADS = 8
B_WIDTH = B_HEADS * HEAD_DIM
MOBA_BLOCK = 256
MOBA_TOPK = 3
N_BUCKETS = 32
MAX_DISTANCE = 128
DEPTH = 1
DN_ALPHA = (2.0 * DEPTH) ** 0.25
LN_EPS = 1e-5
NEG = -1e30
ATTN_SCALE = HEAD_DIM ** -0.5
LOG2E = math.log2(math.e)

LANE = 128
HEAD_PAD = 2 * HEAD_DIM
W_QB, W_QA, W_VB, W_VA = 0, B_WIDTH, B_WIDTH + A_WIDTH, 2 * B_WIDTH + A_WIDTH
W_ROWS = W_VA + A_KV_WIDTH
DM_QB, DM_QA = 0, B_HEADS * HEAD_PAD
DM_VB = DM_QA + A_WIDTH
DM_VA = DM_VB + B_WIDTH
DM_ROWS = DM_VA + A_KV_WIDTH
RM_COLS = A_KV_WIDTH + B_WIDTH
PROJ_TOKENS = 512
SWA_TOKENS = 256
FAR_CHUNK = 2
FAR_UNROLL = 16
ONES_ROWS = 16
SEL_PAD_ROWS = 8
VMEM_LIMIT = 56 * 1024 * 1024


def _t5_bucket_np(dist):
    max_exact = N_BUCKETS // 2
    n = np.maximum(dist, 0)
    nf = np.maximum(n, 1).astype(np.float32)
    large = max_exact + (np.log(nf / max_exact) / math.log(MAX_DISTANCE / max_exact)
                         * (N_BUCKETS - max_exact)).astype(np.int32)
    large = np.minimum(large, N_BUCKETS - 1)
    return np.where(n < max_exact, n, large).astype(np.int32)


def _select_rows(gate, n):
    nblk = gate.shape[0]
    blk = lax.broadcasted_iota(jnp.int32, gate.shape, 0)
    work = jnp.where(blk < n, gate, -jnp.inf)
    sel = jnp.zeros(gate.shape, jnp.float32)
    for _ in range(MOBA_TOPK):
        mx = jnp.max(work, axis=0, keepdims=True)
        idx = jnp.min(jnp.where(work == mx, blk, nblk), axis=0, keepdims=True)
        idx = jnp.where(mx > -jnp.inf, idx, nblk)
        pick = blk == idx
        sel = jnp.where(pick, 1.0, sel)
        work = jnp.where(pick, -jnp.inf, work)
    on_prev = jnp.max(jnp.where(blk == n - 1, sel, 0.0), axis=0, keepdims=True)
    far = jnp.where(blk < n - 1, sel, 0.0)
    pad_row = lax.broadcasted_iota(jnp.int32, (SEL_PAD_ROWS, gate.shape[1]), 0)
    return far, jnp.where(pad_row == 0, on_prev, 0.0)


def _proj_kernel(x_ref, wrm_ref, wdm_ref, krm_ref, dm_ref, sel_ref, kmean_ref):
    t = pl.program_id(1)
    blocks = PROJ_TOKENS // MOBA_BLOCK
    nblk = kmean_ref.shape[0]

    @pl.when(t == 0)
    def _():
        kmean_ref[...] = jnp.zeros(kmean_ref.shape, jnp.float32)

    xb = x_ref[0].astype(jnp.bfloat16)
    rm = jnp.dot(xb, wrm_ref[...], preferred_element_type=jnp.float32)
    krm_ref[0] = rm.astype(jnp.bfloat16)
    for c in range(blocks):
        kb = rm[c * MOBA_BLOCK:(c + 1) * MOBA_BLOCK, A_KV_WIDTH:]
        kmean_ref[pl.ds(t * blocks + c, 1), :] = (
            jnp.sum(kb, axis=0, keepdims=True) * (1.0 / MOBA_BLOCK))
    dm = lax.dot_general(wdm_ref[...], xb, (((1,), (1,)), ((), ())),
                         preferred_element_type=jnp.float32)
    dmb = dm.astype(jnp.bfloat16)
    zeros = jnp.zeros((HEAD_DIM, MOBA_BLOCK), jnp.bfloat16)
    km = kmean_ref[...].astype(jnp.bfloat16)
    for c in range(blocks):
        piece = dmb[:, c * MOBA_BLOCK:(c + 1) * MOBA_BLOCK]
        rows = []
        for h in range(B_HEADS):
            q = piece[W_QB + h * HEAD_DIM:W_QB + (h + 1) * HEAD_DIM]
            qpad = jnp.concatenate([q, zeros] if h % 2 == 0 else [zeros, q], axis=0)
            rows.append(qpad)
            pair = h // 2
            gate = jnp.dot(km[:, pair * HEAD_PAD:(pair + 1) * HEAD_PAD], qpad,
                           preferred_element_type=jnp.float32)
            far, near = _select_rows(gate, t * blocks + c)
            sel_ref[0, h, c, :nblk, :] = far
            sel_ref[0, h, c, nblk:, :] = near
        dm_ref[0, c] = jnp.concatenate(rows + [piece[W_QA:]], axis=0)


def _proj(x, w_rm, w_dm_t):
    B, S, D = x.shape
    nt = S // PROJ_TOKENS
    cpt = PROJ_TOKENS // MOBA_BLOCK
    NB = S // MOBA_BLOCK
    return pl.pallas_call(
        _proj_kernel,
        grid=(B, nt),
        in_specs=[
            pl.BlockSpec((1, PROJ_TOKENS, D), lambda b, t: (b, t, 0)),
            pl.BlockSpec((D, RM_COLS), lambda b, t: (0, 0)),
            pl.BlockSpec((W_ROWS, D), lambda b, t: (0, 0)),
        ],
        out_specs=[
            pl.BlockSpec((1, PROJ_TOKENS, RM_COLS), lambda b, t: (b, t, 0)),
            pl.BlockSpec((1, cpt, DM_ROWS, MOBA_BLOCK), lambda b, t: (b, t, 0, 0)),
            pl.BlockSpec((1, B_HEADS, cpt, NB + SEL_PAD_ROWS, MOBA_BLOCK),
                         lambda b, t: (b, 0, t, 0, 0)),
        ],
        out_shape=[
            jax.ShapeDtypeStruct((B, S, RM_COLS), jnp.bfloat16),
            jax.ShapeDtypeStruct((B, NB, DM_ROWS, MOBA_BLOCK), jnp.bfloat16),
            jax.ShapeDtypeStruct((B, B_HEADS, NB, NB + SEL_PAD_ROWS, MOBA_BLOCK), jnp.float32),
        ],
        scratch_shapes=[pltpu.VMEM((NB, B_WIDTH), jnp.float32)],
        compiler_params=pltpu.CompilerParams(
            dimension_semantics=("parallel", "arbitrary"), vmem_limit_bytes=VMEM_LIMIT),
        name="proj",
    )(x, w_rm, w_dm_t)


def _swa_logits(t, qt_ref, kcur_ref, kprev_ref, bias_ref, s_ref):
    qt = qt_ref[0, 0]
    kcat = jnp.concatenate([kprev_ref[0], kcur_ref[0]], axis=0)
    kidx = lax.broadcasted_iota(jnp.int32, (2 * WINDOW, A_GROUP * WINDOW), 0)
    zeros_q = jnp.zeros((HEAD_DIM, A_GROUP * WINDOW), jnp.bfloat16)
    for w in range(SWA_TOKENS // WINDOW):
        k2 = kcat[w * WINDOW:(w + 2) * WINDOW, :]
        for g in range(A_KV_HEADS):
            q4 = jnp.concatenate(
                [qt[(g * A_GROUP + i) * HEAD_DIM:(g * A_GROUP + i + 1) * HEAD_DIM,
                    w * WINDOW:(w + 1) * WINDOW] for i in range(A_GROUP)], axis=1)
            qpad = jnp.concatenate([q4, zeros_q] if g == 0 else [zeros_q, q4], axis=0)
            s = jnp.dot(k2, qpad, preferred_element_type=jnp.float32) + bias_ref[g]
            if w == 0:
                s = jnp.where((t == 0) & (kidx < WINDOW), NEG, s)
            s_ref[w, g] = s


def _swa_unit(w, g, s_ref, vcat, sink_ref):
    s = s_ref[w, g]
    sink = sink_ref[g]
    m = jnp.maximum(jnp.max(s, axis=0, keepdims=True), sink)
    p = jnp.exp2(s - m)
    l = jnp.sum(p, axis=0, keepdims=True) + jnp.exp2(sink - m)
    v2 = vcat[g * HEAD_DIM:(g + 1) * HEAD_DIM, w * WINDOW:(w + 2) * WINDOW]
    o = jnp.dot(v2, p.astype(jnp.bfloat16), preferred_element_type=jnp.float32) / l
    return [o[:, i * WINDOW:(i + 1) * WINDOW] for i in range(A_GROUP)]


def _swa_assemble(units):
    return jnp.concatenate(
        [jnp.concatenate([o for g in range(A_KV_HEADS) for o in units[w][g]], axis=0).T
         for w in range(SWA_TOKENS // WINDOW)], axis=0)


def _far_schedule(nblk):
    items = [(n, c) for n in range(nblk) for c in range((max(n - 1, 0) + FAR_CHUNK - 1) // FAR_CHUNK)]
    assert len(items) % FAR_UNROLL == 0
    return (np.array([i[0] for i in items], np.int32), np.array([i[1] for i in items], np.int32))


def _moba_kernel(nof_ref, cof_ref, c31_ref, q_ref, k_ref, vt_ref, sel_ref, bnear_ref, o_ref,
                 sa_ref, sb_ref, m_ref, acc_ref):
    pr = pl.program_id(1)
    nblk = q_ref.shape[1]
    nitems = nof_ref.shape[0]
    heads = (0, 1)
    c31 = [c31_ref[2 * pr + r] for r in heads]
    ones_rows = jnp.ones((ONES_ROWS, MOBA_BLOCK), jnp.bfloat16)

    def qpad(n, r):
        return q_ref[0, n, r * HEAD_PAD:(r + 1) * HEAD_PAD, :]

    def k_blk(j, nb=1):
        return k_ref[0, pl.ds(pl.multiple_of(j * MOBA_BLOCK, MOBA_BLOCK), nb * MOBA_BLOCK), :]

    def vt_ext(blocks, r):
        return jnp.concatenate(
            [jnp.concatenate([vt_ref[0, j, r * HEAD_DIM:(r + 1) * HEAD_DIM, :], ones_rows], axis=0)
             for j in blocks], axis=1)

    def on_row(n, r, row):
        return sel_ref[0, r, n, pl.ds(row, 1), :] > 0.5

    def near_logits(n, buf_ref):
        jp = jnp.maximum(n - 1, 0)
        kc = jnp.concatenate([k_blk(jp), k_blk(n)], axis=0)
        out = []
        for r in heads:
            s = jnp.dot(kc, qpad(n, r), preferred_element_type=jnp.float32) + bnear_ref[r]
            buf_ref[r] = s
            out.append(jnp.maximum(
                jnp.max(s[MOBA_BLOCK:], axis=0, keepdims=True),
                jnp.where(on_row(n, r, nblk),
                          jnp.max(s[:MOBA_BLOCK], axis=0, keepdims=True), -jnp.inf)))
        return out

    def near_softmax(n, buf_ref, ms):
        jp = jnp.maximum(n - 1, 0)
        for r in heads:
            s = buf_ref[r]
            p_prev = jnp.exp2(s[:MOBA_BLOCK] - jnp.where(on_row(n, r, nblk), ms[r], -NEG))
            p_own = jnp.exp2(s[MOBA_BLOCK:] - ms[r])
            p = jnp.concatenate([p_prev, p_own], axis=0).astype(jnp.bfloat16)
            m_ref[n, r] = ms[r]
            acc_ref[n, r] = jnp.dot(vt_ext((jp, n), r), p, preferred_element_type=jnp.float32)

    def near_body(it, m_a):
        n = 2 * it
        m_b = near_logits(n + 1, sb_ref)
        near_softmax(n, sa_ref, m_a)
        m_a = near_logits(jnp.minimum(n + 2, nblk - 1), sa_ref)
        near_softmax(n + 1, sb_ref, m_b)
        return tuple(m_a)

    lax.fori_loop(0, nblk // 2, near_body, tuple(near_logits(0, sa_ref)))

    def far_logits(k, buf_ref):
        n, c = nof_ref[k], cof_ref[k]
        kc = k_blk(c * FAR_CHUNK, FAR_CHUNK)
        out = []
        for r in heads:
            s = jnp.dot(kc, qpad(n, r), preferred_element_type=jnp.float32)
            buf_ref[r] = s
            mc = jnp.full((1, MOBA_BLOCK), -jnp.inf, jnp.float32)
            for i in range(FAR_CHUNK):
                bm = jnp.max(s[i * MOBA_BLOCK:(i + 1) * MOBA_BLOCK], axis=0, keepdims=True)
                mc = jnp.maximum(mc, jnp.where(on_row(n, r, c * FAR_CHUNK + i), bm + c31[r], -jnp.inf))
            out.append(mc)
        return out

    def far_softmax(k, buf_ref, mcs):
        n, c = nof_ref[k], cof_ref[k]
        for r in heads:
            m = m_ref[n, r]
            m_new = jnp.maximum(m, mcs[r])
            alpha = jnp.exp2(m - m_new)
            shift = m_new - c31[r]
            s = buf_ref[r]
            ps = []
            for i in range(FAR_CHUNK):
                on = on_row(n, r, c * FAR_CHUNK + i)
                p = jnp.exp2(s[i * MOBA_BLOCK:(i + 1) * MOBA_BLOCK] - jnp.where(on, shift, -NEG))
                ps.append(p.astype(jnp.bfloat16))
            vt = vt_ext([c * FAR_CHUNK + i for i in range(FAR_CHUNK)], r)
            m_ref[n, r] = m_new
            acc_ref[n, r] = alpha * acc_ref[n, r] + jnp.dot(
                vt, jnp.concatenate(ps, axis=0), preferred_element_type=jnp.float32)

    def far_body(it, mc_a):
        for u in range(FAR_UNROLL // 2):
            k = FAR_UNROLL * it + 2 * u
            mc_b = far_logits(k + 1, sb_ref)
            far_softmax(k, sa_ref, mc_a)
            mc_a = far_logits(jnp.minimum(k + 2, nitems - 1), sa_ref)
            far_softmax(k + 1, sb_ref, mc_b)
        return tuple(mc_a)

    lax.fori_loop(0, nitems // FAR_UNROLL, far_body, tuple(far_logits(0, sa_ref)))

    def out_body(it, _):
        for u in range(2):
            n = 2 * it + u
            ot = jnp.concatenate([acc_ref[n, r, :HEAD_DIM, :] / acc_ref[n, r, HEAD_DIM:HEAD_DIM + 1, :]
                                  for r in heads], axis=0)
            o_ref[0, pl.ds(pl.multiple_of(n * MOBA_BLOCK, MOBA_BLOCK), MOBA_BLOCK), :] = (
                ot.T.astype(jnp.bfloat16))
        return 0

    lax.fori_loop(0, nblk // 2, out_body, 0)


def _moba(dm, krm, sel, bias_near, c31):
    B, NB = dm.shape[0], dm.shape[1]
    S = NB * MOBA_BLOCK
    pairs = B_HEADS // 2
    vb_blk = DM_VB // LANE
    kb_blk = A_KV_WIDTH // LANE
    n_of, c_of = _far_schedule(NB)
    smem = pl.BlockSpec(memory_space=pltpu.SMEM)
    return pl.pallas_call(
        _moba_kernel,
        grid=(B, pairs),
        in_specs=[
            smem, smem, smem,
            pl.BlockSpec((1, NB, 2 * HEAD_PAD, MOBA_BLOCK), lambda b, p: (b, 0, p, 0)),
            pl.BlockSpec((1, S, LANE), lambda b, p: (b, 0, kb_blk + p)),
            pl.BlockSpec((1, NB, LANE, MOBA_BLOCK), lambda b, p: (b, 0, vb_blk + p, 0)),
            pl.BlockSpec((1, 2, NB, NB + SEL_PAD_ROWS, MOBA_BLOCK), lambda b, p: (b, p, 0, 0, 0)),
            pl.BlockSpec((2, 2 * MOBA_BLOCK, MOBA_BLOCK), lambda b, p: (p, 0, 0)),
        ],
        out_specs=pl.BlockSpec((1, S, LANE), lambda b, p: (b, 0, p)),
        out_shape=jax.ShapeDtypeStruct((B, S, B_WIDTH), jnp.bfloat16),
        scratch_shapes=[
            pltpu.VMEM((2, FAR_CHUNK * MOBA_BLOCK, MOBA_BLOCK), jnp.float32),
            pltpu.VMEM((2, FAR_CHUNK * MOBA_BLOCK, MOBA_BLOCK), jnp.float32),
            pltpu.VMEM((NB, 2, 1, MOBA_BLOCK), jnp.float32),
            pltpu.VMEM((NB, 2, HEAD_DIM + ONES_ROWS, MOBA_BLOCK), jnp.float32),
        ],
        compiler_params=pltpu.CompilerParams(
            dimension_semantics=("parallel", "parallel"), vmem_limit_bytes=VMEM_LIMIT),
        name="moba",
    )(jnp.asarray(n_of), jnp.asarray(c_of), c31, dm, krm, dm, sel, bias_near)


def _final_kernel(x_ref, qt_ref, vcur_ref, vprev_ref, kcur_ref, kprev_ref, bias_ref, sink_ref,
                  ob_ref, wzg_ref, bg_ref, woa_ref, wob_ref, wout_ref, gamma_ref, beta_ref, y_ref,
                  oa_ref, s_ref, *, tiles_per_row, n_tiles):
    s = pl.program_id(0)

    @pl.when(s == 0)
    def _():
        oa_ref[...] = jnp.zeros(oa_ref.shape, jnp.float32)

    t_next = jnp.minimum(s, n_tiles - 1) % tiles_per_row
    _swa_logits(t_next, qt_ref, kcur_ref, kprev_ref, bias_ref, s_ref)
    oa = oa_ref[...]
    vcat = jnp.concatenate([vprev_ref[0, 0][:, WINDOW:], vcur_ref[0, 0]], axis=1)
    units = [[None] * A_KV_HEADS for _ in range(SWA_TOKENS // WINDOW)]
    x = x_ref[0]
    xb = x.astype(jnp.bfloat16)
    zw = A_WIDTH + B_WIDTH
    z = jnp.dot(xb, wzg_ref[:, :zw], preferred_element_type=jnp.float32)
    units[0][0] = _swa_unit(0, 0, s_ref, vcat, sink_ref)
    ga = jnp.dot(xb, wzg_ref[:, zw:zw + D_MODEL],
                 preferred_element_type=jnp.float32) + bg_ref[:, :D_MODEL]
    units[0][1] = _swa_unit(0, 1, s_ref, vcat, sink_ref)
    gb = jnp.dot(xb, wzg_ref[:, zw + D_MODEL:],
                 preferred_element_type=jnp.float32) + bg_ref[:, D_MODEL:]
    units[1][0] = _swa_unit(1, 0, s_ref, vcat, sink_ref)
    za, zb = z[:, :A_WIDTH], z[:, A_WIDTH:]
    ua = oa * (za * jax.nn.sigmoid(za))
    ub = ob_ref[0].astype(jnp.float32) * (zb * jax.nn.sigmoid(zb))
    ya = jnp.dot(ua.astype(jnp.bfloat16), woa_ref[...], preferred_element_type=jnp.float32)
    yb = jnp.dot(ub.astype(jnp.bfloat16), wob_ref[...], preferred_element_type=jnp.float32)
    units[1][1] = _swa_unit(1, 1, s_ref, vcat, sink_ref)
    merged = jax.nn.sigmoid(ga) * ya + jax.nn.sigmoid(gb) * yb
    out = jnp.dot(merged.astype(jnp.bfloat16), wout_ref[...], preferred_element_type=jnp.float32)
    oa_ref[...] = _swa_assemble(units)
    r = DN_ALPHA * x + out
    mu = jnp.mean(r, axis=-1, keepdims=True)
    rc = r - mu
    var = jnp.mean(rc * rc, axis=-1, keepdims=True)
    y_ref[0] = rc * lax.rsqrt(var + LN_EPS) * gamma_ref[...] + beta_ref[...]


def _final(x, dm, krm, bias_a, sink_a, ob, w_zg, b_gate, w_oa, w_ob, w_out, gamma, beta):
    B, S, D = x.shape
    T = SWA_TOKENS
    nt = S // T
    n_tiles = B * nt
    qa_blk = DM_QA // A_WIDTH
    va_blk = DM_VA // A_KV_WIDTH

    def tile(s, lag):
        i = jnp.clip(s - lag, 0, n_tiles - 1)
        return i // nt, i % nt

    def swa_spec(shape, idx):
        return pl.BlockSpec(shape, lambda s: idx(*tile(s, 0)))

    def fin_spec(width, lag):
        return pl.BlockSpec((1, T, width), lambda s: tile(s, lag) + (0,))

    full = lambda a: pl.BlockSpec(a.shape, lambda s: (0,) * a.ndim)
    return pl.pallas_call(
        functools.partial(_final_kernel, tiles_per_row=nt, n_tiles=n_tiles),
        grid=(n_tiles + 1,),
        in_specs=[
            fin_spec(D, 1),
            swa_spec((1, 1, A_WIDTH, T), lambda b, t: (b, t, qa_blk, 0)),
            swa_spec((1, 1, A_KV_WIDTH, T), lambda b, t: (b, t, va_blk, 0)),
            swa_spec((1, 1, A_KV_WIDTH, T), lambda b, t: (b, jnp.maximum(t - 1, 0), va_blk, 0)),
            swa_spec((1, T, A_KV_WIDTH), lambda b, t: (b, t, 0)),
            swa_spec((1, WINDOW, A_KV_WIDTH), lambda b, t: (b, jnp.maximum(2 * t - 1, 0), 0)),
            full(bias_a), full(sink_a),
            fin_spec(B_WIDTH, 1),
            full(w_zg), full(b_gate), full(w_oa), full(w_ob), full(w_out), full(gamma), full(beta),
        ],
        out_specs=fin_spec(D, 1),
        out_shape=jax.ShapeDtypeStruct((B, S, D), jnp.float32),
        scratch_shapes=[
            pltpu.VMEM((T, A_WIDTH), jnp.float32),
            pltpu.VMEM((T // WINDOW, A_KV_HEADS, 2 * WINDOW, A_GROUP * WINDOW), jnp.float32),
        ],
        compiler_params=pltpu.CompilerParams(
            dimension_semantics=("arbitrary",), vmem_limit_bytes=VMEM_LIMIT),
        name="final",
    )(x, dm, dm, dm, krm, krm, bias_a, sink_a, ob, w_zg, b_gate, w_oa, w_ob, w_out, gamma, beta)


def _toeplitz(v, rows, col0, ncols):
    H, L = v.shape
    assert col0 >= rows - 1 and col0 + ncols <= L - 1
    t = jnp.tile(v, (1, rows))[:, :rows * (L - 1)].reshape(H, rows, L - 1)
    return t[:, :, col0:col0 + ncols]


def _bias_tables(rel_table, sinks):
    table_a = rel_table[:, :A_HEADS].astype(jnp.float32) * LOG2E
    table_b = rel_table[:, A_HEADS:].astype(jnp.float32) * LOG2E
    sinks = sinks.astype(jnp.float32) * LOG2E
    d_a = np.arange(-2 * WINDOW, 3 * WINDOW)
    ok_a = (d_a >= 0) & (d_a < WINDOW)
    v_a = jnp.where(ok_a[:, None], table_a[_t5_bucket_np(d_a)], NEG).T
    ba = _toeplitz(v_a, 2 * WINDOW, 3 * WINDOW, WINDOW)
    ba = ba.reshape(A_KV_HEADS, A_GROUP, 2 * WINDOW, WINDOW)
    ba = ba.transpose(0, 2, 1, 3).reshape(A_KV_HEADS, 2 * WINDOW, A_GROUP * WINDOW)
    sink = jnp.broadcast_to(sinks.reshape(A_KV_HEADS, 1, A_GROUP, 1),
                            (A_KV_HEADS, 1, A_GROUP, WINDOW)).reshape(A_KV_HEADS, 1, A_GROUP * WINDOW)
    d_b = np.arange(-MOBA_BLOCK, 3 * MOBA_BLOCK)
    v_b = jnp.where((d_b >= 0)[:, None], table_b[_t5_bucket_np(d_b)], NEG).T
    b_own = _toeplitz(v_b, MOBA_BLOCK, MOBA_BLOCK, MOBA_BLOCK)
    b_prev = _toeplitz(v_b, MOBA_BLOCK, 2 * MOBA_BLOCK, MOBA_BLOCK)
    b_near = jnp.concatenate([b_prev, b_own], axis=1)
    c31 = table_b[N_BUCKETS - 1]
    return ba, sink, b_near, c31


def kernel(x, w_in, b_gate, sinks, rel_table, w_out_a, w_out_b, w_out, ln_gamma, ln_beta):
    B, S, D = x.shape
    assert (D, w_in.shape[0]) == (D_MODEL, DEPTH) and S % PROJ_TOKENS == 0
    w = w_in[0]
    sizes = (A_WIDTH, A_KV_WIDTH, A_KV_WIDTH, A_WIDTH, B_WIDTH, B_WIDTH, B_WIDTH, B_WIDTH,
             D_MODEL, D_MODEL)
    offs = np.concatenate([[0], np.cumsum(sizes)])
    w_qa, w_ka, w_va, w_za, w_qb, w_kb, w_vb, w_zb, w_ga, w_gb = [
        w[:, offs[i]:offs[i + 1]] for i in range(len(sizes))]
    bf = jnp.bfloat16
    q_scale = ATTN_SCALE * LOG2E
    w_rm = jnp.concatenate([w_ka, w_kb], axis=1).astype(bf)
    w_dm_t = jnp.concatenate([w_qb * q_scale, w_qa * q_scale, w_vb, w_va], axis=1).T.astype(bf)
    w_zg = jnp.concatenate([w_za, w_zb, w_ga, w_gb], axis=1).astype(bf)
    bias_a, sink_a, b_near, c31 = _bias_tables(rel_table, sinks[0])

    krm, dm, sel = _proj(x, w_rm, w_dm_t)
    ob = _moba(dm, krm, sel, b_near, c31)
    return _final(x, dm, krm, bias_a, sink_a, ob, w_zg, b_gate[0][None, :],
                  w_out_a[0].astype(bf), w_out_b[0].astype(bf), w_out[0].astype(bf),
                  ln_gamma[0][None, :], ln_beta[0][None, :])
```

```python
import functools
import math

import numpy as np
import jax
import jax.numpy as jnp
from jax import lax
from jax.experimental import pallas as pl
from jax.experimental.pallas import tpu as pltpu

D_MODEL = 1024
HEAD_DIM = 64
A_HEADS = 8
A_KV_HEADS = 2
A_GROUP = A_HEADS // A_KV_HEADS
A_WIDTH = A_HEADS * HEAD_DIM
A_KV_WIDTH = A_KV_HEADS * HEAD_DIM
WINDOW = 128
B_HEADS = 8
B_WIDTH = B_HEADS * HEAD_DIM
MOBA_BLOCK = 256
MOBA_TOPK = 3
N_BUCKETS = 32
MAX_DISTANCE = 128
DEPTH = 1
DN_ALPHA = (2.0 * DEPTH) ** 0.25
LN_EPS = 1e-5
NEG = -1e30
ATTN_SCALE = HEAD_DIM ** -0.5
LOG2E = math.log2(math.e)

LANE = 128
HEAD_PAD = 2 * HEAD_DIM
W_QB, W_QA, W_VB, W_VA = 0, B_WIDTH, B_WIDTH + A_WIDTH, 2 * B_WIDTH + A_WIDTH
W_ROWS = W_VA + A_KV_WIDTH
DM_QB, DM_QA = 0, B_HEADS * HEAD_PAD
DM_VB = DM_QA + A_WIDTH
DM_VA = DM_VB + B_WIDTH
DM_ROWS = DM_VA + A_KV_WIDTH
RM_COLS = A_KV_WIDTH + B_WIDTH
PROJ_TOKENS = 512
SWA_TOKENS = 256
FAR_CHUNK = 2
FAR_UNROLL = 24
NEAR_UNROLL = 8
ONES_ROWS = 16
SEL_PAD_ROWS = 8
VMEM_LIMIT = 56 * 1024 * 1024


def _t5_bucket_np(dist):
    max_exact = N_BUCKETS // 2
    n = np.maximum(dist, 0)
    nf = np.maximum(n, 1).astype(np.float32)
    large = max_exact + (np.log(nf / max_exact) / math.log(MAX_DISTANCE / max_exact)
                         * (N_BUCKETS - max_exact)).astype(np.int32)
    large = np.minimum(large, N_BUCKETS - 1)
    return np.where(n < max_exact, n, large).astype(np.int32)


def _select_rows(gate, n):
    nblk = gate.shape[0]
    blk = lax.broadcasted_iota(jnp.int32, gate.shape, 0)
    work = jnp.where(blk < n, gate, -jnp.inf)
    sel = jnp.zeros(gate.shape, jnp.float32)
    for _ in range(MOBA_TOPK):
        mx = jnp.max(work, axis=0, keepdims=True)
        idx = jnp.min(jnp.where(work == mx, blk, nblk), axis=0, keepdims=True)
        idx = jnp.where(mx > -jnp.inf, idx, nblk)
        pick = blk == idx
        sel = jnp.where(pick, 1.0, sel)
        work = jnp.where(pick, -jnp.inf, work)
    on_prev = jnp.max(jnp.where(blk == n - 1, sel, 0.0), axis=0, keepdims=True)
    far = jnp.where(blk < n - 1, sel, 0.0)
    pad_row = lax.broadcasted_iota(jnp.int32, (SEL_PAD_ROWS, gate.shape[1]), 0)
    return far, jnp.where(pad_row == 0, on_prev, 0.0)


def _proj_kernel(x_ref, wrm_ref, wdm_ref, krm_ref, dm_ref, sel_ref, kmean_ref):
    t = pl.program_id(1)
    blocks = PROJ_TOKENS // MOBA_BLOCK
    nblk = kmean_ref.shape[0]

    @pl.when(t == 0)
    def _():
        kmean_ref[...] = jnp.zeros(kmean_ref.shape, jnp.float32)

    xb = x_ref[0].astype(jnp.bfloat16)
    rm = jnp.dot(xb, wrm_ref[...], preferred_element_type=jnp.float32)
    krm_ref[0] = rm.astype(jnp.bfloat16)
    for c in range(blocks):
        kb = rm[c * MOBA_BLOCK:(c + 1) * MOBA_BLOCK, A_KV_WIDTH:]
        kmean_ref[pl.ds(t * blocks + c, 1), :] = (
            jnp.sum(kb, axis=0, keepdims=True) * (1.0 / MOBA_BLOCK))
    dm = lax.dot_general(wdm_ref[...], xb, (((1,), (1,)), ((), ())),
                         preferred_element_type=jnp.float32)
    dmb = dm.astype(jnp.bfloat16)
    zeros = jnp.zeros((HEAD_DIM, MOBA_BLOCK), jnp.bfloat16)
    km = kmean_ref[...].astype(jnp.bfloat16)
    for c in range(blocks):
        piece = dmb[:, c * MOBA_BLOCK:(c + 1) * MOBA_BLOCK]
        rows = []
        for h in range(B_HEADS):
            q = piece[W_QB + h * HEAD_DIM:W_QB + (h + 1) * HEAD_DIM]
            qpad = jnp.concatenate([q, zeros] if h % 2 == 0 else [zeros, q], axis=0)
            rows.append(qpad)
            pair = h // 2
            gate = jnp.dot(km[:, pair * HEAD_PAD:(pair + 1) * HEAD_PAD], qpad,
                           preferred_element_type=jnp.float32)
            far, near = _select_rows(gate, t * blocks + c)
            sel_ref[0, h, c, :nblk, :] = far
            sel_ref[0, h, c, nblk:, :] = near
        dm_ref[0, c] = jnp.concatenate(rows + [piece[W_QA:]], axis=0)


def _proj(x, w_rm, w_dm_t):
    B, S, D = x.shape
    nt = S // PROJ_TOKENS
    cpt = PROJ_TOKENS // MOBA_BLOCK
    NB = S // MOBA_BLOCK
    return pl.pallas_call(
        _proj_kernel,
        grid=(B, nt),
        in_specs=[
            pl.BlockSpec((1, PROJ_TOKENS, D), lambda b, t: (b, t, 0)),
            pl.BlockSpec((D, RM_COLS), lambda b, t: (0, 0)),
            pl.BlockSpec((W_ROWS, D), lambda b, t: (0, 0)),
        ],
        out_specs=[
            pl.BlockSpec((1, PROJ_TOKENS, RM_COLS), lambda b, t: (b, t, 0)),
            pl.BlockSpec((1, cpt, DM_ROWS, MOBA_BLOCK), lambda b, t: (b, t, 0, 0)),
            pl.BlockSpec((1, B_HEADS, cpt, NB + SEL_PAD_ROWS, MOBA_BLOCK),
                         lambda b, t: (b, 0, t, 0, 0)),
        ],
        out_shape=[
            jax.ShapeDtypeStruct((B, S, RM_COLS), jnp.bfloat16),
            jax.ShapeDtypeStruct((B, NB, DM_ROWS, MOBA_BLOCK), jnp.bfloat16),
            jax.ShapeDtypeStruct((B, B_HEADS, NB, NB + SEL_PAD_ROWS, MOBA_BLOCK), jnp.float32),
        ],
        scratch_shapes=[pltpu.VMEM((NB, B_WIDTH), jnp.float32)],
        compiler_params=pltpu.CompilerParams(
            dimension_semantics=("parallel", "arbitrary"), vmem_limit_bytes=VMEM_LIMIT),
        name="proj",
    )(x, w_rm, w_dm_t)


def _swa_logits(t, qt_ref, kcur_ref, kprev_ref, bias_ref, s_ref):
    qt = qt_ref[0, 0]
    kcat = jnp.concatenate([kprev_ref[0], kcur_ref[0]], axis=0)
    kidx = lax.broadcasted_iota(jnp.int32, (2 * WINDOW, A_GROUP * WINDOW), 0)
    zeros_q = jnp.zeros((HEAD_DIM, A_GROUP * WINDOW), jnp.bfloat16)
    for w in range(SWA_TOKENS // WINDOW):
        k2 = kcat[w * WINDOW:(w + 2) * WINDOW, :]
        for g in range(A_KV_HEADS):
            q4 = jnp.concatenate(
                [qt[(g * A_GROUP + i) * HEAD_DIM:(g * A_GROUP + i + 1) * HEAD_DIM,
                    w * WINDOW:(w + 1) * WINDOW] for i in range(A_GROUP)], axis=1)
            qpad = jnp.concatenate([q4, zeros_q] if g == 0 else [zeros_q, q4], axis=0)
            s = jnp.dot(k2, qpad, preferred_element_type=jnp.float32) + bias_ref[g]
            if w == 0:
                s = jnp.where((t == 0) & (kidx < WINDOW), NEG, s)
            s_ref[w, g] = s


def _swa_unit(w, g, s_ref, vcat, sink_ref):
    s = s_ref[w, g]
    sink = sink_ref[g]
    m = jnp.maximum(jnp.max(s, axis=0, keepdims=True), sink)
    p = jnp.exp2(s - m)
    l = jnp.sum(p, axis=0, keepdims=True) + jnp.exp2(sink - m)
    v2 = vcat[g * HEAD_DIM:(g + 1) * HEAD_DIM, w * WINDOW:(w + 2) * WINDOW]
    o = jnp.dot(v2, p.astype(jnp.bfloat16), preferred_element_type=jnp.float32) / l
    return [o[:, i * WINDOW:(i + 1) * WINDOW] for i in range(A_GROUP)]


def _swa_assemble(units):
    return jnp.concatenate(
        [jnp.concatenate([o for g in range(A_KV_HEADS) for o in units[w][g]], axis=0).T
         for w in range(SWA_TOKENS // WINDOW)], axis=0)


def _far_schedule(nblk):
    items = [(n, c) for n in range(nblk) for c in range((max(n - 1, 0) + FAR_CHUNK - 1) // FAR_CHUNK)]
    assert len(items) % FAR_UNROLL == 0
    return (np.array([i[0] for i in items], np.int32), np.array([i[1] for i in items], np.int32))


def _moba_kernel(nof_ref, cof_ref, c31_ref, q_ref, k_ref, vt_ref, sel_ref, bnear_ref, o_ref,
                 sa_ref, sb_ref, m_ref, acc_ref):
    pr = pl.program_id(1)
    nblk = q_ref.shape[1]
    nitems = nof_ref.shape[0]
    heads = (0, 1)
    c31 = [c31_ref[2 * pr + r] for r in heads]
    ones_rows = jnp.ones((ONES_ROWS, MOBA_BLOCK), jnp.bfloat16)

    def qpad(n, r):
        return q_ref[0, n, r * HEAD_PAD:(r + 1) * HEAD_PAD, :]

    def k_blk(j, nb=1):
        return k_ref[0, pl.ds(pl.multiple_of(j * MOBA_BLOCK, MOBA_BLOCK), nb * MOBA_BLOCK), :]

    def vt_ext(blocks, r):
        return jnp.concatenate(
            [jnp.concatenate([vt_ref[0, j, r * HEAD_DIM:(r + 1) * HEAD_DIM, :], ones_rows], axis=0)
             for j in blocks], axis=1)

    def on_row(n, r, row):
        return sel_ref[0, r, n, pl.ds(row, 1), :] > 0.5

    def near_logits(n, buf_ref):
        jp = jnp.maximum(n - 1, 0)
        kc = jnp.concatenate([k_blk(jp), k_blk(n)], axis=0)
        out = []
        for r in heads:
            s = jnp.dot(kc, qpad(n, r), preferred_element_type=jnp.float32) + bnear_ref[r]
            buf_ref[r] = s
            out.append(jnp.maximum(
                jnp.max(s[MOBA_BLOCK:], axis=0, keepdims=True),
                jnp.where(on_row(n, r, nblk),
                          jnp.max(s[:MOBA_BLOCK], axis=0, keepdims=True), -jnp.inf)))
        return out

    def near_softmax(n, buf_ref, ms):
        jp = jnp.maximum(n - 1, 0)
        for r in heads:
            s = buf_ref[r]
            p_prev = jnp.exp2(s[:MOBA_BLOCK] - jnp.where(on_row(n, r, nblk), ms[r], -NEG))
            p_own = jnp.exp2(s[MOBA_BLOCK:] - ms[r])
            p = jnp.concatenate([p_prev, p_own], axis=0).astype(jnp.bfloat16)
            m_ref[n, r] = ms[r]
            acc_ref[n, r] = jnp.dot(vt_ext((jp, n), r), p, preferred_element_type=jnp.float32)

    def near_body(it, m_a):
        for u in range(NEAR_UNROLL // 2):
            n = NEAR_UNROLL * it + 2 * u
            m_b = near_logits(n + 1, sb_ref)
            near_softmax(n, sa_ref, m_a)
            m_a = near_logits(jnp.minimum(n + 2, nblk - 1), sa_ref)
            near_softmax(n + 1, sb_ref, m_b)
        return tuple(m_a)

    lax.fori_loop(0, nblk // NEAR_UNROLL, near_body, tuple(near_logits(0, sa_ref)))

    def far_logits(k, buf_ref):
        n, c = nof_ref[k], cof_ref[k]
        kc = k_blk(c * FAR_CHUNK, FAR_CHUNK)
        out = []
        for r in heads:
            s = jnp.dot(kc, qpad(n, r), preferred_element_type=jnp.float32)
            buf_ref[r] = s
            mc = jnp.full((1, MOBA_BLOCK), -jnp.inf, jnp.float32)
            for i in range(FAR_CHUNK):
                bm = jnp.max(s[i * MOBA_BLOCK:(i + 1) * MOBA_BLOCK], axis=0, keepdims=True)
                mc = jnp.maximum(mc, jnp.where(on_row(n, r, c * FAR_CHUNK + i), bm + c31[r], -jnp.inf))
            out.append(mc)
        return out

    def far_softmax(k, buf_ref, mcs):
        n, c = nof_ref[k], cof_ref[k]
        for r in heads:
            m = m_ref[n, r]
            m_new = jnp.maximum(m, mcs[r])
            alpha = jnp.exp2(m - m_new)
            shift = m_new - c31[r]
            s = buf_ref[r]
            ps = []
            for i in range(FAR_CHUNK):
                on = on_row(n, r, c * FAR_CHUNK + i)
                p = jnp.exp2(s[i * MOBA_BLOCK:(i + 1) * MOBA_BLOCK] - jnp.where(on, shift, -NEG))
                ps.append(p.astype(jnp.bfloat16))
            vt = vt_ext([c * FAR_CHUNK + i for i in range(FAR_CHUNK)], r)
            m_ref[n, r] = m_new
            acc_ref[n, r] = alpha * acc_ref[n, r] + jnp.dot(
                vt, jnp.concatenate(ps, axis=0), preferred_element_type=jnp.float32)

    def far_body(it, mc_a):
        for u in range(FAR_UNROLL // 2):
            k = FAR_UNROLL * it + 2 * u
            mc_b = far_logits(k + 1, sb_ref)
            far_softmax(k, sa_ref, mc_a)
            mc_a = far_logits(jnp.minimum(k + 2, nitems - 1), sa_ref)
            far_softmax(k + 1, sb_ref, mc_b)
        return tuple(mc_a)

    lax.fori_loop(0, nitems // FAR_UNROLL, far_body, tuple(far_logits(0, sa_ref)))

    def out_body(it, _):
        for u in range(2):
            n = 2 * it + u
            ot = jnp.concatenate([acc_ref[n, r, :HEAD_DIM, :] / acc_ref[n, r, HEAD_DIM:HEAD_DIM + 1, :]
                                  for r in heads], axis=0)
            o_ref[0, pl.ds(pl.multiple_of(n * MOBA_BLOCK, MOBA_BLOCK), MOBA_BLOCK), :] = (
                ot.T.astype(jnp.bfloat16))
        return 0

    lax.fori_loop(0, nblk // 2, out_body, 0)


def _moba(dm, krm, sel, bias_near, c31):
    B, NB = dm.shape[0], dm.shape[1]
    S = NB * MOBA_BLOCK
    pairs = B_HEADS // 2
    vb_blk = DM_VB // LANE
    kb_blk = A_KV_WIDTH // LANE
    n_of, c_of = _far_schedule(NB)
    smem = pl.BlockSpec(memory_space=pltpu.SMEM)
    return pl.pallas_call(
        _moba_kernel,
        grid=(B, pairs),
        in_specs=[
            smem, smem, smem,
            pl.BlockSpec((1, NB, 2 * HEAD_PAD, MOBA_BLOCK), lambda b, p: (b, 0, p, 0)),
            pl.BlockSpec((1, S, LANE), lambda b, p: (b, 0, kb_blk + p)),
            pl.BlockSpec((1, NB, LANE, MOBA_BLOCK), lambda b, p: (b, 0, vb_blk + p, 0)),
            pl.BlockSpec((1, 2, NB, NB + SEL_PAD_ROWS, MOBA_BLOCK), lambda b, p: (b, p, 0, 0, 0)),
            pl.BlockSpec((2, 2 * MOBA_BLOCK, MOBA_BLOCK), lambda b, p: (p, 0, 0)),
        ],
        out_specs=pl.BlockSpec((1, S, LANE), lambda b, p: (b, 0, p)),
        out_shape=jax.ShapeDtypeStruct((B, S, B_WIDTH), jnp.bfloat16),
        scratch_shapes=[
            pltpu.VMEM((2, FAR_CHUNK * MOBA_BLOCK, MOBA_BLOCK), jnp.float32),
            pltpu.VMEM((2, FAR_CHUNK * MOBA_BLOCK, MOBA_BLOCK), jnp.float32),
            pltpu.VMEM((NB, 2, 1, MOBA_BLOCK), jnp.float32),
            pltpu.VMEM((NB, 2, HEAD_DIM + ONES_ROWS, MOBA_BLOCK), jnp.float32),
        ],
        compiler_params=pltpu.CompilerParams(
            dimension_semantics=("parallel", "parallel"), vmem_limit_bytes=VMEM_LIMIT),
        name="moba",
    )(jnp.asarray(n_of), jnp.asarray(c_of), c31, dm, krm, dm, sel, bias_near)


def _final_kernel(x_ref, qt_ref, vcur_ref, vprev_ref, kcur_ref, kprev_ref, bias_ref, sink_ref,
                  ob_ref, wzg_ref, bg_ref, woa_ref, wob_ref, wout_ref, gamma_ref, beta_ref, y_ref,
                  oa_ref, s_ref, *, tiles_per_row, n_tiles):
    s = pl.program_id(0)

    @pl.when(s == 0)
    def _():
        oa_ref[...] = jnp.zeros(oa_ref.shape, jnp.float32)

    t_next = jnp.minimum(s, n_tiles - 1) % tiles_per_row
    _swa_logits(t_next, qt_ref, kcur_ref, kprev_ref, bias_ref, s_ref)
    oa = oa_ref[...]
    vcat = jnp.concatenate([vprev_ref[0, 0][:, WINDOW:], vcur_ref[0, 0]], axis=1)
    units = [[None] * A_KV_HEADS for _ in range(SWA_TOKENS // WINDOW)]
    x = x_ref[0]
    xb = x.astype(jnp.bfloat16)
    zw = A_WIDTH + B_WIDTH
    z = jnp.dot(xb, wzg_ref[:, :zw], preferred_element_type=jnp.float32)
    units[0][0] = _swa_unit(0, 0, s_ref, vcat, sink_ref)
    ga = jnp.dot(xb, wzg_ref[:, zw:zw + D_MODEL],
                 preferred_element_type=jnp.float32) + bg_ref[:, :D_MODEL]
    units[0][1] = _swa_unit(0, 1, s_ref, vcat, sink_ref)
    gb = jnp.dot(xb, wzg_ref[:, zw + D_MODEL:],
                 preferred_element_type=jnp.float32) + bg_ref[:, D_MODEL:]
    units[1][0] = _swa_unit(1, 0, s_ref, vcat, sink_ref)
    za, zb = z[:, :A_WIDTH], z[:, A_WIDTH:]
    ua = oa * (za * jax.nn.sigmoid(za))
    ub = ob_ref[0].astype(jnp.float32) * (zb * jax.nn.sigmoid(zb))
    ya = jnp.dot(ua.astype(jnp.bfloat16), woa_ref[...], preferred_element_type=jnp.float32)
    yb = jnp.dot(ub.astype(jnp.bfloat16), wob_ref[...], preferred_element_type=jnp.float32)
    units[1][1] = _swa_unit(1, 1, s_ref, vcat, sink_ref)
    merged = jax.nn.sigmoid(ga) * ya + jax.nn.sigmoid(gb) * yb
    out = jnp.dot(merged.astype(jnp.bfloat16), wout_ref[...], preferred_element_type=jnp.float32)
    oa_ref[...] = _swa_assemble(units)
    r = DN_ALPHA * x + out
    mu = jnp.mean(r, axis=-1, keepdims=True)
    rc = r - mu
    var = jnp.mean(rc * rc, axis=-1, keepdims=True)
    y_ref[0] = rc * lax.rsqrt(var + LN_EPS) * gamma_ref[...] + beta_ref[...]


def _final(x, dm, krm, bias_a, sink_a, ob, w_zg, b_gate, w_oa, w_ob, w_out, gamma, beta):
    B, S, D = x.shape
    T = SWA_TOKENS
    nt = S // T
    n_tiles = B * nt
    qa_blk = DM_QA // A_WIDTH
    va_blk = DM_VA // A_KV_WIDTH

    def tile(s, lag):
        i = jnp.clip(s - lag, 0, n_tiles - 1)
        return i // nt, i % nt

    def swa_spec(shape, idx):
        return pl.BlockSpec(shape, lambda s: idx(*tile(s, 0)))

    def fin_spec(width, lag):
        return pl.BlockSpec((1, T, width), lambda s: tile(s, lag) + (0,))

    full = lambda a: pl.BlockSpec(a.shape, lambda s: (0,) * a.ndim)
    return pl.pallas_call(
        functools.partial(_final_kernel, tiles_per_row=nt, n_tiles=n_tiles),
        grid=(n_tiles + 1,),
        in_specs=[
            fin_spec(D, 1),
            swa_spec((1, 1, A_WIDTH, T), lambda b, t: (b, t, qa_blk, 0)),
            swa_spec((1, 1, A_KV_WIDTH, T), lambda b, t: (b, t, va_blk, 0)),
            swa_spec((1, 1, A_KV_WIDTH, T), lambda b, t: (b, jnp.maximum(t - 1, 0), va_blk, 0)),
            swa_spec((1, T, A_KV_WIDTH), lambda b, t: (b, t, 0)),
            swa_spec((1, WINDOW, A_KV_WIDTH), lambda b, t: (b, jnp.maximum(2 * t - 1, 0), 0)),
            full(bias_a), full(sink_a),
            fin_spec(B_WIDTH, 1),
            full(w_zg), full(b_gate), full(w_oa), full(w_ob), full(w_out), full(gamma), full(beta),
        ],
        out_specs=fin_spec(D, 1),
        out_shape=jax.ShapeDtypeStruct((B, S, D), jnp.float32),
        scratch_shapes=[
            pltpu.VMEM((T, A_WIDTH), jnp.float32),
            pltpu.VMEM((T // WINDOW, A_KV_HEADS, 2 * WINDOW, A_GROUP * WINDOW), jnp.float32),
        ],
        compiler_params=pltpu.CompilerParams(
            dimension_semantics=("arbitrary",), vmem_limit_bytes=VMEM_LIMIT),
        name="final",
    )(x, dm, dm, dm, krm, krm, bias_a, sink_a, ob, w_zg, b_gate, w_oa, w_ob, w_out, gamma, beta)


def _toeplitz(v, rows, col0, ncols):
    H, L = v.shape
    assert col0 >= rows - 1 and col0 + ncols <= L - 1
    t = jnp.tile(v, (1, rows))[:, :rows * (L - 1)].reshape(H, rows, L - 1)
    return t[:, :, col0:col0 + ncols]


def _bias_tables(rel_table, sinks):
    table_a = rel_table[:, :A_HEADS].astype(jnp.float32) * LOG2E
    table_b = rel_table[:, A_HEADS:].astype(jnp.float32) * LOG2E
    sinks = sinks.astype(jnp.float32) * LOG2E
    d_a = np.arange(-2 * WINDOW, 3 * WINDOW)
    ok_a = (d_a >= 0) & (d_a < WINDOW)
    v_a = jnp.where(ok_a[:, None], table_a[_t5_bucket_np(d_a)], NEG).T
    ba = _toeplitz(v_a, 2 * WINDOW, 3 * WINDOW, WINDOW)
    ba = ba.reshape(A_KV_HEADS, A_GROUP, 2 * WINDOW, WINDOW)
    ba = ba.transpose(0, 2, 1, 3).reshape(A_KV_HEADS, 2 * WINDOW, A_GROUP * WINDOW)
    sink = jnp.broadcast_to(sinks.reshape(A_KV_HEADS, 1, A_GROUP, 1),
                            (A_KV_HEADS, 1, A_GROUP, WINDOW)).reshape(A_KV_HEADS, 1, A_GROUP * WINDOW)
    d_b = np.arange(-MOBA_BLOCK, 3 * MOBA_BLOCK)
    v_b = jnp.where((d_b >= 0)[:, None], table_b[_t5_bucket_np(d_b)], NEG).T
    b_own = _toeplitz(v_b, MOBA_BLOCK, MOBA_BLOCK, MOBA_BLOCK)
    b_prev = _toeplitz(v_b, MOBA_BLOCK, 2 * MOBA_BLOCK, MOBA_BLOCK)
    b_near = jnp.concatenate([b_prev, b_own], axis=1)
    c31 = table_b[N_BUCKETS - 1]
    return ba, sink, b_near, c31


def kernel(x, w_in, b_gate, sinks, rel_table, w_out_a, w_out_b, w_out, ln_gamma, ln_beta):
    B, S, D = x.shape
    assert (D, w_in.shape[0]) == (D_MODEL, DEPTH) and S % PROJ_TOKENS == 0
    w = w_in[0]
    sizes = (A_WIDTH, A_KV_WIDTH, A_KV_WIDTH, A_WIDTH, B_WIDTH, B_WIDTH, B_WIDTH, B_WIDTH,
             D_MODEL, D_MODEL)
    offs = np.concatenate([[0], np.cumsum(sizes)])
    w_qa, w_ka, w_va, w_za, w_qb, w_kb, w_vb, w_zb, w_ga, w_gb = [
        w[:, offs[i]:offs[i + 1]] for i in range(len(sizes))]
    bf = jnp.bfloat16
    q_scale = ATTN_SCALE * LOG2E
    w_rm = jnp.concatenate([w_ka, w_kb], axis=1).astype(bf)
    w_dm_t = jnp.concatenate([w_qb * q_scale, w_qa * q_scale, w_vb, w_va], axis=1).T.astype(bf)
    w_zg = jnp.concatenate([w_za, w_zb, w_ga, w_gb], axis=1).astype(bf)
    bias_a, sink_a, b_near, c31 = _bias_tables(rel_table, sinks[0])

    krm, dm, sel = _proj(x, w_rm, w_dm_t)
    ob = _moba(dm, krm, sel, b_near, c31)
    return _final(x, dm, krm, bias_a, sink_a, ob, w_zg, b_gate[0][None, :],
                  w_out_a[0].astype(bf), w_out_b[0].astype(bf), w_out[0].astype(bf),
                  ln_gamma[0][None, :], ln_beta[0][None, :])
```

```python
import functools
import math

import numpy as np
import jax
import jax.numpy as jnp
from jax import lax
from jax.experimental import pallas as pl
from jax.experimental.pallas import tpu as pltpu

D_MODEL = 1024
HEAD_DIM = 64
A_HEADS = 8
A_KV_HEADS = 2
A_GROUP = A_HEADS // A_KV_HEADS
A_WIDTH = A_HEADS * HEAD_DIM
A_KV_WIDTH = A_KV_HEADS * HEAD_DIM
WINDOW = 128
B_HEADS = 8
B_WIDTH = B_HEADS * HEAD_DIM
MOBA_BLOCK = 256
MOBA_TOPK = 3
N_BUCKETS = 32
MAX_DISTANCE = 128
DEPTH = 1
DN_ALPHA = (2.0 * DEPTH) ** 0.25
LN_EPS = 1e-5
NEG = -1e30
ATTN_SCALE = HEAD_DIM ** -0.5
LOG2E = math.log2(math.e)

LANE = 128
HEAD_PAD = 2 * HEAD_DIM
W_QB, W_QA, W_VB, W_VA = 0, B_WIDTH, B_WIDTH + A_WIDTH, 2 * B_WIDTH + A_WIDTH
W_ROWS = W_VA + A_KV_WIDTH
DM_QB, DM_QA = 0, B_HEADS * HEAD_PAD
DM_VB = DM_QA + A_WIDTH
DM_VA = DM_VB + B_WIDTH
DM_ROWS = DM_VA + A_KV_WIDTH
RM_COLS = A_KV_WIDTH + B_WIDTH
PROJ_TOKENS = 512
SWA_TOKENS = 256
FAR_CHUNK = 2
FAR_UNROLL = 48
NEAR_UNROLL = 8
ONES_ROWS = 16
SEL_PAD_ROWS = 8
VMEM_LIMIT = 56 * 1024 * 1024


def _t5_bucket_np(dist):
    max_exact = N_BUCKETS // 2
    n = np.maximum(dist, 0)
    nf = np.maximum(n, 1).astype(np.float32)
    large = max_exact + (np.log(nf / max_exact) / math.log(MAX_DISTANCE / max_exact)
                         * (N_BUCKETS - max_exact)).astype(np.int32)
    large = np.minimum(large, N_BUCKETS - 1)
    return np.where(n < max_exact, n, large).astype(np.int32)


def _select_rows(gate, n):
    nblk = gate.shape[0]
    blk = lax.broadcasted_iota(jnp.int32, gate.shape, 0)
    work = jnp.where(blk < n, gate, -jnp.inf)
    sel = jnp.zeros(gate.shape, jnp.float32)
    for _ in range(MOBA_TOPK):
        mx = jnp.max(work, axis=0, keepdims=True)
        idx = jnp.min(jnp.where(work == mx, blk, nblk), axis=0, keepdims=True)
        idx = jnp.where(mx > -jnp.inf, idx, nblk)
        pick = blk == idx
        sel = jnp.where(pick, 1.0, sel)
        work = jnp.where(pick, -jnp.inf, work)
    on_prev = jnp.max(jnp.where(blk == n - 1, sel, 0.0), axis=0, keepdims=True)
    far = jnp.where(blk < n - 1, sel, 0.0)
    pad_row = lax.broadcasted_iota(jnp.int32, (SEL_PAD_ROWS, gate.shape[1]), 0)
    return far, jnp.where(pad_row == 0, on_prev, 0.0)


def _proj_kernel(x_ref, wrm_ref, wdm_ref, krm_ref, dm_ref, sel_ref, kmean_ref):
    t = pl.program_id(1)
    blocks = PROJ_TOKENS // MOBA_BLOCK
    nblk = kmean_ref.shape[0]

    @pl.when(t == 0)
    def _():
        kmean_ref[...] = jnp.zeros(kmean_ref.shape, jnp.float32)

    xb = x_ref[0].astype(jnp.bfloat16)
    rm = jnp.dot(xb, wrm_ref[...], preferred_element_type=jnp.float32)
    krm_ref[0] = rm.astype(jnp.bfloat16)
    for c in range(blocks):
        kb = rm[c * MOBA_BLOCK:(c + 1) * MOBA_BLOCK, A_KV_WIDTH:]
        kmean_ref[pl.ds(t * blocks + c, 1), :] = (
            jnp.sum(kb, axis=0, keepdims=True) * (1.0 / MOBA_BLOCK))
    dm = lax.dot_general(wdm_ref[...], xb, (((1,), (1,)), ((), ())),
                         preferred_element_type=jnp.float32)
    dmb = dm.astype(jnp.bfloat16)
    zeros = jnp.zeros((HEAD_DIM, MOBA_BLOCK), jnp.bfloat16)
    km = kmean_ref[...].astype(jnp.bfloat16)
    for c in range(blocks):
        piece = dmb[:, c * MOBA_BLOCK:(c + 1) * MOBA_BLOCK]
        rows = []
        for h in range(B_HEADS):
            q = piece[W_QB + h * HEAD_DIM:W_QB + (h + 1) * HEAD_DIM]
            qpad = jnp.concatenate([q, zeros] if h % 2 == 0 else [zeros, q], axis=0)
            rows.append(qpad)
            pair = h // 2
            gate = jnp.dot(km[:, pair * HEAD_PAD:(pair + 1) * HEAD_PAD], qpad,
                           preferred_element_type=jnp.float32)
            far, near = _select_rows(gate, t * blocks + c)
            sel_ref[0, h, c, :nblk, :] = far
            sel_ref[0, h, c, nblk:, :] = near
        dm_ref[0, c] = jnp.concatenate(rows + [piece[W_QA:]], axis=0)


def _proj(x, w_rm, w_dm_t):
    B, S, D = x.shape
    nt = S // PROJ_TOKENS
    cpt = PROJ_TOKENS // MOBA_BLOCK
    NB = S // MOBA_BLOCK
    return pl.pallas_call(
        _proj_kernel,
        grid=(B, nt),
        in_specs=[
            pl.BlockSpec((1, PROJ_TOKENS, D), lambda b, t: (b, t, 0)),
            pl.BlockSpec((D, RM_COLS), lambda b, t: (0, 0)),
            pl.BlockSpec((W_ROWS, D), lambda b, t: (0, 0)),
        ],
        out_specs=[
            pl.BlockSpec((1, PROJ_TOKENS, RM_COLS), lambda b, t: (b, t, 0)),
            pl.BlockSpec((1, cpt, DM_ROWS, MOBA_BLOCK), lambda b, t: (b, t, 0, 0)),
            pl.BlockSpec((1, B_HEADS, cpt, NB + SEL_PAD_ROWS, MOBA_BLOCK),
                         lambda b, t: (b, 0, t, 0, 0)),
        ],
        out_shape=[
            jax.ShapeDtypeStruct((B, S, RM_COLS), jnp.bfloat16),
            jax.ShapeDtypeStruct((B, NB, DM_ROWS, MOBA_BLOCK), jnp.bfloat16),
            jax.ShapeDtypeStruct((B, B_HEADS, NB, NB + SEL_PAD_ROWS, MOBA_BLOCK), jnp.float32),
        ],
        scratch_shapes=[pltpu.VMEM((NB, B_WIDTH), jnp.float32)],
        compiler_params=pltpu.CompilerParams(
            dimension_semantics=("parallel", "arbitrary"), vmem_limit_bytes=VMEM_LIMIT),
        name="proj",
    )(x, w_rm, w_dm_t)


def _swa_logits(t, qt_ref, kcur_ref, kprev_ref, bias_ref, s_ref):
    qt = qt_ref[0, 0]
    kcat = jnp.concatenate([kprev_ref[0], kcur_ref[0]], axis=0)
    kidx = lax.broadcasted_iota(jnp.int32, (2 * WINDOW, A_GROUP * WINDOW), 0)
    zeros_q = jnp.zeros((HEAD_DIM, A_GROUP * WINDOW), jnp.bfloat16)
    for w in range(SWA_TOKENS // WINDOW):
        k2 = kcat[w * WINDOW:(w + 2) * WINDOW, :]
        for g in range(A_KV_HEADS):
            q4 = jnp.concatenate(
                [qt[(g * A_GROUP + i) * HEAD_DIM:(g * A_GROUP + i + 1) * HEAD_DIM,
                    w * WINDOW:(w + 1) * WINDOW] for i in range(A_GROUP)], axis=1)
            qpad = jnp.concatenate([q4, zeros_q] if g == 0 else [zeros_q, q4], axis=0)
            s = jnp.dot(k2, qpad, preferred_element_type=jnp.float32) + bias_ref[g]
            if w == 0:
                s = jnp.where((t == 0) & (kidx < WINDOW), NEG, s)
            s_ref[w, g] = s


def _swa_unit(w, g, s_ref, vcat, sink_ref):
    s = s_ref[w, g]
    sink = sink_ref[g]
    m = jnp.maximum(jnp.max(s, axis=0, keepdims=True), sink)
    p = jnp.exp2(s - m)
    l = jnp.sum(p, axis=0, keepdims=True) + jnp.exp2(sink - m)
    v2 = vcat[g * HEAD_DIM:(g + 1) * HEAD_DIM, w * WINDOW:(w + 2) * WINDOW]
    o = jnp.dot(v2, p.astype(jnp.bfloat16), preferred_element_type=jnp.float32) / l
    return [o[:, i * WINDOW:(i + 1) * WINDOW] for i in range(A_GROUP)]


def _swa_assemble(units):
    return jnp.concatenate(
        [jnp.concatenate([o for g in range(A_KV_HEADS) for o in units[w][g]], axis=0).T
         for w in range(SWA_TOKENS // WINDOW)], axis=0)


def _far_schedule(nblk):
    items = [(n, c) for n in range(nblk) for c in range((max(n - 1, 0) + FAR_CHUNK - 1) // FAR_CHUNK)]
    assert len(items) % FAR_UNROLL == 0
    return (np.array([i[0] for i in items], np.int32), np.array([i[1] for i in items], np.int32))


def _moba_kernel(nof_ref, cof_ref, c31_ref, q_ref, k_ref, vt_ref, sel_ref, bnear_ref, o_ref,
                 sa_ref, sb_ref, m_ref, acc_ref):
    pr = pl.program_id(1)
    nblk = q_ref.shape[1]
    nitems = nof_ref.shape[0]
    heads = (0, 1)
    c31 = [c31_ref[2 * pr + r] for r in heads]
    ones_rows = jnp.ones((ONES_ROWS, MOBA_BLOCK), jnp.bfloat16)

    def qpad(n, r):
        return q_ref[0, n, r * HEAD_PAD:(r + 1) * HEAD_PAD, :]

    def k_blk(j, nb=1):
        return k_ref[0, pl.ds(pl.multiple_of(j * MOBA_BLOCK, MOBA_BLOCK), nb * MOBA_BLOCK), :]

    def vt_ext(blocks, r):
        return jnp.concatenate(
            [jnp.concatenate([vt_ref[0, j, r * HEAD_DIM:(r + 1) * HEAD_DIM, :], ones_rows], axis=0)
             for j in blocks], axis=1)

    def on_row(n, r, row):
        return sel_ref[0, r, n, pl.ds(row, 1), :] > 0.5

    def near_logits(n, buf_ref):
        jp = jnp.maximum(n - 1, 0)
        kc = jnp.concatenate([k_blk(jp), k_blk(n)], axis=0)
        out = []
        for r in heads:
            s = jnp.dot(kc, qpad(n, r), preferred_element_type=jnp.float32) + bnear_ref[r]
            buf_ref[r] = s
            out.append(jnp.maximum(
                jnp.max(s[MOBA_BLOCK:], axis=0, keepdims=True),
                jnp.where(on_row(n, r, nblk),
                          jnp.max(s[:MOBA_BLOCK], axis=0, keepdims=True), -jnp.inf)))
        return out

    def near_softmax(n, buf_ref, ms):
        jp = jnp.maximum(n - 1, 0)
        for r in heads:
            s = buf_ref[r]
            p_prev = jnp.exp2(s[:MOBA_BLOCK] - jnp.where(on_row(n, r, nblk), ms[r], -NEG))
            p_own = jnp.exp2(s[MOBA_BLOCK:] - ms[r])
            p = jnp.concatenate([p_prev, p_own], axis=0).astype(jnp.bfloat16)
            m_ref[n, r] = ms[r]
            acc_ref[n, r] = jnp.dot(vt_ext((jp, n), r), p, preferred_element_type=jnp.float32)

    def near_body(it, m_a):
        for u in range(NEAR_UNROLL // 2):
            n = NEAR_UNROLL * it + 2 * u
            m_b = near_logits(n + 1, sb_ref)
            near_softmax(n, sa_ref, m_a)
            m_a = near_logits(jnp.minimum(n + 2, nblk - 1), sa_ref)
            near_softmax(n + 1, sb_ref, m_b)
        return tuple(m_a)

    lax.fori_loop(0, nblk // NEAR_UNROLL, near_body, tuple(near_logits(0, sa_ref)))

    def far_logits(k, buf_ref):
        n, c = nof_ref[k], cof_ref[k]
        kc = k_blk(c * FAR_CHUNK, FAR_CHUNK)
        out = []
        for r in heads:
            s = jnp.dot(kc, qpad(n, r), preferred_element_type=jnp.float32)
            buf_ref[r] = s
            mc = jnp.full((1, MOBA_BLOCK), -jnp.inf, jnp.float32)
            for i in range(FAR_CHUNK):
                bm = jnp.max(s[i * MOBA_BLOCK:(i + 1) * MOBA_BLOCK], axis=0, keepdims=True)
                mc = jnp.maximum(mc, jnp.where(on_row(n, r, c * FAR_CHUNK + i), bm + c31[r], -jnp.inf))
            out.append(mc)
        return out

    def far_softmax(k, buf_ref, mcs):
        n, c = nof_ref[k], cof_ref[k]
        for r in heads:
            m = m_ref[n, r]
            m_new = jnp.maximum(m, mcs[r])
            alpha = jnp.exp2(m - m_new)
            shift = m_new - c31[r]
            s = buf_ref[r]
            ps = []
            for i in range(FAR_CHUNK):
                on = on_row(n, r, c * FAR_CHUNK + i)
                p = jnp.exp2(s[i * MOBA_BLOCK:(i + 1) * MOBA_BLOCK] - jnp.where(on, shift, -NEG))
                ps.append(p.astype(jnp.bfloat16))
            vt = vt_ext([c * FAR_CHUNK + i for i in range(FAR_CHUNK)], r)
            m_ref[n, r] = m_new
            acc_ref[n, r] = alpha * acc_ref[n, r] + jnp.dot(
                vt, jnp.concatenate(ps, axis=0), preferred_element_type=jnp.float32)

    def far_body(it, mc_a):
        for u in range(FAR_UNROLL // 2):
            k = FAR_UNROLL * it + 2 * u
            mc_b = far_logits(k + 1, sb_ref)
            far_softmax(k, sa_ref, mc_a)
            mc_a = far_logits(jnp.minimum(k + 2, nitems - 1), sa_ref)
            far_softmax(k + 1, sb_ref, mc_b)
        return tuple(mc_a)

    lax.fori_loop(0, nitems // FAR_UNROLL, far_body, tuple(far_logits(0, sa_ref)))

    def out_body(it, _):
        for u in range(2):
            n = 2 * it + u
            ot = jnp.concatenate([acc_ref[n, r, :HEAD_DIM, :] / acc_ref[n, r, HEAD_DIM:HEAD_DIM + 1, :]
                                  for r in heads], axis=0)
            o_ref[0, pl.ds(pl.multiple_of(n * MOBA_BLOCK, MOBA_BLOCK), MOBA_BLOCK), :] = (
                ot.T.astype(jnp.bfloat16))
        return 0

    lax.fori_loop(0, nblk // 2, out_body, 0)


def _moba(dm, krm, sel, bias_near, c31):
    B, NB = dm.shape[0], dm.shape[1]
    S = NB * MOBA_BLOCK
    pairs = B_HEADS // 2
    vb_blk = DM_VB // LANE
    kb_blk = A_KV_WIDTH // LANE
    n_of, c_of = _far_schedule(NB)
    smem = pl.BlockSpec(memory_space=pltpu.SMEM)
    return pl.pallas_call(
        _moba_kernel,
        grid=(B, pairs),
        in_specs=[
            smem, smem, smem,
            pl.BlockSpec((1, NB, 2 * HEAD_PAD, MOBA_BLOCK), lambda b, p: (b, 0, p, 0)),
            pl.BlockSpec((1, S, LANE), lambda b, p: (b, 0, kb_blk + p)),
            pl.BlockSpec((1, NB, LANE, MOBA_BLOCK), lambda b, p: (b, 0, vb_blk + p, 0)),
            pl.BlockSpec((1, 2, NB, NB + SEL_PAD_ROWS, MOBA_BLOCK), lambda b, p: (b, p, 0, 0, 0)),
            pl.BlockSpec((2, 2 * MOBA_BLOCK, MOBA_BLOCK), lambda b, p: (p, 0, 0)),
        ],
        out_specs=pl.BlockSpec((1, S, LANE), lambda b, p: (b, 0, p)),
        out_shape=jax.ShapeDtypeStruct((B, S, B_WIDTH), jnp.bfloat16),
        scratch_shapes=[
            pltpu.VMEM((2, FAR_CHUNK * MOBA_BLOCK, MOBA_BLOCK), jnp.float32),
            pltpu.VMEM((2, FAR_CHUNK * MOBA_BLOCK, MOBA_BLOCK), jnp.float32),
            pltpu.VMEM((NB, 2, 1, MOBA_BLOCK), jnp.float32),
            pltpu.VMEM((NB, 2, HEAD_DIM + ONES_ROWS, MOBA_BLOCK), jnp.float32),
        ],
        compiler_params=pltpu.CompilerParams(
            dimension_semantics=("parallel", "parallel"), vmem_limit_bytes=VMEM_LIMIT),
        name="moba",
    )(jnp.asarray(n_of), jnp.asarray(c_of), c31, dm, krm, dm, sel, bias_near)


def _final_kernel(x_ref, qt_ref, vcur_ref, vprev_ref, kcur_ref, kprev_ref, bias_ref, sink_ref,
                  ob_ref, wzg_ref, bg_ref, woa_ref, wob_ref, wout_ref, gamma_ref, beta_ref, y_ref,
                  oa_ref, s_ref, *, tiles_per_row, n_tiles):
    s = pl.program_id(0)

    @pl.when(s == 0)
    def _():
        oa_ref[...] = jnp.zeros(oa_ref.shape, jnp.float32)

    t_next = jnp.minimum(s, n_tiles - 1) % tiles_per_row
    _swa_logits(t_next, qt_ref, kcur_ref, kprev_ref, bias_ref, s_ref)
    oa = oa_ref[...]
    vcat = jnp.concatenate([vprev_ref[0, 0][:, WINDOW:], vcur_ref[0, 0]], axis=1)
    units = [[None] * A_KV_HEADS for _ in range(SWA_TOKENS // WINDOW)]
    x = x_ref[0]
    xb = x.astype(jnp.bfloat16)
    zw = A_WIDTH + B_WIDTH
    z = jnp.dot(xb, wzg_ref[:, :zw], preferred_element_type=jnp.float32)
    units[0][0] = _swa_unit(0, 0, s_ref, vcat, sink_ref)
    ga = jnp.dot(xb, wzg_ref[:, zw:zw + D_MODEL],
                 preferred_element_type=jnp.float32) + bg_ref[:, :D_MODEL]
    units[0][1] = _swa_unit(0, 1, s_ref, vcat, sink_ref)
    gb = jnp.dot(xb, wzg_ref[:, zw + D_MODEL:],
                 preferred_element_type=jnp.float32) + bg_ref[:, D_MODEL:]
    units[1][0] = _swa_unit(1, 0, s_ref, vcat, sink_ref)
    za, zb = z[:, :A_WIDTH], z[:, A_WIDTH:]
    ua = oa * (za * jax.nn.sigmoid(za))
    ub = ob_ref[0].astype(jnp.float32) * (zb * jax.nn.sigmoid(zb))
    ya = jnp.dot(ua.astype(jnp.bfloat16), woa_ref[...], preferred_element_type=jnp.float32)
    yb = jnp.dot(ub.astype(jnp.bfloat16), wob_ref[...], preferred_element_type=jnp.float32)
    units[1][1] = _swa_unit(1, 1, s_ref, vcat, sink_ref)
    merged = jax.nn.sigmoid(ga) * ya + jax.nn.sigmoid(gb) * yb
    out = jnp.dot(merged.astype(jnp.bfloat16), wout_ref[...], preferred_element_type=jnp.float32)
    oa_ref[...] = _swa_assemble(units)
    r = DN_ALPHA * x + out
    mu = jnp.mean(r, axis=-1, keepdims=True)
    rc = r - mu
    var = jnp.mean(rc * rc, axis=-1, keepdims=True)
    y_ref[0] = rc * lax.rsqrt(var + LN_EPS) * gamma_ref[...] + beta_ref[...]


def _final(x, dm, krm, bias_a, sink_a, ob, w_zg, b_gate, w_oa, w_ob, w_out, gamma, beta):
    B, S, D = x.shape
    T = SWA_TOKENS
    nt = S // T
    n_tiles = B * nt
    qa_blk = DM_QA // A_WIDTH
    va_blk = DM_VA // A_KV_WIDTH

    def tile(s, lag):
        i = jnp.clip(s - lag, 0, n_tiles - 1)
        return i // nt, i % nt

    def swa_spec(shape, idx):
        return pl.BlockSpec(shape, lambda s: idx(*tile(s, 0)))

    def fin_spec(width, lag):
        return pl.BlockSpec((1, T, width), lambda s: tile(s, lag) + (0,))

    full = lambda a: pl.BlockSpec(a.shape, lambda s: (0,) * a.ndim)
    return pl.pallas_call(
        functools.partial(_final_kernel, tiles_per_row=nt, n_tiles=n_tiles),
        grid=(n_tiles + 1,),
        in_specs=[
            fin_spec(D, 1),
            swa_spec((1, 1, A_WIDTH, T), lambda b, t: (b, t, qa_blk, 0)),
            swa_spec((1, 1, A_KV_WIDTH, T), lambda b, t: (b, t, va_blk, 0)),
            swa_spec((1, 1, A_KV_WIDTH, T), lambda b, t: (b, jnp.maximum(t - 1, 0), va_blk, 0)),
            swa_spec((1, T, A_KV_WIDTH), lambda b, t: (b, t, 0)),
            swa_spec((1, WINDOW, A_KV_WIDTH), lambda b, t: (b, jnp.maximum(2 * t - 1, 0), 0)),
            full(bias_a), full(sink_a),
            fin_spec(B_WIDTH, 1),
            full(w_zg), full(b_gate), full(w_oa), full(w_ob), full(w_out), full(gamma), full(beta),
        ],
        out_specs=fin_spec(D, 1),
        out_shape=jax.ShapeDtypeStruct((B, S, D), jnp.float32),
        scratch_shapes=[
            pltpu.VMEM((T, A_WIDTH), jnp.float32),
            pltpu.VMEM((T // WINDOW, A_KV_HEADS, 2 * WINDOW, A_GROUP * WINDOW), jnp.float32),
        ],
        compiler_params=pltpu.CompilerParams(
            dimension_semantics=("arbitrary",), vmem_limit_bytes=VMEM_LIMIT),
        name="final",
    )(x, dm, dm, dm, krm, krm, bias_a, sink_a, ob, w_zg, b_gate, w_oa, w_ob, w_out, gamma, beta)


def _toeplitz(v, rows, col0, ncols):
    H, L = v.shape
    assert col0 >= rows - 1 and col0 + ncols <= L - 1
    t = jnp.tile(v, (1, rows))[:, :rows * (L - 1)].reshape(H, rows, L - 1)
    return t[:, :, col0:col0 + ncols]


def _bias_tables(rel_table, sinks):
    table_a = rel_table[:, :A_HEADS].astype(jnp.float32) * LOG2E
    table_b = rel_table[:, A_HEADS:].astype(jnp.float32) * LOG2E
    sinks = sinks.astype(jnp.float32) * LOG2E
    d_a = np.arange(-2 * WINDOW, 3 * WINDOW)
    ok_a = (d_a >= 0) & (d_a < WINDOW)
    v_a = jnp.where(ok_a[:, None], table_a[_t5_bucket_np(d_a)], NEG).T
    ba = _toeplitz(v_a, 2 * WINDOW, 3 * WINDOW, WINDOW)
    ba = ba.reshape(A_KV_HEADS, A_GROUP, 2 * WINDOW, WINDOW)
    ba = ba.transpose(0, 2, 1, 3).reshape(A_KV_HEADS, 2 * WINDOW, A_GROUP * WINDOW)
    sink = jnp.broadcast_to(sinks.reshape(A_KV_HEADS, 1, A_GROUP, 1),
                            (A_KV_HEADS, 1, A_GROUP, WINDOW)).reshape(A_KV_HEADS, 1, A_GROUP * WINDOW)
    d_b = np.arange(-MOBA_BLOCK, 3 * MOBA_BLOCK)
    v_b = jnp.where((d_b >= 0)[:, None], table_b[_t5_bucket_np(d_b)], NEG).T
    b_own = _toeplitz(v_b, MOBA_BLOCK, MOBA_BLOCK, MOBA_BLOCK)
    b_prev = _toeplitz(v_b, MOBA_BLOCK, 2 * MOBA_BLOCK, MOBA_BLOCK)
    b_near = jnp.concatenate([b_prev, b_own], axis=1)
    c31 = table_b[N_BUCKETS - 1]
    return ba, sink, b_near, c31


def kernel(x, w_in, b_gate, sinks, rel_table, w_out_a, w_out_b, w_out, ln_gamma, ln_beta):
    B, S, D = x.shape
    assert (D, w_in.shape[0]) == (D_MODEL, DEPTH) and S % PROJ_TOKENS == 0
    w = w_in[0]
    sizes = (A_WIDTH, A_KV_WIDTH, A_KV_WIDTH, A_WIDTH, B_WIDTH, B_WIDTH, B_WIDTH, B_WIDTH,
             D_MODEL, D_MODEL)
    offs = np.concatenate([[0], np.cumsum(sizes)])
    w_qa, w_ka, w_va, w_za, w_qb, w_kb, w_vb, w_zb, w_ga, w_gb = [
        w[:, offs[i]:offs[i + 1]] for i in range(len(sizes))]
    bf = jnp.bfloat16
    q_scale = ATTN_SCALE * LOG2E
    w_rm = jnp.concatenate([w_ka, w_kb], axis=1).astype(bf)
    w_dm_t = jnp.concatenate([w_qb * q_scale, w_qa * q_scale, w_vb, w_va], axis=1).T.astype(bf)
    w_zg = jnp.concatenate([w_za, w_zb, w_ga, w_gb], axis=1).astype(bf)
    bias_a, sink_a, b_near, c31 = _bias_tables(rel_table, sinks[0])

    krm, dm, sel = _proj(x, w_rm, w_dm_t)
    ob = _moba(dm, krm, sel, b_near, c31)
    return _final(x, dm, krm, bias_a, sink_a, ob, w_zg, b_gate[0][None, :],
                  w_out_a[0].astype(bf), w_out_b[0].astype(bf), w_out[0].astype(bf),
                  ln_gamma[0][None, :], ln_beta[0][None, :])
```

```python
import functools
import math

import numpy as np
import jax
import jax.numpy as jnp
from jax import lax
from jax.experimental import pallas as pl
from jax.experimental.pallas import tpu as pltpu

D_MODEL = 1024
HEAD_DIM = 64
A_HEADS = 8
A_KV_HEADS = 2
A_GROUP = A_HEADS // A_KV_HEADS
A_WIDTH = A_HEADS * HEAD_DIM
A_KV_WIDTH = A_KV_HEADS * HEAD_DIM
WINDOW = 128
B_HEADS = 8
B_WIDTH = B_HEADS * HEAD_DIM
MOBA_BLOCK = 256
MOBA_TOPK = 3
N_BUCKETS = 32
MAX_DISTANCE = 128
DEPTH = 1
DN_ALPHA = (2.0 * DEPTH) ** 0.25
LN_EPS = 1e-5
NEG = -1e30
ATTN_SCALE = HEAD_DIM ** -0.5
LOG2E = math.log2(math.e)

LANE = 128
HEAD_PAD = 2 * HEAD_DIM
W_QB, W_QA, W_VB, W_VA = 0, B_WIDTH, B_WIDTH + A_WIDTH, 2 * B_WIDTH + A_WIDTH
W_ROWS = W_VA + A_KV_WIDTH
DM_QB, DM_QA = 0, B_HEADS * HEAD_PAD
DM_VB = DM_QA + A_WIDTH
DM_VA = DM_VB + B_WIDTH
DM_ROWS = DM_VA + A_KV_WIDTH
RM_COLS = A_KV_WIDTH + B_WIDTH
PROJ_TOKENS = 512
SWA_TOKENS = 256
FAR_CHUNK = 2
FAR_UNROLL = 80
NEAR_UNROLL = 16
ONES_ROWS = 16
SEL_PAD_ROWS = 8
VMEM_LIMIT = 56 * 1024 * 1024


def _t5_bucket_np(dist):
    max_exact = N_BUCKETS // 2
    n = np.maximum(dist, 0)
    nf = np.maximum(n, 1).astype(np.float32)
    large = max_exact + (np.log(nf / max_exact) / math.log(MAX_DISTANCE / max_exact)
                         * (N_BUCKETS - max_exact)).astype(np.int32)
    large = np.minimum(large, N_BUCKETS - 1)
    return np.where(n < max_exact, n, large).astype(np.int32)


def _select_rows(gate, n):
    nblk = gate.shape[0]
    blk = lax.broadcasted_iota(jnp.int32, gate.shape, 0)
    work = jnp.where(blk < n, gate, -jnp.inf)
    sel = jnp.zeros(gate.shape, jnp.float32)
    for _ in range(MOBA_TOPK):
        mx = jnp.max(work, axis=0, keepdims=True)
        idx = jnp.min(jnp.where(work == mx, blk, nblk), axis=0, keepdims=True)
        idx = jnp.where(mx > -jnp.inf, idx, nblk)
        pick = blk == idx
        sel = jnp.where(pick, 1.0, sel)
        work = jnp.where(pick, -jnp.inf, work)
    on_prev = jnp.max(jnp.where(blk == n - 1, sel, 0.0), axis=0, keepdims=True)
    far = jnp.where(blk < n - 1, sel, 0.0)
    pad_row = lax.broadcasted_iota(jnp.int32, (SEL_PAD_ROWS, gate.shape[1]), 0)
    return far, jnp.where(pad_row == 0, on_prev, 0.0)


def _proj_kernel(x_ref, wrm_ref, wdm_ref, krm_ref, dm_ref, sel_ref, kmean_ref):
    t = pl.program_id(1)
    blocks = PROJ_TOKENS // MOBA_BLOCK
    nblk = kmean_ref.shape[0]

    @pl.when(t == 0)
    def _():
        kmean_ref[...] = jnp.zeros(kmean_ref.shape, jnp.float32)

    xb = x_ref[0].astype(jnp.bfloat16)
    rm = jnp.dot(xb, wrm_ref[...], preferred_element_type=jnp.float32)
    krm_ref[0] = rm.astype(jnp.bfloat16)
    for c in range(blocks):
        kb = rm[c * MOBA_BLOCK:(c + 1) * MOBA_BLOCK, A_KV_WIDTH:]
        kmean_ref[pl.ds(t * blocks + c, 1), :] = (
            jnp.sum(kb, axis=0, keepdims=True) * (1.0 / MOBA_BLOCK))
    dm = lax.dot_general(wdm_ref[...], xb, (((1,), (1,)), ((), ())),
                         preferred_element_type=jnp.float32)
    dmb = dm.astype(jnp.bfloat16)
    zeros = jnp.zeros((HEAD_DIM, MOBA_BLOCK), jnp.bfloat16)
    km = kmean_ref[...].astype(jnp.bfloat16)
    for c in range(blocks):
        piece = dmb[:, c * MOBA_BLOCK:(c + 1) * MOBA_BLOCK]
        rows = []
        for h in range(B_HEADS):
            q = piece[W_QB + h * HEAD_DIM:W_QB + (h + 1) * HEAD_DIM]
            qpad = jnp.concatenate([q, zeros] if h % 2 == 0 else [zeros, q], axis=0)
            rows.append(qpad)
            pair = h // 2
            gate = jnp.dot(km[:, pair * HEAD_PAD:(pair + 1) * HEAD_PAD], qpad,
                           preferred_element_type=jnp.float32)
            far, near = _select_rows(gate, t * blocks + c)
            sel_ref[0, h, c, :nblk, :] = far
            sel_ref[0, h, c, nblk:, :] = near
        dm_ref[0, c] = jnp.concatenate(rows + [piece[W_QA:]], axis=0)


def _proj(x, w_rm, w_dm_t):
    B, S, D = x.shape
    nt = S // PROJ_TOKENS
    cpt = PROJ_TOKENS // MOBA_BLOCK
    NB = S // MOBA_BLOCK
    return pl.pallas_call(
        _proj_kernel,
        grid=(B, nt),
        in_specs=[
            pl.BlockSpec((1, PROJ_TOKENS, D), lambda b, t: (b, t, 0)),
            pl.BlockSpec((D, RM_COLS), lambda b, t: (0, 0)),
            pl.BlockSpec((W_ROWS, D), lambda b, t: (0, 0)),
        ],
        out_specs=[
            pl.BlockSpec((1, PROJ_TOKENS, RM_COLS), lambda b, t: (b, t, 0)),
            pl.BlockSpec((1, cpt, DM_ROWS, MOBA_BLOCK), lambda b, t: (b, t, 0, 0)),
            pl.BlockSpec((1, B_HEADS, cpt, NB + SEL_PAD_ROWS, MOBA_BLOCK),
                         lambda b, t: (b, 0, t, 0, 0)),
        ],
        out_shape=[
            jax.ShapeDtypeStruct((B, S, RM_COLS), jnp.bfloat16),
            jax.ShapeDtypeStruct((B, NB, DM_ROWS, MOBA_BLOCK), jnp.bfloat16),
            jax.ShapeDtypeStruct((B, B_HEADS, NB, NB + SEL_PAD_ROWS, MOBA_BLOCK), jnp.float32),
        ],
        scratch_shapes=[pltpu.VMEM((NB, B_WIDTH), jnp.float32)],
        compiler_params=pltpu.CompilerParams(
            dimension_semantics=("parallel", "arbitrary"), vmem_limit_bytes=VMEM_LIMIT),
        name="proj",
    )(x, w_rm, w_dm_t)


def _swa_logits(t, qt_ref, kcur_ref, kprev_ref, bias_ref, s_ref):
    qt = qt_ref[0, 0]
    kcat = jnp.concatenate([kprev_ref[0], kcur_ref[0]], axis=0)
    kidx = lax.broadcasted_iota(jnp.int32, (2 * WINDOW, A_GROUP * WINDOW), 0)
    zeros_q = jnp.zeros((HEAD_DIM, A_GROUP * WINDOW), jnp.bfloat16)
    for w in range(SWA_TOKENS // WINDOW):
        k2 = kcat[w * WINDOW:(w + 2) * WINDOW, :]
        for g in range(A_KV_HEADS):
            q4 = jnp.concatenate(
                [qt[(g * A_GROUP + i) * HEAD_DIM:(g * A_GROUP + i + 1) * HEAD_DIM,
                    w * WINDOW:(w + 1) * WINDOW] for i in range(A_GROUP)], axis=1)
            qpad = jnp.concatenate([q4, zeros_q] if g == 0 else [zeros_q, q4], axis=0)
            s = jnp.dot(k2, qpad, preferred_element_type=jnp.float32) + bias_ref[g]
            if w == 0:
                s = jnp.where((t == 0) & (kidx < WINDOW), NEG, s)
            s_ref[w, g] = s


def _swa_unit(w, g, s_ref, vcat, sink_ref):
    s = s_ref[w, g]
    sink = sink_ref[g]
    m = jnp.maximum(jnp.max(s, axis=0, keepdims=True), sink)
    p = jnp.exp2(s - m)
    l = jnp.sum(p, axis=0, keepdims=True) + jnp.exp2(sink - m)
    v2 = vcat[g * HEAD_DIM:(g + 1) * HEAD_DIM, w * WINDOW:(w + 2) * WINDOW]
    o = jnp.dot(v2, p.astype(jnp.bfloat16), preferred_element_type=jnp.float32) / l
    return [o[:, i * WINDOW:(i + 1) * WINDOW] for i in range(A_GROUP)]


def _swa_assemble(units):
    return jnp.concatenate(
        [jnp.concatenate([o for g in range(A_KV_HEADS) for o in units[w][g]], axis=0).T
         for w in range(SWA_TOKENS // WINDOW)], axis=0)


def _far_schedule(nblk):
    items = [(n, c) for n in range(nblk) for c in range((max(n - 1, 0) + FAR_CHUNK - 1) // FAR_CHUNK)]
    assert len(items) % FAR_UNROLL == 0
    return (np.array([i[0] for i in items], np.int32), np.array([i[1] for i in items], np.int32))


def _moba_kernel(nof_ref, cof_ref, c31_ref, q_ref, k_ref, vt_ref, sel_ref, bnear_ref, o_ref,
                 sa_ref, sb_ref, m_ref, acc_ref):
    pr = pl.program_id(1)
    nblk = q_ref.shape[1]
    nitems = nof_ref.shape[0]
    heads = (0, 1)
    c31 = [c31_ref[2 * pr + r] for r in heads]
    ones_rows = jnp.ones((ONES_ROWS, MOBA_BLOCK), jnp.bfloat16)

    def qpad(n, r):
        return q_ref[0, n, r * HEAD_PAD:(r + 1) * HEAD_PAD, :]

    def k_blk(j, nb=1):
        return k_ref[0, pl.ds(pl.multiple_of(j * MOBA_BLOCK, MOBA_BLOCK), nb * MOBA_BLOCK), :]

    def vt_ext(blocks, r):
        return jnp.concatenate(
            [jnp.concatenate([vt_ref[0, j, r * HEAD_DIM:(r + 1) * HEAD_DIM, :], ones_rows], axis=0)
             for j in blocks], axis=1)

    def on_row(n, r, row):
        return sel_ref[0, r, n, pl.ds(row, 1), :] > 0.5

    def near_logits(n, buf_ref):
        jp = jnp.maximum(n - 1, 0)
        kc = jnp.concatenate([k_blk(jp), k_blk(n)], axis=0)
        out = []
        for r in heads:
            s = jnp.dot(kc, qpad(n, r), preferred_element_type=jnp.float32) + bnear_ref[r]
            buf_ref[r] = s
            out.append(jnp.maximum(
                jnp.max(s[MOBA_BLOCK:], axis=0, keepdims=True),
                jnp.where(on_row(n, r, nblk),
                          jnp.max(s[:MOBA_BLOCK], axis=0, keepdims=True), -jnp.inf)))
        return out

    def near_softmax(n, buf_ref, ms):
        jp = jnp.maximum(n - 1, 0)
        for r in heads:
            s = buf_ref[r]
            p_prev = jnp.exp2(s[:MOBA_BLOCK] - jnp.where(on_row(n, r, nblk), ms[r], -NEG))
            p_own = jnp.exp2(s[MOBA_BLOCK:] - ms[r])
            p = jnp.concatenate([p_prev, p_own], axis=0).astype(jnp.bfloat16)
            m_ref[n, r] = ms[r]
            acc_ref[n, r] = jnp.dot(vt_ext((jp, n), r), p, preferred_element_type=jnp.float32)

    def near_body(it, m_a):
        for u in range(NEAR_UNROLL // 2):
            n = NEAR_UNROLL * it + 2 * u
            m_b = near_logits(n + 1, sb_ref)
            near_softmax(n, sa_ref, m_a)
            m_a = near_logits(jnp.minimum(n + 2, nblk - 1), sa_ref)
            near_softmax(n + 1, sb_ref, m_b)
        return tuple(m_a)

    lax.fori_loop(0, nblk // NEAR_UNROLL, near_body, tuple(near_logits(0, sa_ref)))

    def far_logits(k, buf_ref):
        n, c = nof_ref[k], cof_ref[k]
        kc = k_blk(c * FAR_CHUNK, FAR_CHUNK)
        out = []
        for r in heads:
            s = jnp.dot(kc, qpad(n, r), preferred_element_type=jnp.float32)
            buf_ref[r] = s
            mc = jnp.full((1, MOBA_BLOCK), -jnp.inf, jnp.float32)
            for i in range(FAR_CHUNK):
                bm = jnp.max(s[i * MOBA_BLOCK:(i + 1) * MOBA_BLOCK], axis=0, keepdims=True)
                mc = jnp.maximum(mc, jnp.where(on_row(n, r, c * FAR_CHUNK + i), bm + c31[r], -jnp.inf))
            out.append(mc)
        return out

    def far_softmax(k, buf_ref, mcs):
        n, c = nof_ref[k], cof_ref[k]
        for r in heads:
            m = m_ref[n, r]
            m_new = jnp.maximum(m, mcs[r])
            alpha = jnp.exp2(m - m_new)
            shift = m_new - c31[r]
            s = buf_ref[r]
            ps = []
            for i in range(FAR_CHUNK):
                on = on_row(n, r, c * FAR_CHUNK + i)
                p = jnp.exp2(s[i * MOBA_BLOCK:(i + 1) * MOBA_BLOCK] - jnp.where(on, shift, -NEG))
                ps.append(p.astype(jnp.bfloat16))
            vt = vt_ext([c * FAR_CHUNK + i for i in range(FAR_CHUNK)], r)
            m_ref[n, r] = m_new
            acc_ref[n, r] = alpha * acc_ref[n, r] + jnp.dot(
                vt, jnp.concatenate(ps, axis=0), preferred_element_type=jnp.float32)

    def far_body(it, mc_a):
        for u in range(FAR_UNROLL // 2):
            k = FAR_UNROLL * it + 2 * u
            mc_b = far_logits(k + 1, sb_ref)
            far_softmax(k, sa_ref, mc_a)
            mc_a = far_logits(jnp.minimum(k + 2, nitems - 1), sa_ref)
            far_softmax(k + 1, sb_ref, mc_b)
        return tuple(mc_a)

    lax.fori_loop(0, nitems // FAR_UNROLL, far_body, tuple(far_logits(0, sa_ref)))

    def out_body(it, _):
        for u in range(2):
            n = 2 * it + u
            ot = jnp.concatenate([acc_ref[n, r, :HEAD_DIM, :] / acc_ref[n, r, HEAD_DIM:HEAD_DIM + 1, :]
                                  for r in heads], axis=0)
            o_ref[0, pl.ds(pl.multiple_of(n * MOBA_BLOCK, MOBA_BLOCK), MOBA_BLOCK), :] = (
                ot.T.astype(jnp.bfloat16))
        return 0

    lax.fori_loop(0, nblk // 2, out_body, 0)


def _moba(dm, krm, sel, bias_near, c31):
    B, NB = dm.shape[0], dm.shape[1]
    S = NB * MOBA_BLOCK
    pairs = B_HEADS // 2
    vb_blk = DM_VB // LANE
    kb_blk = A_KV_WIDTH // LANE
    n_of, c_of = _far_schedule(NB)
    smem = pl.BlockSpec(memory_space=pltpu.SMEM)
    return pl.pallas_call(
        _moba_kernel,
        grid=(B, pairs),
        in_specs=[
            smem, smem, smem,
            pl.BlockSpec((1, NB, 2 * HEAD_PAD, MOBA_BLOCK), lambda b, p: (b, 0, p, 0)),
            pl.BlockSpec((1, S, LANE), lambda b, p: (b, 0, kb_blk + p)),
            pl.BlockSpec((1, NB, LANE, MOBA_BLOCK), lambda b, p: (b, 0, vb_blk + p, 0)),
            pl.BlockSpec((1, 2, NB, NB + SEL_PAD_ROWS, MOBA_BLOCK), lambda b, p: (b, p, 0, 0, 0)),
            pl.BlockSpec((2, 2 * MOBA_BLOCK, MOBA_BLOCK), lambda b, p: (p, 0, 0)),
        ],
        out_specs=pl.BlockSpec((1, S, LANE), lambda b, p: (b, 0, p)),
        out_shape=jax.ShapeDtypeStruct((B, S, B_WIDTH), jnp.bfloat16),
        scratch_shapes=[
            pltpu.VMEM((2, FAR_CHUNK * MOBA_BLOCK, MOBA_BLOCK), jnp.float32),
            pltpu.VMEM((2, FAR_CHUNK * MOBA_BLOCK, MOBA_BLOCK), jnp.float32),
            pltpu.VMEM((NB, 2, 1, MOBA_BLOCK), jnp.float32),
            pltpu.VMEM((NB, 2, HEAD_DIM + ONES_ROWS, MOBA_BLOCK), jnp.float32),
        ],
        compiler_params=pltpu.CompilerParams(
            dimension_semantics=("parallel", "parallel"), vmem_limit_bytes=VMEM_LIMIT),
        name="moba",
    )(jnp.asarray(n_of), jnp.asarray(c_of), c31, dm, krm, dm, sel, bias_near)


def _final_kernel(x_ref, qt_ref, vcur_ref, vprev_ref, kcur_ref, kprev_ref, bias_ref, sink_ref,
                  ob_ref, wzg_ref, bg_ref, woa_ref, wob_ref, wout_ref, gamma_ref, beta_ref, y_ref,
                  oa_ref, s_ref, *, tiles_per_row, n_tiles):
    s = pl.program_id(0)

    @pl.when(s == 0)
    def _():
        oa_ref[...] = jnp.zeros(oa_ref.shape, jnp.float32)

    t_next = jnp.minimum(s, n_tiles - 1) % tiles_per_row
    _swa_logits(t_next, qt_ref, kcur_ref, kprev_ref, bias_ref, s_ref)
    oa = oa_ref[...]
    vcat = jnp.concatenate([vprev_ref[0, 0][:, WINDOW:], vcur_ref[0, 0]], axis=1)
    units = [[None] * A_KV_HEADS for _ in range(SWA_TOKENS // WINDOW)]
    x = x_ref[0]
    xb = x.astype(jnp.bfloat16)
    zw = A_WIDTH + B_WIDTH
    z = jnp.dot(xb, wzg_ref[:, :zw], preferred_element_type=jnp.float32)
    units[0][0] = _swa_unit(0, 0, s_ref, vcat, sink_ref)
    ga = jnp.dot(xb, wzg_ref[:, zw:zw + D_MODEL],
                 preferred_element_type=jnp.float32) + bg_ref[:, :D_MODEL]
    units[0][1] = _swa_unit(0, 1, s_ref, vcat, sink_ref)
    gb = jnp.dot(xb, wzg_ref[:, zw + D_MODEL:],
                 preferred_element_type=jnp.float32) + bg_ref[:, D_MODEL:]
    units[1][0] = _swa_unit(1, 0, s_ref, vcat, sink_ref)
    za, zb = z[:, :A_WIDTH], z[:, A_WIDTH:]
    ua = oa * (za * jax.nn.sigmoid(za))
    ub = ob_ref[0].astype(jnp.float32) * (zb * jax.nn.sigmoid(zb))
    ya = jnp.dot(ua.astype(jnp.bfloat16), woa_ref[...], preferred_element_type=jnp.float32)
    yb = jnp.dot(ub.astype(jnp.bfloat16), wob_ref[...], preferred_element_type=jnp.float32)
    units[1][1] = _swa_unit(1, 1, s_ref, vcat, sink_ref)
    merged = jax.nn.sigmoid(ga) * ya + jax.nn.sigmoid(gb) * yb
    out = jnp.dot(merged.astype(jnp.bfloat16), wout_ref[...], preferred_element_type=jnp.float32)
    oa_ref[...] = _swa_assemble(units)
    r = DN_ALPHA * x + out
    mu = jnp.mean(r, axis=-1, keepdims=True)
    rc = r - mu
    var = jnp.mean(rc * rc, axis=-1, keepdims=True)
    y_ref[0] = rc * lax.rsqrt(var + LN_EPS) * gamma_ref[...] + beta_ref[...]


def _final(x, dm, krm, bias_a, sink_a, ob, w_zg, b_gate, w_oa, w_ob, w_out, gamma, beta):
    B, S, D = x.shape
    T = SWA_TOKENS
    nt = S // T
    n_tiles = B * nt
    qa_blk = DM_QA // A_WIDTH
    va_blk = DM_VA // A_KV_WIDTH

    def tile(s, lag):
        i = jnp.clip(s - lag, 0, n_tiles - 1)
        return i // nt, i % nt

    def swa_spec(shape, idx):
        return pl.BlockSpec(shape, lambda s: idx(*tile(s, 0)))

    def fin_spec(width, lag):
        return pl.BlockSpec((1, T, width), lambda s: tile(s, lag) + (0,))

    full = lambda a: pl.BlockSpec(a.shape, lambda s: (0,) * a.ndim)
    return pl.pallas_call(
        functools.partial(_final_kernel, tiles_per_row=nt, n_tiles=n_tiles),
        grid=(n_tiles + 1,),
        in_specs=[
            fin_spec(D, 1),
            swa_spec((1, 1, A_WIDTH, T), lambda b, t: (b, t, qa_blk, 0)),
            swa_spec((1, 1, A_KV_WIDTH, T), lambda b, t: (b, t, va_blk, 0)),
            swa_spec((1, 1, A_KV_WIDTH, T), lambda b, t: (b, jnp.maximum(t - 1, 0), va_blk, 0)),
            swa_spec((1, T, A_KV_WIDTH), lambda b, t: (b, t, 0)),
            swa_spec((1, WINDOW, A_KV_WIDTH), lambda b, t: (b, jnp.maximum(2 * t - 1, 0), 0)),
            full(bias_a), full(sink_a),
            fin_spec(B_WIDTH, 1),
            full(w_zg), full(b_gate), full(w_oa), full(w_ob), full(w_out), full(gamma), full(beta),
        ],
        out_specs=fin_spec(D, 1),
        out_shape=jax.ShapeDtypeStruct((B, S, D), jnp.float32),
        scratch_shapes=[
            pltpu.VMEM((T, A_WIDTH), jnp.float32),
            pltpu.VMEM((T // WINDOW, A_KV_HEADS, 2 * WINDOW, A_GROUP * WINDOW), jnp.float32),
        ],
        compiler_params=pltpu.CompilerParams(
            dimension_semantics=("arbitrary",), vmem_limit_bytes=VMEM_LIMIT),
        name="final",
    )(x, dm, dm, dm, krm, krm, bias_a, sink_a, ob, w_zg, b_gate, w_oa, w_ob, w_out, gamma, beta)


def _toeplitz(v, rows, col0, ncols):
    H, L = v.shape
    assert col0 >= rows - 1 and col0 + ncols <= L - 1
    t = jnp.tile(v, (1, rows))[:, :rows * (L - 1)].reshape(H, rows, L - 1)
    return t[:, :, col0:col0 + ncols]


def _bias_tables(rel_table, sinks):
    table_a = rel_table[:, :A_HEADS].astype(jnp.float32) * LOG2E
    table_b = rel_table[:, A_HEADS:].astype(jnp.float32) * LOG2E
    sinks = sinks.astype(jnp.float32) * LOG2E
    d_a = np.arange(-2 * WINDOW, 3 * WINDOW)
    ok_a = (d_a >= 0) & (d_a < WINDOW)
    v_a = jnp.where(ok_a[:, None], table_a[_t5_bucket_np(d_a)], NEG).T
    ba = _toeplitz(v_a, 2 * WINDOW, 3 * WINDOW, WINDOW)
    ba = ba.reshape(A_KV_HEADS, A_GROUP, 2 * WINDOW, WINDOW)
    ba = ba.transpose(0, 2, 1, 3).reshape(A_KV_HEADS, 2 * WINDOW, A_GROUP * WINDOW)
    sink = jnp.broadcast_to(sinks.reshape(A_KV_HEADS, 1, A_GROUP, 1),
                            (A_KV_HEADS, 1, A_GROUP, WINDOW)).reshape(A_KV_HEADS, 1, A_GROUP * WINDOW)
    d_b = np.arange(-MOBA_BLOCK, 3 * MOBA_BLOCK)
    v_b = jnp.where((d_b >= 0)[:, None], table_b[_t5_bucket_np(d_b)], NEG).T
    b_own = _toeplitz(v_b, MOBA_BLOCK, MOBA_BLOCK, MOBA_BLOCK)
    b_prev = _toeplitz(v_b, MOBA_BLOCK, 2 * MOBA_BLOCK, MOBA_BLOCK)
    b_near = jnp.concatenate([b_prev, b_own], axis=1)
    c31 = table_b[N_BUCKETS - 1]
    return ba, sink, b_near, c31


def kernel(x, w_in, b_gate, sinks, rel_table, w_out_a, w_out_b, w_out, ln_gamma, ln_beta):
    B, S, D = x.shape
    assert (D, w_in.shape[0]) == (D_MODEL, DEPTH) and S % PROJ_TOKENS == 0
    w = w_in[0]
    sizes = (A_WIDTH, A_KV_WIDTH, A_KV_WIDTH, A_WIDTH, B_WIDTH, B_WIDTH, B_WIDTH, B_WIDTH,
             D_MODEL, D_MODEL)
    offs = np.concatenate([[0], np.cumsum(sizes)])
    w_qa, w_ka, w_va, w_za, w_qb, w_kb, w_vb, w_zb, w_ga, w_gb = [
        w[:, offs[i]:offs[i + 1]] for i in range(len(sizes))]
    bf = jnp.bfloat16
    q_scale = ATTN_SCALE * LOG2E
    w_rm = jnp.concatenate([w_ka, w_kb], axis=1).astype(bf)
    w_dm_t = jnp.concatenate([w_qb * q_scale, w_qa * q_scale, w_vb, w_va], axis=1).T.astype(bf)
    w_zg = jnp.concatenate([w_za, w_zb, w_ga, w_gb], axis=1).astype(bf)
    bias_a, sink_a, b_near, c31 = _bias_tables(rel_table, sinks[0])

    krm, dm, sel = _proj(x, w_rm, w_dm_t)
    ob = _moba(dm, krm, sel, b_near, c31)
    return _final(x, dm, krm, bias_a, sink_a, ob, w_zg, b_gate[0][None, :],
                  w_out_a[0].astype(bf), w_out_b[0].astype(bf), w_out[0].astype(bf),
                  ln_gamma[0][None, :], ln_beta[0][None, :])
```

```python
import functools
import math

import numpy as np
import jax
import jax.numpy as jnp
from jax import lax
from jax.experimental import pallas as pl
from jax.experimental.pallas import tpu as pltpu

D_MODEL = 1024
HEAD_DIM = 64
A_HEADS = 8
A_KV_HEADS = 2
A_GROUP = A_HEADS // A_KV_HEADS
A_WIDTH = A_HEADS * HEAD_DIM
A_KV_WIDTH = A_KV_HEADS * HEAD_DIM
WINDOW = 128
B_HEADS = 8
B_WIDTH = B_HEADS * HEAD_DIM
MOBA_BLOCK = 256
MOBA_TOPK = 3
N_BUCKETS = 32
MAX_DISTANCE = 128
DEPTH = 1
DN_ALPHA = (2.0 * DEPTH) ** 0.25
LN_EPS = 1e-5
NEG = -1e30
ATTN_SCALE = HEAD_DIM ** -0.5
LOG2E = math.log2(math.e)

LANE = 128
HEAD_PAD = 2 * HEAD_DIM
W_QB, W_QA, W_VB, W_VA = 0, B_WIDTH, B_WIDTH + A_WIDTH, 2 * B_WIDTH + A_WIDTH
W_ROWS = W_VA + A_KV_WIDTH
DM_QB, DM_QA = 0, B_HEADS * HEAD_PAD
DM_VB = DM_QA + A_WIDTH
DM_VA = DM_VB + B_WIDTH
DM_ROWS = DM_VA + A_KV_WIDTH
RM_COLS = A_KV_WIDTH + B_WIDTH
PROJ_TOKENS = 512
SWA_TOKENS = 256
FAR_CHUNK = 2
FAR_UNROLL = 80
NEAR_UNROLL = 16
OUT_UNROLL = 16
ONES_ROWS = 16
SEL_PAD_ROWS = 8
VMEM_LIMIT = 56 * 1024 * 1024


def _t5_bucket_np(dist):
    max_exact = N_BUCKETS // 2
    n = np.maximum(dist, 0)
    nf = np.maximum(n, 1).astype(np.float32)
    large = max_exact + (np.log(nf / max_exact) / math.log(MAX_DISTANCE / max_exact)
                         * (N_BUCKETS - max_exact)).astype(np.int32)
    large = np.minimum(large, N_BUCKETS - 1)
    return np.where(n < max_exact, n, large).astype(np.int32)


def _select_rows(gate, n):
    nblk = gate.shape[0]
    blk = lax.broadcasted_iota(jnp.int32, gate.shape, 0)
    work = jnp.where(blk < n, gate, -jnp.inf)
    sel = jnp.zeros(gate.shape, jnp.float32)
    for _ in range(MOBA_TOPK):
        mx = jnp.max(work, axis=0, keepdims=True)
        idx = jnp.min(jnp.where(work == mx, blk, nblk), axis=0, keepdims=True)
        idx = jnp.where(mx > -jnp.inf, idx, nblk)
        pick = blk == idx
        sel = jnp.where(pick, 1.0, sel)
        work = jnp.where(pick, -jnp.inf, work)
    on_prev = jnp.max(jnp.where(blk == n - 1, sel, 0.0), axis=0, keepdims=True)
    far = jnp.where(blk < n - 1, sel, 0.0)
    pad_row = lax.broadcasted_iota(jnp.int32, (SEL_PAD_ROWS, gate.shape[1]), 0)
    return far, jnp.where(pad_row == 0, on_prev, 0.0)


def _proj_kernel(x_ref, wrm_ref, wdm_ref, krm_ref, dm_ref, sel_ref, kmean_ref):
    t = pl.program_id(1)
    blocks = PROJ_TOKENS // MOBA_BLOCK
    nblk = kmean_ref.shape[0]

    @pl.when(t == 0)
    def _():
        kmean_ref[...] = jnp.zeros(kmean_ref.shape, jnp.float32)

    xb = x_ref[0].astype(jnp.bfloat16)
    rm = jnp.dot(xb, wrm_ref[...], preferred_element_type=jnp.float32)
    krm_ref[0] = rm.astype(jnp.bfloat16)
    for c in range(blocks):
        kb = rm[c * MOBA_BLOCK:(c + 1) * MOBA_BLOCK, A_KV_WIDTH:]
        kmean_ref[pl.ds(t * blocks + c, 1), :] = (
            jnp.sum(kb, axis=0, keepdims=True) * (1.0 / MOBA_BLOCK))
    dm = lax.dot_general(wdm_ref[...], xb, (((1,), (1,)), ((), ())),
                         preferred_element_type=jnp.float32)
    dmb = dm.astype(jnp.bfloat16)
    zeros = jnp.zeros((HEAD_DIM, MOBA_BLOCK), jnp.bfloat16)
    km = kmean_ref[...].astype(jnp.bfloat16)
    for c in range(blocks):
        piece = dmb[:, c * MOBA_BLOCK:(c + 1) * MOBA_BLOCK]
        rows = []
        for h in range(B_HEADS):
            q = piece[W_QB + h * HEAD_DIM:W_QB + (h + 1) * HEAD_DIM]
            qpad = jnp.concatenate([q, zeros] if h % 2 == 0 else [zeros, q], axis=0)
            rows.append(qpad)
            pair = h // 2
            gate = jnp.dot(km[:, pair * HEAD_PAD:(pair + 1) * HEAD_PAD], qpad,
                           preferred_element_type=jnp.float32)
            far, near = _select_rows(gate, t * blocks + c)
            sel_ref[0, h, c, :nblk, :] = far
            sel_ref[0, h, c, nblk:, :] = near
        dm_ref[0, c] = jnp.concatenate(rows + [piece[W_QA:]], axis=0)


def _proj(x, w_rm, w_dm_t):
    B, S, D = x.shape
    nt = S // PROJ_TOKENS
    cpt = PROJ_TOKENS // MOBA_BLOCK
    NB = S // MOBA_BLOCK
    return pl.pallas_call(
        _proj_kernel,
        grid=(B, nt),
        in_specs=[
            pl.BlockSpec((1, PROJ_TOKENS, D), lambda b, t: (b, t, 0)),
            pl.BlockSpec((D, RM_COLS), lambda b, t: (0, 0)),
            pl.BlockSpec((W_ROWS, D), lambda b, t: (0, 0)),
        ],
        out_specs=[
            pl.BlockSpec((1, PROJ_TOKENS, RM_COLS), lambda b, t: (b, t, 0)),
            pl.BlockSpec((1, cpt, DM_ROWS, MOBA_BLOCK), lambda b, t: (b, t, 0, 0)),
            pl.BlockSpec((1, B_HEADS, cpt, NB + SEL_PAD_ROWS, MOBA_BLOCK),
                         lambda b, t: (b, 0, t, 0, 0)),
        ],
        out_shape=[
            jax.ShapeDtypeStruct((B, S, RM_COLS), jnp.bfloat16),
            jax.ShapeDtypeStruct((B, NB, DM_ROWS, MOBA_BLOCK), jnp.bfloat16),
            jax.ShapeDtypeStruct((B, B_HEADS, NB, NB + SEL_PAD_ROWS, MOBA_BLOCK), jnp.float32),
        ],
        scratch_shapes=[pltpu.VMEM((NB, B_WIDTH), jnp.float32)],
        compiler_params=pltpu.CompilerParams(
            dimension_semantics=("parallel", "arbitrary"), vmem_limit_bytes=VMEM_LIMIT),
        name="proj",
    )(x, w_rm, w_dm_t)


def _swa_logits(t, qt_ref, kcur_ref, kprev_ref, bias_ref, s_ref):
    qt = qt_ref[0, 0]
    kcat = jnp.concatenate([kprev_ref[0], kcur_ref[0]], axis=0)
    kidx = lax.broadcasted_iota(jnp.int32, (2 * WINDOW, A_GROUP * WINDOW), 0)
    zeros_q = jnp.zeros((HEAD_DIM, A_GROUP * WINDOW), jnp.bfloat16)
    for w in range(SWA_TOKENS // WINDOW):
        k2 = kcat[w * WINDOW:(w + 2) * WINDOW, :]
        for g in range(A_KV_HEADS):
            q4 = jnp.concatenate(
                [qt[(g * A_GROUP + i) * HEAD_DIM:(g * A_GROUP + i + 1) * HEAD_DIM,
                    w * WINDOW:(w + 1) * WINDOW] for i in range(A_GROUP)], axis=1)
            qpad = jnp.concatenate([q4, zeros_q] if g == 0 else [zeros_q, q4], axis=0)
            s = jnp.dot(k2, qpad, preferred_element_type=jnp.float32) + bias_ref[g]
            if w == 0:
                s = jnp.where((t == 0) & (kidx < WINDOW), NEG, s)
            s_ref[w, g] = s


def _swa_unit(w, g, s_ref, vcat, sink_ref):
    s = s_ref[w, g]
    sink = sink_ref[g]
    m = jnp.maximum(jnp.max(s, axis=0, keepdims=True), sink)
    p = jnp.exp2(s - m)
    l = jnp.sum(p, axis=0, keepdims=True) + jnp.exp2(sink - m)
    v2 = vcat[g * HEAD_DIM:(g + 1) * HEAD_DIM, w * WINDOW:(w + 2) * WINDOW]
    o = jnp.dot(v2, p.astype(jnp.bfloat16), preferred_element_type=jnp.float32) / l
    return [o[:, i * WINDOW:(i + 1) * WINDOW] for i in range(A_GROUP)]


def _swa_assemble(units):
    return jnp.concatenate(
        [jnp.concatenate([o for g in range(A_KV_HEADS) for o in units[w][g]], axis=0).T
         for w in range(SWA_TOKENS // WINDOW)], axis=0)


def _far_schedule(nblk):
    items = [(n, c) for n in range(nblk) for c in range((max(n - 1, 0) + FAR_CHUNK - 1) // FAR_CHUNK)]
    assert len(items) % FAR_UNROLL == 0
    return (np.array([i[0] for i in items], np.int32), np.array([i[1] for i in items], np.int32))


def _moba_kernel(nof_ref, cof_ref, c31_ref, q_ref, k_ref, vt_ref, sel_ref, bnear_ref, o_ref,
                 sa_ref, sb_ref, m_ref, acc_ref):
    pr = pl.program_id(1)
    nblk = q_ref.shape[1]
    nitems = nof_ref.shape[0]
    heads = (0, 1)
    c31 = [c31_ref[2 * pr + r] for r in heads]
    ones_rows = jnp.ones((ONES_ROWS, MOBA_BLOCK), jnp.bfloat16)

    def qpad(n, r):
        return q_ref[0, n, r * HEAD_PAD:(r + 1) * HEAD_PAD, :]

    def k_blk(j, nb=1):
        return k_ref[0, pl.ds(pl.multiple_of(j * MOBA_BLOCK, MOBA_BLOCK), nb * MOBA_BLOCK), :]

    def vt_ext(blocks, r):
        return jnp.concatenate(
            [jnp.concatenate([vt_ref[0, j, r * HEAD_DIM:(r + 1) * HEAD_DIM, :], ones_rows], axis=0)
             for j in blocks], axis=1)

    def on_row(n, r, row):
        return sel_ref[0, r, n, pl.ds(row, 1), :] > 0.5

    def near_logits(n, buf_ref):
        jp = jnp.maximum(n - 1, 0)
        kc = jnp.concatenate([k_blk(jp), k_blk(n)], axis=0)
        out = []
        for r in heads:
            s = jnp.dot(kc, qpad(n, r), preferred_element_type=jnp.float32) + bnear_ref[r]
            buf_ref[r] = s
            out.append(jnp.maximum(
                jnp.max(s[MOBA_BLOCK:], axis=0, keepdims=True),
                jnp.where(on_row(n, r, nblk),
                          jnp.max(s[:MOBA_BLOCK], axis=0, keepdims=True), -jnp.inf)))
        return out

    def near_softmax(n, buf_ref, ms):
        jp = jnp.maximum(n - 1, 0)
        for r in heads:
            s = buf_ref[r]
            p_prev = jnp.exp2(s[:MOBA_BLOCK] - jnp.where(on_row(n, r, nblk), ms[r], -NEG))
            p_own = jnp.exp2(s[MOBA_BLOCK:] - ms[r])
            p = jnp.concatenate([p_prev, p_own], axis=0).astype(jnp.bfloat16)
            m_ref[n, r] = ms[r]
            acc_ref[n, r] = jnp.dot(vt_ext((jp, n), r), p, preferred_element_type=jnp.float32)

    def near_body(it, m_a):
        for u in range(NEAR_UNROLL // 2):
            n = NEAR_UNROLL * it + 2 * u
            m_b = near_logits(n + 1, sb_ref)
            near_softmax(n, sa_ref, m_a)
            m_a = near_logits(jnp.minimum(n + 2, nblk - 1), sa_ref)
            near_softmax(n + 1, sb_ref, m_b)
        return tuple(m_a)

    lax.fori_loop(0, nblk // NEAR_UNROLL, near_body, tuple(near_logits(0, sa_ref)))

    def far_logits(k, buf_ref):
        n, c = nof_ref[k], cof_ref[k]
        kc = k_blk(c * FAR_CHUNK, FAR_CHUNK)
        out = []
        for r in heads:
            s = jnp.dot(kc, qpad(n, r), preferred_element_type=jnp.float32)
            buf_ref[r] = s
            mc = jnp.full((1, MOBA_BLOCK), -jnp.inf, jnp.float32)
            for i in range(FAR_CHUNK):
                bm = jnp.max(s[i * MOBA_BLOCK:(i + 1) * MOBA_BLOCK], axis=0, keepdims=True)
                mc = jnp.maximum(mc, jnp.where(on_row(n, r, c * FAR_CHUNK + i), bm + c31[r], -jnp.inf))
            out.append(mc)
        return out

    def far_softmax(k, buf_ref, mcs):
        n, c = nof_ref[k], cof_ref[k]
        for r in heads:
            m = m_ref[n, r]
            m_new = jnp.maximum(m, mcs[r])
            alpha = jnp.exp2(m - m_new)
            shift = m_new - c31[r]
            s = buf_ref[r]
            ps = []
            for i in range(FAR_CHUNK):
                on = on_row(n, r, c * FAR_CHUNK + i)
                p = jnp.exp2(s[i * MOBA_BLOCK:(i + 1) * MOBA_BLOCK] - jnp.where(on, shift, -NEG))
                ps.append(p.astype(jnp.bfloat16))
            vt = vt_ext([c * FAR_CHUNK + i for i in range(FAR_CHUNK)], r)
            m_ref[n, r] = m_new
            acc_ref[n, r] = alpha * acc_ref[n, r] + jnp.dot(
                vt, jnp.concatenate(ps, axis=0), preferred_element_type=jnp.float32)

    def far_body(it, mc_a):
        for u in range(FAR_UNROLL // 2):
            k = FAR_UNROLL * it + 2 * u
            mc_b = far_logits(k + 1, sb_ref)
            far_softmax(k, sa_ref, mc_a)
            mc_a = far_logits(jnp.minimum(k + 2, nitems - 1), sa_ref)
            far_softmax(k + 1, sb_ref, mc_b)
        return tuple(mc_a)

    lax.fori_loop(0, nitems // FAR_UNROLL, far_body, tuple(far_logits(0, sa_ref)))

    def out_body(it, _):
        for u in range(OUT_UNROLL):
            n = OUT_UNROLL * it + u
            ot = jnp.concatenate([acc_ref[n, r, :HEAD_DIM, :] / acc_ref[n, r, HEAD_DIM:HEAD_DIM + 1, :]
                                  for r in heads], axis=0)
            o_ref[0, pl.ds(pl.multiple_of(n * MOBA_BLOCK, MOBA_BLOCK), MOBA_BLOCK), :] = (
                ot.T.astype(jnp.bfloat16))
        return 0

    lax.fori_loop(0, nblk // OUT_UNROLL, out_body, 0)


def _moba(dm, krm, sel, bias_near, c31):
    B, NB = dm.shape[0], dm.shape[1]
    S = NB * MOBA_BLOCK
    pairs = B_HEADS // 2
    vb_blk = DM_VB // LANE
    kb_blk = A_KV_WIDTH // LANE
    n_of, c_of = _far_schedule(NB)
    smem = pl.BlockSpec(memory_space=pltpu.SMEM)
    return pl.pallas_call(
        _moba_kernel,
        grid=(B, pairs),
        in_specs=[
            smem, smem, smem,
            pl.BlockSpec((1, NB, 2 * HEAD_PAD, MOBA_BLOCK), lambda b, p: (b, 0, p, 0)),
            pl.BlockSpec((1, S, LANE), lambda b, p: (b, 0, kb_blk + p)),
            pl.BlockSpec((1, NB, LANE, MOBA_BLOCK), lambda b, p: (b, 0, vb_blk + p, 0)),
            pl.BlockSpec((1, 2, NB, NB + SEL_PAD_ROWS, MOBA_BLOCK), lambda b, p: (b, p, 0, 0, 0)),
            pl.BlockSpec((2, 2 * MOBA_BLOCK, MOBA_BLOCK), lambda b, p: (p, 0, 0)),
        ],
        out_specs=pl.BlockSpec((1, S, LANE), lambda b, p: (b, 0, p)),
        out_shape=jax.ShapeDtypeStruct((B, S, B_WIDTH), jnp.bfloat16),
        scratch_shapes=[
            pltpu.VMEM((2, FAR_CHUNK * MOBA_BLOCK, MOBA_BLOCK), jnp.float32),
            pltpu.VMEM((2, FAR_CHUNK * MOBA_BLOCK, MOBA_BLOCK), jnp.float32),
            pltpu.VMEM((NB, 2, 1, MOBA_BLOCK), jnp.float32),
            pltpu.VMEM((NB, 2, HEAD_DIM + ONES_ROWS, MOBA_BLOCK), jnp.float32),
        ],
        compiler_params=pltpu.CompilerParams(
            dimension_semantics=("parallel", "parallel"), vmem_limit_bytes=VMEM_LIMIT),
        name="moba",
    )(jnp.asarray(n_of), jnp.asarray(c_of), c31, dm, krm, dm, sel, bias_near)


def _final_kernel(x_ref, qt_ref, vcur_ref, vprev_ref, kcur_ref, kprev_ref, bias_ref, sink_ref,
                  ob_ref, wzg_ref, bg_ref, woa_ref, wob_ref, wout_ref, gamma_ref, beta_ref, y_ref,
                  oa_ref, s_ref, *, tiles_per_row, n_tiles):
    s = pl.program_id(0)

    @pl.when(s == 0)
    def _():
        oa_ref[...] = jnp.zeros(oa_ref.shape, jnp.float32)

    t_next = jnp.minimum(s, n_tiles - 1) % tiles_per_row
    _swa_logits(t_next, qt_ref, kcur_ref, kprev_ref, bias_ref, s_ref)
    oa = oa_ref[...]
    vcat = jnp.concatenate([vprev_ref[0, 0][:, WINDOW:], vcur_ref[0, 0]], axis=1)
    units = [[None] * A_KV_HEADS for _ in range(SWA_TOKENS // WINDOW)]
    x = x_ref[0]
    xb = x.astype(jnp.bfloat16)
    zw = A_WIDTH + B_WIDTH
    z = jnp.dot(xb, wzg_ref[:, :zw], preferred_element_type=jnp.float32)
    units[0][0] = _swa_unit(0, 0, s_ref, vcat, sink_ref)
    ga = jnp.dot(xb, wzg_ref[:, zw:zw + D_MODEL],
                 preferred_element_type=jnp.float32) + bg_ref[:, :D_MODEL]
    units[0][1] = _swa_unit(0, 1, s_ref, vcat, sink_ref)
    gb = jnp.dot(xb, wzg_ref[:, zw + D_MODEL:],
                 preferred_element_type=jnp.float32) + bg_ref[:, D_MODEL:]
    units[1][0] = _swa_unit(1, 0, s_ref, vcat, sink_ref)
    za, zb = z[:, :A_WIDTH], z[:, A_WIDTH:]
    ua = oa * (za * jax.nn.sigmoid(za))
    ub = ob_ref[0].astype(jnp.float32) * (zb * jax.nn.sigmoid(zb))
    ya = jnp.dot(ua.astype(jnp.bfloat16), woa_ref[...], preferred_element_type=jnp.float32)
    yb = jnp.dot(ub.astype(jnp.bfloat16), wob_ref[...], preferred_element_type=jnp.float32)
    units[1][1] = _swa_unit(1, 1, s_ref, vcat, sink_ref)
    merged = jax.nn.sigmoid(ga) * ya + jax.nn.sigmoid(gb) * yb
    out = jnp.dot(merged.astype(jnp.bfloat16), wout_ref[...], preferred_element_type=jnp.float32)
    oa_ref[...] = _swa_assemble(units)
    r = DN_ALPHA * x + out
    mu = jnp.mean(r, axis=-1, keepdims=True)
    rc = r - mu
    var = jnp.mean(rc * rc, axis=-1, keepdims=True)
    y_ref[0] = rc * lax.rsqrt(var + LN_EPS) * gamma_ref[...] + beta_ref[...]


def _final(x, dm, krm, bias_a, sink_a, ob, w_zg, b_gate, w_oa, w_ob, w_out, gamma, beta):
    B, S, D = x.shape
    T = SWA_TOKENS
    nt = S // T
    n_tiles = B * nt
    qa_blk = DM_QA // A_WIDTH
    va_blk = DM_VA // A_KV_WIDTH

    def tile(s, lag):
        i = jnp.clip(s - lag, 0, n_tiles - 1)
        return i // nt, i % nt

    def swa_spec(shape, idx):
        return pl.BlockSpec(shape, lambda s: idx(*tile(s, 0)))

    def fin_spec(width, lag):
        return pl.BlockSpec((1, T, width), lambda s: tile(s, lag) + (0,))

    full = lambda a: pl.BlockSpec(a.shape, lambda s: (0,) * a.ndim)
    return pl.pallas_call(
        functools.partial(_final_kernel, tiles_per_row=nt, n_tiles=n_tiles),
        grid=(n_tiles + 1,),
        in_specs=[
            fin_spec(D, 1),
            swa_spec((1, 1, A_WIDTH, T), lambda b, t: (b, t, qa_blk, 0)),
            swa_spec((1, 1, A_KV_WIDTH, T), lambda b, t: (b, t, va_blk, 0)),
            swa_spec((1, 1, A_KV_WIDTH, T), lambda b, t: (b, jnp.maximum(t - 1, 0), va_blk, 0)),
            swa_spec((1, T, A_KV_WIDTH), lambda b, t: (b, t, 0)),
            swa_spec((1, WINDOW, A_KV_WIDTH), lambda b, t: (b, jnp.maximum(2 * t - 1, 0), 0)),
            full(bias_a), full(sink_a),
            fin_spec(B_WIDTH, 1),
            full(w_zg), full(b_gate), full(w_oa), full(w_ob), full(w_out), full(gamma), full(beta),
        ],
        out_specs=fin_spec(D, 1),
        out_shape=jax.ShapeDtypeStruct((B, S, D), jnp.float32),
        scratch_shapes=[
            pltpu.VMEM((T, A_WIDTH), jnp.float32),
            pltpu.VMEM((T // WINDOW, A_KV_HEADS, 2 * WINDOW, A_GROUP * WINDOW), jnp.float32),
        ],
        compiler_params=pltpu.CompilerParams(
            dimension_semantics=("arbitrary",), vmem_limit_bytes=VMEM_LIMIT),
        name="final",
    )(x, dm, dm, dm, krm, krm, bias_a, sink_a, ob, w_zg, b_gate, w_oa, w_ob, w_out, gamma, beta)


def _toeplitz(v, rows, col0, ncols):
    H, L = v.shape
    assert col0 >= rows - 1 and col0 + ncols <= L - 1
    t = jnp.tile(v, (1, rows))[:, :rows * (L - 1)].reshape(H, rows, L - 1)
    return t[:, :, col0:col0 + ncols]


def _bias_tables(rel_table, sinks):
    table_a = rel_table[:, :A_HEADS].astype(jnp.float32) * LOG2E
    table_b = rel_table[:, A_HEADS:].astype(jnp.float32) * LOG2E
    sinks = sinks.astype(jnp.float32) * LOG2E
    d_a = np.arange(-2 * WINDOW, 3 * WINDOW)
    ok_a = (d_a >= 0) & (d_a < WINDOW)
    v_a = jnp.where(ok_a[:, None], table_a[_t5_bucket_np(d_a)], NEG).T
    ba = _toeplitz(v_a, 2 * WINDOW, 3 * WINDOW, WINDOW)
    ba = ba.reshape(A_KV_HEADS, A_GROUP, 2 * WINDOW, WINDOW)
    ba = ba.transpose(0, 2, 1, 3).reshape(A_KV_HEADS, 2 * WINDOW, A_GROUP * WINDOW)
    sink = jnp.broadcast_to(sinks.reshape(A_KV_HEADS, 1, A_GROUP, 1),
                            (A_KV_HEADS, 1, A_GROUP, WINDOW)).reshape(A_KV_HEADS, 1, A_GROUP * WINDOW)
    d_b = np.arange(-MOBA_BLOCK, 3 * MOBA_BLOCK)
    v_b = jnp.where((d_b >= 0)[:, None], table_b[_t5_bucket_np(d_b)], NEG).T
    b_own = _toeplitz(v_b, MOBA_BLOCK, MOBA_BLOCK, MOBA_BLOCK)
    b_prev = _toeplitz(v_b, MOBA_BLOCK, 2 * MOBA_BLOCK, MOBA_BLOCK)
    b_near = jnp.concatenate([b_prev, b_own], axis=1)
    c31 = table_b[N_BUCKETS - 1]
    return ba, sink, b_near, c31


def kernel(x, w_in, b_gate, sinks, rel_table, w_out_a, w_out_b, w_out, ln_gamma, ln_beta):
    B, S, D = x.shape
    assert (D, w_in.shape[0]) == (D_MODEL, DEPTH) and S % PROJ_TOKENS == 0
    w = w_in[0]
    sizes = (A_WIDTH, A_KV_WIDTH, A_KV_WIDTH, A_WIDTH, B_WIDTH, B_WIDTH, B_WIDTH, B_WIDTH,
             D_MODEL, D_MODEL)
    offs = np.concatenate([[0], np.cumsum(sizes)])
    w_qa, w_ka, w_va, w_za, w_qb, w_kb, w_vb, w_zb, w_ga, w_gb = [
        w[:, offs[i]:offs[i + 1]] for i in range(len(sizes))]
    bf = jnp.bfloat16
    q_scale = ATTN_SCALE * LOG2E
    w_rm = jnp.concatenate([w_ka, w_kb], axis=1).astype(bf)
    w_dm_t = jnp.concatenate([w_qb * q_scale, w_qa * q_scale, w_vb, w_va], axis=1).T.astype(bf)
    w_zg = jnp.concatenate([w_za, w_zb, w_ga, w_gb], axis=1).astype(bf)
    bias_a, sink_a, b_near, c31 = _bias_tables(rel_table, sinks[0])

    krm, dm, sel = _proj(x, w_rm, w_dm_t)
    ob = _moba(dm, krm, sel, b_near, c31)
    return _final(x, dm, krm, bias_a, sink_a, ob, w_zg, b_gate[0][None, :],
                  w_out_a[0].astype(bf), w_out_b[0].astype(bf), w_out[0].astype(bf),
                  ln_gamma[0][None, :], ln_beta[0][None, :])
```

```python
import functools
import math

import numpy as np
import jax
import jax.numpy as jnp
from jax import lax
from jax.experimental import pallas as pl
from jax.experimental.pallas import tpu as pltpu

D_MODEL = 1024
HEAD_DIM = 64
A_HEADS = 8
A_KV_HEADS = 2
A_GROUP = A_HEADS // A_KV_HEADS
A_WIDTH = A_HEADS * HEAD_DIM
A_KV_WIDTH = A_KV_HEADS * HEAD_DIM
WINDOW = 128
B_HEADS = 8
B_WIDTH = B_HEADS * HEAD_DIM
MOBA_BLOCK = 256
MOBA_TOPK = 3
N_BUCKETS = 32
MAX_DISTANCE = 128
DEPTH = 1
DN_ALPHA = (2.0 * DEPTH) ** 0.25
LN_EPS = 1e-5
NEG = -1e30
ATTN_SCALE = HEAD_DIM ** -0.5
LOG2E = math.log2(math.e)

LANE = 128
HEAD_PAD = 2 * HEAD_DIM
W_QB, W_QA, W_VB, W_VA = 0, B_WIDTH, B_WIDTH + A_WIDTH, 2 * B_WIDTH + A_WIDTH
W_ROWS = W_VA + A_KV_WIDTH
DM_QB, DM_QA = 0, B_HEADS * HEAD_PAD
DM_VB = DM_QA + A_WIDTH
DM_VA = DM_VB + B_WIDTH
DM_ROWS = DM_VA + A_KV_WIDTH
RM_COLS = A_KV_WIDTH + B_WIDTH
PROJ_TOKENS = 512
SWA_TOKENS = 256
FAR_CHUNK = 2
FAR_UNROLL = 80
NEAR_UNROLL = 16
ONES_ROWS = 16
SEL_PAD_ROWS = 8
VMEM_LIMIT = 56 * 1024 * 1024


def _t5_bucket_np(dist):
    max_exact = N_BUCKETS // 2
    n = np.maximum(dist, 0)
    nf = np.maximum(n, 1).astype(np.float32)
    large = max_exact + (np.log(nf / max_exact) / math.log(MAX_DISTANCE / max_exact)
                         * (N_BUCKETS - max_exact)).astype(np.int32)
    large = np.minimum(large, N_BUCKETS - 1)
    return np.where(n < max_exact, n, large).astype(np.int32)


def _select_rows(gate, n):
    nblk = gate.shape[0]
    blk = lax.broadcasted_iota(jnp.int32, gate.shape, 0)
    work = jnp.where(blk < n, gate, -jnp.inf)
    sel = jnp.zeros(gate.shape, jnp.float32)
    for _ in range(MOBA_TOPK):
        mx = jnp.max(work, axis=0, keepdims=True)
        idx = jnp.min(jnp.where(work == mx, blk, nblk), axis=0, keepdims=True)
        idx = jnp.where(mx > -jnp.inf, idx, nblk)
        pick = blk == idx
        sel = jnp.where(pick, 1.0, sel)
        work = jnp.where(pick, -jnp.inf, work)
    on_prev = jnp.max(jnp.where(blk == n - 1, sel, 0.0), axis=0, keepdims=True)
    far = jnp.where(blk < n - 1, sel, 0.0)
    pad_row = lax.broadcasted_iota(jnp.int32, (SEL_PAD_ROWS, gate.shape[1]), 0)
    return far, jnp.where(pad_row == 0, on_prev, 0.0)


def _proj_kernel(x_ref, wrm_ref, wdm_ref, krm_ref, dm_ref, sel_ref, kmean_ref):
    t = pl.program_id(1)
    blocks = PROJ_TOKENS // MOBA_BLOCK
    nblk = kmean_ref.shape[0]

    @pl.when(t == 0)
    def _():
        kmean_ref[...] = jnp.zeros(kmean_ref.shape, jnp.float32)

    xb = x_ref[0].astype(jnp.bfloat16)
    rm = jnp.dot(xb, wrm_ref[...], preferred_element_type=jnp.float32)
    for j in range(RM_COLS // LANE):
        krm_ref[0, j] = rm[:, j * LANE:(j + 1) * LANE].astype(jnp.bfloat16)
    for c in range(blocks):
        kb = rm[c * MOBA_BLOCK:(c + 1) * MOBA_BLOCK, A_KV_WIDTH:]
        kmean_ref[pl.ds(t * blocks + c, 1), :] = (
            jnp.sum(kb, axis=0, keepdims=True) * (1.0 / MOBA_BLOCK))
    dm = lax.dot_general(wdm_ref[...], xb, (((1,), (1,)), ((), ())),
                         preferred_element_type=jnp.float32)
    dmb = dm.astype(jnp.bfloat16)
    zeros = jnp.zeros((HEAD_DIM, MOBA_BLOCK), jnp.bfloat16)
    km = kmean_ref[...].astype(jnp.bfloat16)
    for c in range(blocks):
        piece = dmb[:, c * MOBA_BLOCK:(c + 1) * MOBA_BLOCK]
        rows = []
        for h in range(B_HEADS):
            q = piece[W_QB + h * HEAD_DIM:W_QB + (h + 1) * HEAD_DIM]
            qpad = jnp.concatenate([q, zeros] if h % 2 == 0 else [zeros, q], axis=0)
            rows.append(qpad)
            pair = h // 2
            gate = jnp.dot(km[:, pair * HEAD_PAD:(pair + 1) * HEAD_PAD], qpad,
                           preferred_element_type=jnp.float32)
            far, near = _select_rows(gate, t * blocks + c)
            sel_ref[0, h, c, :nblk, :] = far
            sel_ref[0, h, c, nblk:, :] = near
        dm_ref[0, c] = jnp.concatenate(rows + [piece[W_QA:]], axis=0)


def _proj(x, w_rm, w_dm_t):
    B, S, D = x.shape
    nt = S // PROJ_TOKENS
    cpt = PROJ_TOKENS // MOBA_BLOCK
    NB = S // MOBA_BLOCK
    return pl.pallas_call(
        _proj_kernel,
        grid=(B, nt),
        in_specs=[
            pl.BlockSpec((1, PROJ_TOKENS, D), lambda b, t: (b, t, 0)),
            pl.BlockSpec((D, RM_COLS), lambda b, t: (0, 0)),
            pl.BlockSpec((W_ROWS, D), lambda b, t: (0, 0)),
        ],
        out_specs=[
            pl.BlockSpec((1, RM_COLS // LANE, PROJ_TOKENS, LANE), lambda b, t: (b, 0, t, 0)),
            pl.BlockSpec((1, cpt, DM_ROWS, MOBA_BLOCK), lambda b, t: (b, t, 0, 0)),
            pl.BlockSpec((1, B_HEADS, cpt, NB + SEL_PAD_ROWS, MOBA_BLOCK),
                         lambda b, t: (b, 0, t, 0, 0)),
        ],
        out_shape=[
            jax.ShapeDtypeStruct((B, RM_COLS // LANE, S, LANE), jnp.bfloat16),
            jax.ShapeDtypeStruct((B, NB, DM_ROWS, MOBA_BLOCK), jnp.bfloat16),
            jax.ShapeDtypeStruct((B, B_HEADS, NB, NB + SEL_PAD_ROWS, MOBA_BLOCK), jnp.float32),
        ],
        scratch_shapes=[pltpu.VMEM((NB, B_WIDTH), jnp.float32)],
        compiler_params=pltpu.CompilerParams(
            dimension_semantics=("parallel", "arbitrary"), vmem_limit_bytes=VMEM_LIMIT),
        name="proj",
    )(x, w_rm, w_dm_t)


def _swa_logits(t, qt_ref, kcur_ref, kprev_ref, bias_ref, s_ref):
    qt = qt_ref[0, 0]
    kcat = jnp.concatenate([kprev_ref[0, 0], kcur_ref[0, 0]], axis=0)
    kidx = lax.broadcasted_iota(jnp.int32, (2 * WINDOW, A_GROUP * WINDOW), 0)
    zeros_q = jnp.zeros((HEAD_DIM, A_GROUP * WINDOW), jnp.bfloat16)
    for w in range(SWA_TOKENS // WINDOW):
        k2 = kcat[w * WINDOW:(w + 2) * WINDOW, :]
        for g in range(A_KV_HEADS):
            q4 = jnp.concatenate(
                [qt[(g * A_GROUP + i) * HEAD_DIM:(g * A_GROUP + i + 1) * HEAD_DIM,
                    w * WINDOW:(w + 1) * WINDOW] for i in range(A_GROUP)], axis=1)
            qpad = jnp.concatenate([q4, zeros_q] if g == 0 else [zeros_q, q4], axis=0)
            s = jnp.dot(k2, qpad, preferred_element_type=jnp.float32) + bias_ref[g]
            if w == 0:
                s = jnp.where((t == 0) & (kidx < WINDOW), NEG, s)
            s_ref[w, g] = s


def _swa_unit(w, g, s_ref, vcat, sink_ref):
    s = s_ref[w, g]
    sink = sink_ref[g]
    m = jnp.maximum(jnp.max(s, axis=0, keepdims=True), sink)
    p = jnp.exp2(s - m)
    l = jnp.sum(p, axis=0, keepdims=True) + jnp.exp2(sink - m)
    v2 = vcat[g * HEAD_DIM:(g + 1) * HEAD_DIM, w * WINDOW:(w + 2) * WINDOW]
    o = jnp.dot(v2, p.astype(jnp.bfloat16), preferred_element_type=jnp.float32) / l
    return [o[:, i * WINDOW:(i + 1) * WINDOW] for i in range(A_GROUP)]


def _swa_assemble(units):
    return jnp.concatenate(
        [jnp.concatenate([o for g in range(A_KV_HEADS) for o in units[w][g]], axis=0).T
         for w in range(SWA_TOKENS // WINDOW)], axis=0)


def _far_schedule(nblk):
    items = [(n, c) for n in range(nblk) for c in range((max(n - 1, 0) + FAR_CHUNK - 1) // FAR_CHUNK)]
    assert len(items) % FAR_UNROLL == 0
    return (np.array([i[0] for i in items], np.int32), np.array([i[1] for i in items], np.int32))


def _moba_kernel(nof_ref, cof_ref, c31_ref, q_ref, k_ref, vt_ref, sel_ref, bnear_ref, o_ref,
                 sa_ref, sb_ref, m_ref, acc_ref):
    pr = pl.program_id(1)
    nblk = q_ref.shape[1]
    nitems = nof_ref.shape[0]
    heads = (0, 1)
    c31 = [c31_ref[2 * pr + r] for r in heads]
    ones_rows = jnp.ones((ONES_ROWS, MOBA_BLOCK), jnp.bfloat16)

    def qpad(n, r):
        return q_ref[0, n, r * HEAD_PAD:(r + 1) * HEAD_PAD, :]

    def k_blk(j, nb=1):
        return k_ref[0, 0, pl.ds(pl.multiple_of(j * MOBA_BLOCK, MOBA_BLOCK), nb * MOBA_BLOCK), :]

    def vt_ext(blocks, r):
        return jnp.concatenate(
            [jnp.concatenate([vt_ref[0, j, r * HEAD_DIM:(r + 1) * HEAD_DIM, :], ones_rows], axis=0)
             for j in blocks], axis=1)

    def on_row(n, r, row):
        return sel_ref[0, r, n, pl.ds(row, 1), :] > 0.5

    def near_logits(n, buf_ref):
        jp = jnp.maximum(n - 1, 0)
        kc = jnp.concatenate([k_blk(jp), k_blk(n)], axis=0)
        out = []
        for r in heads:
            s = jnp.dot(kc, qpad(n, r), preferred_element_type=jnp.float32) + bnear_ref[r]
            buf_ref[r] = s
            out.append(jnp.maximum(
                jnp.max(s[MOBA_BLOCK:], axis=0, keepdims=True),
                jnp.where(on_row(n, r, nblk),
                          jnp.max(s[:MOBA_BLOCK], axis=0, keepdims=True), -jnp.inf)))
        return out

    def near_softmax(n, buf_ref, ms):
        jp = jnp.maximum(n - 1, 0)
        for r in heads:
            s = buf_ref[r]
            p_prev = jnp.exp2(s[:MOBA_BLOCK] - jnp.where(on_row(n, r, nblk), ms[r], -NEG))
            p_own = jnp.exp2(s[MOBA_BLOCK:] - ms[r])
            p = jnp.concatenate([p_prev, p_own], axis=0).astype(jnp.bfloat16)
            m_ref[n, r] = ms[r]
            acc_ref[n, r] = jnp.dot(vt_ext((jp, n), r), p, preferred_element_type=jnp.float32)

    def near_body(it, m_a):
        for u in range(NEAR_UNROLL // 2):
            n = NEAR_UNROLL * it + 2 * u
            m_b = near_logits(n + 1, sb_ref)
            near_softmax(n, sa_ref, m_a)
            m_a = near_logits(jnp.minimum(n + 2, nblk - 1), sa_ref)
            near_softmax(n + 1, sb_ref, m_b)
        return tuple(m_a)

    lax.fori_loop(0, nblk // NEAR_UNROLL, near_body, tuple(near_logits(0, sa_ref)))

    def far_logits(k, buf_ref):
        n, c = nof_ref[k], cof_ref[k]
        kc = k_blk(c * FAR_CHUNK, FAR_CHUNK)
        out = []
        for r in heads:
            s = jnp.dot(kc, qpad(n, r), preferred_element_type=jnp.float32)
            buf_ref[r] = s
            mc = jnp.full((1, MOBA_BLOCK), -jnp.inf, jnp.float32)
            for i in range(FAR_CHUNK):
                bm = jnp.max(s[i * MOBA_BLOCK:(i + 1) * MOBA_BLOCK], axis=0, keepdims=True)
                mc = jnp.maximum(mc, jnp.where(on_row(n, r, c * FAR_CHUNK + i), bm + c31[r], -jnp.inf))
            out.append(mc)
        return out

    def far_softmax(k, buf_ref, mcs):
        n, c = nof_ref[k], cof_ref[k]
        for r in heads:
            m = m_ref[n, r]
            m_new = jnp.maximum(m, mcs[r])
            alpha = jnp.exp2(m - m_new)
            shift = m_new - c31[r]
            s = buf_ref[r]
            ps = []
            for i in range(FAR_CHUNK):
                on = on_row(n, r, c * FAR_CHUNK + i)
                p = jnp.exp2(s[i * MOBA_BLOCK:(i + 1) * MOBA_BLOCK] - jnp.where(on, shift, -NEG))
                ps.append(p.astype(jnp.bfloat16))
            vt = vt_ext([c * FAR_CHUNK + i for i in range(FAR_CHUNK)], r)
            m_ref[n, r] = m_new
            acc_ref[n, r] = alpha * acc_ref[n, r] + jnp.dot(
                vt, jnp.concatenate(ps, axis=0), preferred_element_type=jnp.float32)

    def far_body(it, mc_a):
        for u in range(FAR_UNROLL // 2):
            k = FAR_UNROLL * it + 2 * u
            mc_b = far_logits(k + 1, sb_ref)
            far_softmax(k, sa_ref, mc_a)
            mc_a = far_logits(jnp.minimum(k + 2, nitems - 1), sa_ref)
            far_softmax(k + 1, sb_ref, mc_b)
        return tuple(mc_a)

    lax.fori_loop(0, nitems // FAR_UNROLL, far_body, tuple(far_logits(0, sa_ref)))

    def out_body(it, _):
        for u in range(2):
            n = 2 * it + u
            ot = jnp.concatenate([acc_ref[n, r, :HEAD_DIM, :] / acc_ref[n, r, HEAD_DIM:HEAD_DIM + 1, :]
                                  for r in heads], axis=0)
            o_ref[0, 0, pl.ds(pl.multiple_of(n * MOBA_BLOCK, MOBA_BLOCK), MOBA_BLOCK), :] = (
                ot.T.astype(jnp.bfloat16))
        return 0

    lax.fori_loop(0, nblk // 2, out_body, 0)


def _moba(dm, krm, sel, bias_near, c31):
    B, NB = dm.shape[0], dm.shape[1]
    S = NB * MOBA_BLOCK
    pairs = B_HEADS // 2
    vb_blk = DM_VB // LANE
    kb_blk = A_KV_WIDTH // LANE
    n_of, c_of = _far_schedule(NB)
    smem = pl.BlockSpec(memory_space=pltpu.SMEM)
    return pl.pallas_call(
        _moba_kernel,
        grid=(B, pairs),
        in_specs=[
            smem, smem, smem,
            pl.BlockSpec((1, NB, 2 * HEAD_PAD, MOBA_BLOCK), lambda b, p: (b, 0, p, 0)),
            pl.BlockSpec((1, 1, S, LANE), lambda b, p: (b, kb_blk + p, 0, 0)),
            pl.BlockSpec((1, NB, LANE, MOBA_BLOCK), lambda b, p: (b, 0, vb_blk + p, 0)),
            pl.BlockSpec((1, 2, NB, NB + SEL_PAD_ROWS, MOBA_BLOCK), lambda b, p: (b, p, 0, 0, 0)),
            pl.BlockSpec((2, 2 * MOBA_BLOCK, MOBA_BLOCK), lambda b, p: (p, 0, 0)),
        ],
        out_specs=pl.BlockSpec((1, 1, S, LANE), lambda b, p: (b, p, 0, 0)),
        out_shape=jax.ShapeDtypeStruct((B, pairs, S, LANE), jnp.bfloat16),
        scratch_shapes=[
            pltpu.VMEM((2, FAR_CHUNK * MOBA_BLOCK, MOBA_BLOCK), jnp.float32),
            pltpu.VMEM((2, FAR_CHUNK * MOBA_BLOCK, MOBA_BLOCK), jnp.float32),
            pltpu.VMEM((NB, 2, 1, MOBA_BLOCK), jnp.float32),
            pltpu.VMEM((NB, 2, HEAD_DIM + ONES_ROWS, MOBA_BLOCK), jnp.float32),
        ],
        compiler_params=pltpu.CompilerParams(
            dimension_semantics=("parallel", "parallel"), vmem_limit_bytes=VMEM_LIMIT),
        name="moba",
    )(jnp.asarray(n_of), jnp.asarray(c_of), c31, dm, krm, dm, sel, bias_near)


def _final_kernel(x_ref, qt_ref, vcur_ref, vprev_ref, kcur_ref, kprev_ref, bias_ref, sink_ref,
                  ob_ref, wzg_ref, bg_ref, woa_ref, wob_ref, wout_ref, gamma_ref, beta_ref, y_ref,
                  oa_ref, s_ref, *, tiles_per_row, n_tiles):
    s = pl.program_id(0)

    @pl.when(s == 0)
    def _():
        oa_ref[...] = jnp.zeros(oa_ref.shape, jnp.float32)

    t_next = jnp.minimum(s, n_tiles - 1) % tiles_per_row
    _swa_logits(t_next, qt_ref, kcur_ref, kprev_ref, bias_ref, s_ref)
    oa = oa_ref[...]
    vcat = jnp.concatenate([vprev_ref[0, 0][:, WINDOW:], vcur_ref[0, 0]], axis=1)
    units = [[None] * A_KV_HEADS for _ in range(SWA_TOKENS // WINDOW)]
    x = x_ref[0]
    xb = x.astype(jnp.bfloat16)
    zw = A_WIDTH + B_WIDTH
    z = jnp.dot(xb, wzg_ref[:, :zw], preferred_element_type=jnp.float32)
    units[0][0] = _swa_unit(0, 0, s_ref, vcat, sink_ref)
    ga = jnp.dot(xb, wzg_ref[:, zw:zw + D_MODEL],
                 preferred_element_type=jnp.float32) + bg_ref[:, :D_MODEL]
    units[0][1] = _swa_unit(0, 1, s_ref, vcat, sink_ref)
    gb = jnp.dot(xb, wzg_ref[:, zw + D_MODEL:],
                 preferred_element_type=jnp.float32) + bg_ref[:, D_MODEL:]
    units[1][0] = _swa_unit(1, 0, s_ref, vcat, sink_ref)
    za, zb = z[:, :A_WIDTH], z[:, A_WIDTH:]
    ua = oa * (za * jax.nn.sigmoid(za))
    ob = jnp.concatenate([ob_ref[0, j] for j in range(ob_ref.shape[1])], axis=1)
    ub = ob.astype(jnp.float32) * (zb * jax.nn.sigmoid(zb))
    ya = jnp.dot(ua.astype(jnp.bfloat16), woa_ref[...], preferred_element_type=jnp.float32)
    yb = jnp.dot(ub.astype(jnp.bfloat16), wob_ref[...], preferred_element_type=jnp.float32)
    units[1][1] = _swa_unit(1, 1, s_ref, vcat, sink_ref)
    merged = jax.nn.sigmoid(ga) * ya + jax.nn.sigmoid(gb) * yb
    out = jnp.dot(merged.astype(jnp.bfloat16), wout_ref[...], preferred_element_type=jnp.float32)
    oa_ref[...] = _swa_assemble(units)
    r = DN_ALPHA * x + out
    mu = jnp.mean(r, axis=-1, keepdims=True)
    rc = r - mu
    var = jnp.mean(rc * rc, axis=-1, keepdims=True)
    y_ref[0] = rc * lax.rsqrt(var + LN_EPS) * gamma_ref[...] + beta_ref[...]


def _final(x, dm, krm, bias_a, sink_a, ob, w_zg, b_gate, w_oa, w_ob, w_out, gamma, beta):
    B, S, D = x.shape
    T = SWA_TOKENS
    nt = S // T
    n_tiles = B * nt
    qa_blk = DM_QA // A_WIDTH
    va_blk = DM_VA // A_KV_WIDTH

    def tile(s, lag):
        i = jnp.clip(s - lag, 0, n_tiles - 1)
        return i // nt, i % nt

    def swa_spec(shape, idx):
        return pl.BlockSpec(shape, lambda s: idx(*tile(s, 0)))

    def fin_spec(width, lag):
        return pl.BlockSpec((1, T, width), lambda s: tile(s, lag) + (0,))

    full = lambda a: pl.BlockSpec(a.shape, lambda s: (0,) * a.ndim)
    return pl.pallas_call(
        functools.partial(_final_kernel, tiles_per_row=nt, n_tiles=n_tiles),
        grid=(n_tiles + 1,),
        in_specs=[
            fin_spec(D, 1),
            swa_spec((1, 1, A_WIDTH, T), lambda b, t: (b, t, qa_blk, 0)),
            swa_spec((1, 1, A_KV_WIDTH, T), lambda b, t: (b, t, va_blk, 0)),
            swa_spec((1, 1, A_KV_WIDTH, T), lambda b, t: (b, jnp.maximum(t - 1, 0), va_blk, 0)),
            swa_spec((1, 1, T, A_KV_WIDTH), lambda b, t: (b, 0, t, 0)),
            swa_spec((1, 1, WINDOW, A_KV_WIDTH), lambda b, t: (b, 0, jnp.maximum(2 * t - 1, 0), 0)),
            full(bias_a), full(sink_a),
            pl.BlockSpec((1, B_HEADS // 2, T, LANE),
                         lambda s: (tile(s, 1)[0], 0, tile(s, 1)[1], 0)),
            full(w_zg), full(b_gate), full(w_oa), full(w_ob), full(w_out), full(gamma), full(beta),
        ],
        out_specs=fin_spec(D, 1),
        out_shape=jax.ShapeDtypeStruct((B, S, D), jnp.float32),
        scratch_shapes=[
            pltpu.VMEM((T, A_WIDTH), jnp.float32),
            pltpu.VMEM((T // WINDOW, A_KV_HEADS, 2 * WINDOW, A_GROUP * WINDOW), jnp.float32),
        ],
        compiler_params=pltpu.CompilerParams(
            dimension_semantics=("arbitrary",), vmem_limit_bytes=VMEM_LIMIT),
        name="final",
    )(x, dm, dm, dm, krm, krm, bias_a, sink_a, ob, w_zg, b_gate, w_oa, w_ob, w_out, gamma, beta)


def _toeplitz(v, rows, col0, ncols):
    H, L = v.shape
    assert col0 >= rows - 1 and col0 + ncols <= L - 1
    t = jnp.tile(v, (1, rows))[:, :rows * (L - 1)].reshape(H, rows, L - 1)
    return t[:, :, col0:col0 + ncols]


def _bias_tables(rel_table, sinks):
    table_a = rel_table[:, :A_HEADS].astype(jnp.float32) * LOG2E
    table_b = rel_table[:, A_HEADS:].astype(jnp.float32) * LOG2E
    sinks = sinks.astype(jnp.float32) * LOG2E
    d_a = np.arange(-2 * WINDOW, 3 * WINDOW)
    ok_a = (d_a >= 0) & (d_a < WINDOW)
    v_a = jnp.where(ok_a[:, None], table_a[_t5_bucket_np(d_a)], NEG).T
    ba = _toeplitz(v_a, 2 * WINDOW, 3 * WINDOW, WINDOW)
    ba = ba.reshape(A_KV_HEADS, A_GROUP, 2 * WINDOW, WINDOW)
    ba = ba.transpose(0, 2, 1, 3).reshape(A_KV_HEADS, 2 * WINDOW, A_GROUP * WINDOW)
    sink = jnp.broadcast_to(sinks.reshape(A_KV_HEADS, 1, A_GROUP, 1),
                            (A_KV_HEADS, 1, A_GROUP, WINDOW)).reshape(A_KV_HEADS, 1, A_GROUP * WINDOW)
    d_b = np.arange(-MOBA_BLOCK, 3 * MOBA_BLOCK)
    v_b = jnp.where((d_b >= 0)[:, None], table_b[_t5_bucket_np(d_b)], NEG).T
    b_own = _toeplitz(v_b, MOBA_BLOCK, MOBA_BLOCK, MOBA_BLOCK)
    b_prev = _toeplitz(v_b, MOBA_BLOCK, 2 * MOBA_BLOCK, MOBA_BLOCK)
    b_near = jnp.concatenate([b_prev, b_own], axis=1)
    c31 = table_b[N_BUCKETS - 1]
    return ba, sink, b_near, c31


def kernel(x, w_in, b_gate, sinks, rel_table, w_out_a, w_out_b, w_out, ln_gamma, ln_beta):
    B, S, D = x.shape
    assert (D, w_in.shape[0]) == (D_MODEL, DEPTH) and S % PROJ_TOKENS == 0
    w = w_in[0]
    sizes = (A_WIDTH, A_KV_WIDTH, A_KV_WIDTH, A_WIDTH, B_WIDTH, B_WIDTH, B_WIDTH, B_WIDTH,
             D_MODEL, D_MODEL)
    offs = np.concatenate([[0], np.cumsum(sizes)])
    w_qa, w_ka, w_va, w_za, w_qb, w_kb, w_vb, w_zb, w_ga, w_gb = [
        w[:, offs[i]:offs[i + 1]] for i in range(len(sizes))]
    bf = jnp.bfloat16
    q_scale = ATTN_SCALE * LOG2E
    w_rm = jnp.concatenate([w_ka, w_kb], axis=1).astype(bf)
    w_dm_t = jnp.concatenate([w_qb * q_scale, w_qa * q_scale, w_vb, w_va], axis=1).T.astype(bf)
    w_zg = jnp.concatenate([w_za, w_zb, w_ga, w_gb], axis=1).astype(bf)
    bias_a, sink_a, b_near, c31 = _bias_tables(rel_table, sinks[0])

    krm, dm, sel = _proj(x, w_rm, w_dm_t)
    ob = _moba(dm, krm, sel, b_near, c31)
    return _final(x, dm, krm, bias_a, sink_a, ob, w_zg, b_gate[0][None, :],
                  w_out_a[0].astype(bf), w_out_b[0].astype(bf), w_out[0].astype(bf),
                  ln_gamma[0][None, :], ln_beta[0][None, :])
```

```python
import functools
import math

import numpy as np
import jax
import jax.numpy as jnp
from jax import lax
from jax.experimental import pallas as pl
from jax.experimental.pallas import tpu as pltpu

D_MODEL = 1024
HEAD_DIM = 64
A_HEADS = 8
A_KV_HEADS = 2
A_GROUP = A_HEADS // A_KV_HEADS
A_WIDTH = A_HEADS * HEAD_DIM
A_KV_WIDTH = A_KV_HEADS * HEAD_DIM
WINDOW = 128
B_HEADS = 8
B_WIDTH = B_HEADS * HEAD_DIM
MOBA_BLOCK = 256
MOBA_TOPK = 3
N_BUCKETS = 32
MAX_DISTANCE = 128
DEPTH = 1
DN_ALPHA = (2.0 * DEPTH) ** 0.25
LN_EPS = 1e-5
NEG = -1e30
ATTN_SCALE = HEAD_DIM ** -0.5
LOG2E = math.log2(math.e)

LANE = 128
HEAD_PAD = 2 * HEAD_DIM
W_QB, W_QA, W_VB, W_VA = 0, B_WIDTH, B_WIDTH + A_WIDTH, 2 * B_WIDTH + A_WIDTH
W_ROWS = W_VA + A_KV_WIDTH
DM_QB, DM_QA = 0, B_HEADS * HEAD_PAD
DM_VB = DM_QA + A_WIDTH
DM_VA = DM_VB + B_WIDTH
DM_ROWS = DM_VA + A_KV_WIDTH
RM_COLS = A_KV_WIDTH + B_WIDTH
PROJ_TOKENS = 512
SWA_TOKENS = 256
FAR_CHUNK = 2
FAR_UNROLL = 80
NEAR_UNROLL = 16
ONES_ROWS = 16
SEL_PAD_ROWS = 8
VMEM_LIMIT = 56 * 1024 * 1024


def _t5_bucket_np(dist):
    max_exact = N_BUCKETS // 2
    n = np.maximum(dist, 0)
    nf = np.maximum(n, 1).astype(np.float32)
    large = max_exact + (np.log(nf / max_exact) / math.log(MAX_DISTANCE / max_exact)
                         * (N_BUCKETS - max_exact)).astype(np.int32)
    large = np.minimum(large, N_BUCKETS - 1)
    return np.where(n < max_exact, n, large).astype(np.int32)


def _select_rows(gate, n):
    nblk = gate.shape[0]
    blk = lax.broadcasted_iota(jnp.int32, gate.shape, 0)
    work = jnp.where(blk < n, gate, -jnp.inf)
    sel = jnp.zeros(gate.shape, jnp.float32)
    for _ in range(MOBA_TOPK):
        mx = jnp.max(work, axis=0, keepdims=True)
        idx = jnp.min(jnp.where(work == mx, blk, nblk), axis=0, keepdims=True)
        idx = jnp.where(mx > -jnp.inf, idx, nblk)
        pick = blk == idx
        sel = jnp.where(pick, 1.0, sel)
        work = jnp.where(pick, -jnp.inf, work)
    on_prev = jnp.max(jnp.where(blk == n - 1, sel, 0.0), axis=0, keepdims=True)
    far = jnp.where(blk < n - 1, sel, 0.0)
    pad_row = lax.broadcasted_iota(jnp.int32, (SEL_PAD_ROWS, gate.shape[1]), 0)
    return far, jnp.where(pad_row == 0, on_prev, 0.0)


def _proj_kernel(x_ref, wrm_ref, wdm_ref, krm_ref, dm_ref, sel_ref, kmean_ref):
    t = pl.program_id(1)
    blocks = PROJ_TOKENS // MOBA_BLOCK
    nblk = kmean_ref.shape[0]

    @pl.when(t == 0)
    def _():
        kmean_ref[...] = jnp.zeros(kmean_ref.shape, jnp.float32)

    xb = x_ref[0].astype(jnp.bfloat16)
    rm = jnp.dot(xb, wrm_ref[...], preferred_element_type=jnp.float32)
    krm_ref[0] = rm.astype(jnp.bfloat16)
    for c in range(blocks):
        kb = rm[c * MOBA_BLOCK:(c + 1) * MOBA_BLOCK, A_KV_WIDTH:]
        kmean_ref[pl.ds(t * blocks + c, 1), :] = (
            jnp.sum(kb, axis=0, keepdims=True) * (1.0 / MOBA_BLOCK))
    dm = lax.dot_general(wdm_ref[...], xb, (((1,), (1,)), ((), ())),
                         preferred_element_type=jnp.float32)
    dmb = dm.astype(jnp.bfloat16)
    zeros = jnp.zeros((HEAD_DIM, MOBA_BLOCK), jnp.bfloat16)
    km = kmean_ref[...].astype(jnp.bfloat16)
    for c in range(blocks):
        piece = dmb[:, c * MOBA_BLOCK:(c + 1) * MOBA_BLOCK]
        rows = []
        for h in range(B_HEADS):
            q = piece[W_QB + h * HEAD_DIM:W_QB + (h + 1) * HEAD_DIM]
            qpad = jnp.concatenate([q, zeros] if h % 2 == 0 else [zeros, q], axis=0)
            rows.append(qpad)
            pair = h // 2
            gate = jnp.dot(km[:, pair * HEAD_PAD:(pair + 1) * HEAD_PAD], qpad,
                           preferred_element_type=jnp.float32)
            far, near = _select_rows(gate, t * blocks + c)
            sel_ref[0, h, c, :nblk, :] = far
            sel_ref[0, h, c, nblk:, :] = near
        dm_ref[0, c] = jnp.concatenate(rows + [piece[W_QA:]], axis=0)


def _proj(x, w_rm, w_dm_t):
    B, S, D = x.shape
    nt = S // PROJ_TOKENS
    cpt = PROJ_TOKENS // MOBA_BLOCK
    NB = S // MOBA_BLOCK
    return pl.pallas_call(
        _proj_kernel,
        grid=(B, nt),
        in_specs=[
            pl.BlockSpec((1, PROJ_TOKENS, D), lambda b, t: (b, t, 0)),
            pl.BlockSpec((D, RM_COLS), lambda b, t: (0, 0)),
            pl.BlockSpec((W_ROWS, D), lambda b, t: (0, 0)),
        ],
        out_specs=[
            pl.BlockSpec((1, PROJ_TOKENS, RM_COLS), lambda b, t: (b, t, 0)),
            pl.BlockSpec((1, cpt, DM_ROWS, MOBA_BLOCK), lambda b, t: (b, t, 0, 0)),
            pl.BlockSpec((1, B_HEADS, cpt, NB + SEL_PAD_ROWS, MOBA_BLOCK),
                         lambda b, t: (b, 0, t, 0, 0)),
        ],
        out_shape=[
            jax.ShapeDtypeStruct((B, S, RM_COLS), jnp.bfloat16),
            jax.ShapeDtypeStruct((B, NB, DM_ROWS, MOBA_BLOCK), jnp.bfloat16),
            jax.ShapeDtypeStruct((B, B_HEADS, NB, NB + SEL_PAD_ROWS, MOBA_BLOCK), jnp.float32),
        ],
        scratch_shapes=[pltpu.VMEM((NB, B_WIDTH), jnp.float32)],
        compiler_params=pltpu.CompilerParams(
            dimension_semantics=("parallel", "arbitrary"), vmem_limit_bytes=VMEM_LIMIT),
        name="proj",
    )(x, w_rm, w_dm_t)


def _swa_logits(t, qt_ref, kcur_ref, kprev_ref, bias_ref, s_ref):
    qt = qt_ref[0, 0]
    kcat = jnp.concatenate([kprev_ref[0], kcur_ref[0]], axis=0)
    kidx = lax.broadcasted_iota(jnp.int32, (2 * WINDOW, A_GROUP * WINDOW), 0)
    zeros_q = jnp.zeros((HEAD_DIM, A_GROUP * WINDOW), jnp.bfloat16)
    for w in range(SWA_TOKENS // WINDOW):
        k2 = kcat[w * WINDOW:(w + 2) * WINDOW, :]
        for g in range(A_KV_HEADS):
            q4 = jnp.concatenate(
                [qt[(g * A_GROUP + i) * HEAD_DIM:(g * A_GROUP + i + 1) * HEAD_DIM,
                    w * WINDOW:(w + 1) * WINDOW] for i in range(A_GROUP)], axis=1)
            qpad = jnp.concatenate([q4, zeros_q] if g == 0 else [zeros_q, q4], axis=0)
            s = jnp.dot(k2, qpad, preferred_element_type=jnp.float32) + bias_ref[g]
            if w == 0:
                s = jnp.where((t == 0) & (kidx < WINDOW), NEG, s)
            s_ref[w, g] = s


def _swa_unit(w, g, s_ref, vcat, sink_ref):
    s = s_ref[w, g]
    sink = sink_ref[g]
    m = jnp.maximum(jnp.max(s, axis=0, keepdims=True), sink)
    p = jnp.exp2(s - m)
    l = jnp.sum(p, axis=0, keepdims=True) + jnp.exp2(sink - m)
    v2 = vcat[g * HEAD_DIM:(g + 1) * HEAD_DIM, w * WINDOW:(w + 2) * WINDOW]
    o = jnp.dot(v2, p.astype(jnp.bfloat16), preferred_element_type=jnp.float32) / l
    return [o[:, i * WINDOW:(i + 1) * WINDOW] for i in range(A_GROUP)]


def _swa_assemble(units):
    return jnp.concatenate(
        [jnp.concatenate([o for g in range(A_KV_HEADS) for o in units[w][g]], axis=0).T
         for w in range(SWA_TOKENS // WINDOW)], axis=0)


def _far_schedule(nblk):
    items = [(n, c) for n in range(nblk) for c in range((max(n - 1, 0) + FAR_CHUNK - 1) // FAR_CHUNK)]
    assert len(items) % FAR_UNROLL == 0
    return (np.array([i[0] for i in items], np.int32), np.array([i[1] for i in items], np.int32))


def _moba_kernel(nof_ref, cof_ref, c31_ref, q_ref, k_ref, vt_ref, sel_ref, bnear_ref, o_ref,
                 sa_ref, sb_ref, m_ref, acc_ref):
    pr = pl.program_id(1)
    nblk = q_ref.shape[1]
    nitems = nof_ref.shape[0]
    heads = (0, 1)
    c31 = [c31_ref[2 * pr + r] for r in heads]
    ones_rows = jnp.ones((ONES_ROWS, MOBA_BLOCK), jnp.bfloat16)

    def qpad(n, r):
        return q_ref[0, n, r * HEAD_PAD:(r + 1) * HEAD_PAD, :]

    def k_blk(j, nb=1):
        return k_ref[0, pl.ds(pl.multiple_of(j * MOBA_BLOCK, MOBA_BLOCK), nb * MOBA_BLOCK), :]

    def vt_ext(blocks, r):
        return jnp.concatenate(
            [jnp.concatenate([vt_ref[0, j, r * HEAD_DIM:(r + 1) * HEAD_DIM, :], ones_rows], axis=0)
             for j in blocks], axis=1)

    def on_row(n, r, row):
        return sel_ref[0, r, n, pl.ds(row, 1), :] > 0.5

    def near_logits(n, buf_ref):
        jp = jnp.maximum(n - 1, 0)
        kc = jnp.concatenate([k_blk(jp), k_blk(n)], axis=0)
        out = []
        for r in heads:
            s = jnp.dot(kc, qpad(n, r), preferred_element_type=jnp.float32) + bnear_ref[r]
            buf_ref[r] = s
            out.append(jnp.maximum(
                jnp.max(s[MOBA_BLOCK:], axis=0, keepdims=True),
                jnp.where(on_row(n, r, nblk),
                          jnp.max(s[:MOBA_BLOCK], axis=0, keepdims=True), -jnp.inf)))
        return out

    def near_softmax(n, buf_ref, ms):
        jp = jnp.maximum(n - 1, 0)
        for r in heads:
            s = buf_ref[r]
            p_prev = jnp.exp2(s[:MOBA_BLOCK] - jnp.where(on_row(n, r, nblk), ms[r], -NEG))
            p_own = jnp.exp2(s[MOBA_BLOCK:] - ms[r])
            p = jnp.concatenate([p_prev, p_own], axis=0).astype(jnp.bfloat16)
            m_ref[n, r] = ms[r]
            acc_ref[n, r] = jnp.dot(vt_ext((jp, n), r), p, preferred_element_type=jnp.float32)

    def near_body(it, m_a):
        for u in range(NEAR_UNROLL // 2):
            n = NEAR_UNROLL * it + 2 * u
            m_b = near_logits(n + 1, sb_ref)
            near_softmax(n, sa_ref, m_a)
            m_a = near_logits(jnp.minimum(n + 2, nblk - 1), sa_ref)
            near_softmax(n + 1, sb_ref, m_b)
        return tuple(m_a)

    lax.fori_loop(0, nblk // NEAR_UNROLL, near_body, tuple(near_logits(0, sa_ref)))

    def far_logits(k, buf_ref):
        n, c = nof_ref[k], cof_ref[k]
        kc = k_blk(c * FAR_CHUNK, FAR_CHUNK)
        out = []
        for r in heads:
            s = jnp.dot(kc, qpad(n, r), preferred_element_type=jnp.float32)
            buf_ref[r] = s
            mc = jnp.full((1, MOBA_BLOCK), -jnp.inf, jnp.float32)
            for i in range(FAR_CHUNK):
                bm = jnp.max(s[i * MOBA_BLOCK:(i + 1) * MOBA_BLOCK], axis=0, keepdims=True)
                mc = jnp.maximum(mc, jnp.where(on_row(n, r, c * FAR_CHUNK + i), bm + c31[r], -jnp.inf))
            out.append(mc)
        return out

    def far_softmax(k, buf_ref, mcs):
        n, c = nof_ref[k], cof_ref[k]
        for r in heads:
            m = m_ref[n, r]
            m_new = jnp.maximum(m, mcs[r])
            alpha = jnp.exp2(m - m_new)
            shift = m_new - c31[r]
            s = buf_ref[r]
            ps = []
            for i in range(FAR_CHUNK):
                on = on_row(n, r, c * FAR_CHUNK + i)
                p = jnp.exp2(s[i * MOBA_BLOCK:(i + 1) * MOBA_BLOCK] - jnp.where(on, shift, -NEG))
                ps.append(p.astype(jnp.bfloat16))
            vt = vt_ext([c * FAR_CHUNK + i for i in range(FAR_CHUNK)], r)
            m_ref[n, r] = m_new
            acc_ref[n, r] = alpha * acc_ref[n, r] + jnp.dot(
                vt, jnp.concatenate(ps, axis=0), preferred_element_type=jnp.float32)

    def far_body(it, mc_a):
        for u in range(FAR_UNROLL // 2):
            k = FAR_UNROLL * it + 2 * u
            mc_b = far_logits(k + 1, sb_ref)
            far_softmax(k, sa_ref, mc_a)
            mc_a = far_logits(jnp.minimum(k + 2, nitems - 1), sa_ref)
            far_softmax(k + 1, sb_ref, mc_b)
        return tuple(mc_a)

    lax.fori_loop(0, nitems // FAR_UNROLL, far_body, tuple(far_logits(0, sa_ref)))

    def out_body(it, _):
        for u in range(2):
            n = 2 * it + u
            ot = jnp.concatenate([acc_ref[n, r, :HEAD_DIM, :] / acc_ref[n, r, HEAD_DIM:HEAD_DIM + 1, :]
                                  for r in heads], axis=0)
            o_ref[0, pl.ds(pl.multiple_of(n * MOBA_BLOCK, MOBA_BLOCK), MOBA_BLOCK), :] = (
                ot.T.astype(jnp.bfloat16))
        return 0

    lax.fori_loop(0, nblk // 2, out_body, 0)


def _moba(dm, krm, sel, bias_near, c31):
    B, NB = dm.shape[0], dm.shape[1]
    S = NB * MOBA_BLOCK
    pairs = B_HEADS // 2
    vb_blk = DM_VB // LANE
    kb_blk = A_KV_WIDTH // LANE
    n_of, c_of = _far_schedule(NB)
    smem = pl.BlockSpec(memory_space=pltpu.SMEM)
    return pl.pallas_call(
        _moba_kernel,
        grid=(B, pairs),
        in_specs=[
            smem, smem, smem,
            pl.BlockSpec((1, NB, 2 * HEAD_PAD, MOBA_BLOCK), lambda b, p: (b, 0, p, 0)),
            pl.BlockSpec((1, S, LANE), lambda b, p: (b, 0, kb_blk + p)),
            pl.BlockSpec((1, NB, LANE, MOBA_BLOCK), lambda b, p: (b, 0, vb_blk + p, 0)),
            pl.BlockSpec((1, 2, NB, NB + SEL_PAD_ROWS, MOBA_BLOCK), lambda b, p: (b, p, 0, 0, 0)),
            pl.BlockSpec((2, 2 * MOBA_BLOCK, MOBA_BLOCK), lambda b, p: (p, 0, 0)),
        ],
        out_specs=pl.BlockSpec((1, S, LANE), lambda b, p: (b, 0, p)),
        out_shape=jax.ShapeDtypeStruct((B, S, B_WIDTH), jnp.bfloat16),
        scratch_shapes=[
            pltpu.VMEM((2, FAR_CHUNK * MOBA_BLOCK, MOBA_BLOCK), jnp.float32),
            pltpu.VMEM((2, FAR_CHUNK * MOBA_BLOCK, MOBA_BLOCK), jnp.float32),
            pltpu.VMEM((NB, 2, 1, MOBA_BLOCK), jnp.float32),
            pltpu.VMEM((NB, 2, HEAD_DIM + ONES_ROWS, MOBA_BLOCK), jnp.float32),
        ],
        compiler_params=pltpu.CompilerParams(
            dimension_semantics=("parallel", "parallel"), vmem_limit_bytes=VMEM_LIMIT),
        name="moba",
    )(jnp.asarray(n_of), jnp.asarray(c_of), c31, dm, krm, dm, sel, bias_near)


def _final_kernel(x_ref, qt_ref, vcur_ref, vprev_ref, kcur_ref, kprev_ref, bias_ref, sink_ref,
                  ob_ref, wzg_ref, bg_ref, woa_ref, wob_ref, wout_ref, gamma_ref, beta_ref, y_ref,
                  oa_ref, s_ref, *, tiles_per_row, n_tiles):
    s = pl.program_id(0)

    @pl.when(s == 0)
    def _():
        oa_ref[...] = jnp.zeros(oa_ref.shape, jnp.float32)

    t_next = jnp.minimum(s, n_tiles - 1) % tiles_per_row
    _swa_logits(t_next, qt_ref, kcur_ref, kprev_ref, bias_ref, s_ref)
    oa = oa_ref[...]
    vcat = jnp.concatenate([vprev_ref[0, 0][:, WINDOW:], vcur_ref[0, 0]], axis=1)
    units = [[None] * A_KV_HEADS for _ in range(SWA_TOKENS // WINDOW)]
    x = x_ref[0]
    xb = x.astype(jnp.bfloat16)
    zw = A_WIDTH + B_WIDTH
    z = jnp.dot(xb, wzg_ref[:, :zw], preferred_element_type=jnp.float32)
    units[0][0] = _swa_unit(0, 0, s_ref, vcat, sink_ref)
    ga = jnp.dot(xb, wzg_ref[:, zw:zw + D_MODEL],
                 preferred_element_type=jnp.float32) + bg_ref[:, :D_MODEL]
    units[0][1] = _swa_unit(0, 1, s_ref, vcat, sink_ref)
    gb = jnp.dot(xb, wzg_ref[:, zw + D_MODEL:],
                 preferred_element_type=jnp.float32) + bg_ref[:, D_MODEL:]
    units[1][0] = _swa_unit(1, 0, s_ref, vcat, sink_ref)
    za, zb = z[:, :A_WIDTH], z[:, A_WIDTH:]
    ua = oa * (za * jax.nn.sigmoid(za))
    ub = ob_ref[0].astype(jnp.float32) * (zb * jax.nn.sigmoid(zb))
    ya = jnp.dot(ua.astype(jnp.bfloat16), woa_ref[...], preferred_element_type=jnp.float32)
    yb = jnp.dot(ub.astype(jnp.bfloat16), wob_ref[...], preferred_element_type=jnp.float32)
    units[1][1] = _swa_unit(1, 1, s_ref, vcat, sink_ref)
    merged = jax.nn.sigmoid(ga) * ya + jax.nn.sigmoid(gb) * yb
    out = jnp.dot(merged.astype(jnp.bfloat16), wout_ref[...], preferred_element_type=jnp.float32)
    oa_ref[...] = _swa_assemble(units)
    r = DN_ALPHA * x + out
    mu = jnp.mean(r, axis=-1, keepdims=True)
    rc = r - mu
    var = jnp.mean(rc * rc, axis=-1, keepdims=True)
    y_ref[0] = rc * lax.rsqrt(var + LN_EPS) * gamma_ref[...] + beta_ref[...]


def _final(x, dm, krm, bias_a, sink_a, ob, w_zg, b_gate, w_oa, w_ob, w_out, gamma, beta):
    B, S, D = x.shape
    T = SWA_TOKENS
    nt = S // T
    n_tiles = B * nt
    qa_blk = DM_QA // A_WIDTH
    va_blk = DM_VA // A_KV_WIDTH

    def tile(s, lag):
        i = jnp.clip(s - lag, 0, n_tiles - 1)
        return i // nt, i % nt

    def swa_spec(shape, idx):
        return pl.BlockSpec(shape, lambda s: idx(*tile(s, 0)))

    def fin_spec(width, lag):
        return pl.BlockSpec((1, T, width), lambda s: tile(s, lag) + (0,))

    full = lambda a: pl.BlockSpec(a.shape, lambda s: (0,) * a.ndim)
    return pl.pallas_call(
        functools.partial(_final_kernel, tiles_per_row=nt, n_tiles=n_tiles),
        grid=(n_tiles + 1,),
        in_specs=[
            fin_spec(D, 1),
            swa_spec((1, 1, A_WIDTH, T), lambda b, t: (b, t, qa_blk, 0)),
            swa_spec((1, 1, A_KV_WIDTH, T), lambda b, t: (b, t, va_blk, 0)),
            swa_spec((1, 1, A_KV_WIDTH, T), lambda b, t: (b, jnp.maximum(t - 1, 0), va_blk, 0)),
            swa_spec((1, T, A_KV_WIDTH), lambda b, t: (b, t, 0)),
            swa_spec((1, WINDOW, A_KV_WIDTH), lambda b, t: (b, jnp.maximum(2 * t - 1, 0), 0)),
            full(bias_a), full(sink_a),
            fin_spec(B_WIDTH, 1),
            full(w_zg), full(b_gate), full(w_oa), full(w_ob), full(w_out), full(gamma), full(beta),
        ],
        out_specs=fin_spec(D, 1),
        out_shape=jax.ShapeDtypeStruct((B, S, D), jnp.float32),
        scratch_shapes=[
            pltpu.VMEM((T, A_WIDTH), jnp.float32),
            pltpu.VMEM((T // WINDOW, A_KV_HEADS, 2 * WINDOW, A_GROUP * WINDOW), jnp.float32),
        ],
        compiler_params=pltpu.CompilerParams(
            dimension_semantics=("arbitrary",), vmem_limit_bytes=VMEM_LIMIT),
        name="final",
    )(x, dm, dm, dm, krm, krm, bias_a, sink_a, ob, w_zg, b_gate, w_oa, w_ob, w_out, gamma, beta)


def _bias_matrix(table, dist, valid):
    onehot = np.zeros((N_BUCKETS, dist.size), np.float32)
    onehot[_t5_bucket_np(dist).reshape(-1), np.arange(dist.size)] = 1.0
    vals = jnp.dot(table.T, jnp.asarray(onehot, jnp.bfloat16).astype(jnp.float32),
                   precision=lax.Precision.HIGHEST)
    return jnp.where(valid.reshape(-1), vals, NEG).reshape((table.shape[1],) + dist.shape)


def _bias_tables(rel_table, sinks):
    table_a = rel_table[:, :A_HEADS].astype(jnp.float32) * LOG2E
    table_b = rel_table[:, A_HEADS:].astype(jnp.float32) * LOG2E
    sinks = sinks.astype(jnp.float32) * LOG2E
    d_a = WINDOW + np.arange(WINDOW)[None, :] - np.arange(2 * WINDOW)[:, None]
    ba = _bias_matrix(table_a, d_a, (d_a >= 0) & (d_a < WINDOW))
    ba = ba.reshape(A_KV_HEADS, A_GROUP, 2 * WINDOW, WINDOW)
    ba = ba.transpose(0, 2, 1, 3).reshape(A_KV_HEADS, 2 * WINDOW, A_GROUP * WINDOW)
    sink = jnp.broadcast_to(sinks.reshape(A_KV_HEADS, 1, A_GROUP, 1),
                            (A_KV_HEADS, 1, A_GROUP, WINDOW)).reshape(A_KV_HEADS, 1, A_GROUP * WINDOW)
    d_b = MOBA_BLOCK + np.arange(MOBA_BLOCK)[None, :] - np.arange(2 * MOBA_BLOCK)[:, None]
    b_near = _bias_matrix(table_b, d_b, d_b >= 0)
    c31 = table_b[N_BUCKETS - 1]
    return ba, sink, b_near, c31


def kernel(x, w_in, b_gate, sinks, rel_table, w_out_a, w_out_b, w_out, ln_gamma, ln_beta):
    B, S, D = x.shape
    assert (D, w_in.shape[0]) == (D_MODEL, DEPTH) and S % PROJ_TOKENS == 0
    w = w_in[0]
    sizes = (A_WIDTH, A_KV_WIDTH, A_KV_WIDTH, A_WIDTH, B_WIDTH, B_WIDTH, B_WIDTH, B_WIDTH,
             D_MODEL, D_MODEL)
    offs = np.concatenate([[0], np.cumsum(sizes)])
    w_qa, w_ka, w_va, w_za, w_qb, w_kb, w_vb, w_zb, w_ga, w_gb = [
        w[:, offs[i]:offs[i + 1]] for i in range(len(sizes))]
    bf = jnp.bfloat16
    q_scale = ATTN_SCALE * LOG2E
    w_rm = jnp.concatenate([w_ka, w_kb], axis=1).astype(bf)
    w_dm_t = jnp.concatenate([w_qb * q_scale, w_qa * q_scale, w_vb, w_va], axis=1).T.astype(bf)
    w_zg = jnp.concatenate([w_za, w_zb, w_ga, w_gb], axis=1).astype(bf)
    bias_a, sink_a, b_near, c31 = _bias_tables(rel_table, sinks[0])

    krm, dm, sel = _proj(x, w_rm, w_dm_t)
    ob = _moba(dm, krm, sel, b_near, c31)
    return _final(x, dm, krm, bias_a, sink_a, ob, w_zg, b_gate[0][None, :],
                  w_out_a[0].astype(bf), w_out_b[0].astype(bf), w_out[0].astype(bf),
                  ln_gamma[0][None, :], ln_beta[0][None, :])
```

```python
import functools
import math

import numpy as np
import jax
import jax.numpy as jnp
from jax import lax
from jax.experimental import pallas as pl
from jax.experimental.pallas import tpu as pltpu

D_MODEL = 1024
HEAD_DIM = 64
A_HEADS = 8
A_KV_HEADS = 2
A_GROUP = A_HEADS // A_KV_HEADS
A_WIDTH = A_HEADS * HEAD_DIM
A_KV_WIDTH = A_KV_HEADS * HEAD_DIM
WINDOW = 128
B_HEADS = 8
B_WIDTH = B_HEADS * HEAD_DIM
MOBA_BLOCK = 256
MOBA_TOPK = 3
N_BUCKETS = 32
MAX_DISTANCE = 128
DEPTH = 1
DN_ALPHA = (2.0 * DEPTH) ** 0.25
LN_EPS = 1e-5
NEG = -1e30
ATTN_SCALE = HEAD_DIM ** -0.5
LOG2E = math.log2(math.e)

LANE = 128
HEAD_PAD = 2 * HEAD_DIM
W_QB, W_QA, W_VB, W_VA = 0, B_WIDTH, B_WIDTH + A_WIDTH, 2 * B_WIDTH + A_WIDTH
W_ROWS = W_VA + A_KV_WIDTH
DM_QB, DM_QA = 0, B_HEADS * HEAD_PAD
DM_VB = DM_QA + A_WIDTH
DM_VA = DM_VB + B_WIDTH
DM_ROWS = DM_VA + A_KV_WIDTH
RM_COLS = A_KV_WIDTH + B_WIDTH
PROJ_TOKENS = 512
SWA_TOKENS = 256
MERGE_SLICES = 2
FAR_CHUNK = 2
FAR_UNROLL = 80
NEAR_UNROLL = 16
ONES_ROWS = 16
SEL_PAD_ROWS = 8
VMEM_LIMIT = 56 * 1024 * 1024


def _t5_bucket_np(dist):
    max_exact = N_BUCKETS // 2
    n = np.maximum(dist, 0)
    nf = np.maximum(n, 1).astype(np.float32)
    large = max_exact + (np.log(nf / max_exact) / math.log(MAX_DISTANCE / max_exact)
                         * (N_BUCKETS - max_exact)).astype(np.int32)
    large = np.minimum(large, N_BUCKETS - 1)
    return np.where(n < max_exact, n, large).astype(np.int32)


def _select_rows(gate, n):
    nblk = gate.shape[0]
    blk = lax.broadcasted_iota(jnp.int32, gate.shape, 0)
    work = jnp.where(blk < n, gate, -jnp.inf)
    sel = jnp.zeros(gate.shape, jnp.float32)
    for _ in range(MOBA_TOPK):
        mx = jnp.max(work, axis=0, keepdims=True)
        idx = jnp.min(jnp.where(work == mx, blk, nblk), axis=0, keepdims=True)
        idx = jnp.where(mx > -jnp.inf, idx, nblk)
        pick = blk == idx
        sel = jnp.where(pick, 1.0, sel)
        work = jnp.where(pick, -jnp.inf, work)
    on_prev = jnp.max(jnp.where(blk == n - 1, sel, 0.0), axis=0, keepdims=True)
    far = jnp.where(blk < n - 1, sel, 0.0)
    pad_row = lax.broadcasted_iota(jnp.int32, (SEL_PAD_ROWS, gate.shape[1]), 0)
    return far, jnp.where(pad_row == 0, on_prev, 0.0)


def _proj_kernel(x_ref, wrm_ref, wdm_ref, krm_ref, dm_ref, sel_ref, kmean_ref):
    t = pl.program_id(1)
    blocks = PROJ_TOKENS // MOBA_BLOCK
    nblk = kmean_ref.shape[0]

    @pl.when(t == 0)
    def _():
        kmean_ref[...] = jnp.zeros(kmean_ref.shape, jnp.float32)

    xb = x_ref[0].astype(jnp.bfloat16)
    rm = jnp.dot(xb, wrm_ref[...], preferred_element_type=jnp.float32)
    krm_ref[0] = rm.astype(jnp.bfloat16)
    for c in range(blocks):
        kb = rm[c * MOBA_BLOCK:(c + 1) * MOBA_BLOCK, A_KV_WIDTH:]
        kmean_ref[pl.ds(t * blocks + c, 1), :] = (
            jnp.sum(kb, axis=0, keepdims=True) * (1.0 / MOBA_BLOCK))
    dm = lax.dot_general(wdm_ref[...], xb, (((1,), (1,)), ((), ())),
                         preferred_element_type=jnp.float32)
    dmb = dm.astype(jnp.bfloat16)
    zeros = jnp.zeros((HEAD_DIM, MOBA_BLOCK), jnp.bfloat16)
    km = kmean_ref[...].astype(jnp.bfloat16)
    for c in range(blocks):
        piece = dmb[:, c * MOBA_BLOCK:(c + 1) * MOBA_BLOCK]
        rows = []
        for h in range(B_HEADS):
            q = piece[W_QB + h * HEAD_DIM:W_QB + (h + 1) * HEAD_DIM]
            qpad = jnp.concatenate([q, zeros] if h % 2 == 0 else [zeros, q], axis=0)
            rows.append(qpad)
            pair = h // 2
            gate = jnp.dot(km[:, pair * HEAD_PAD:(pair + 1) * HEAD_PAD], qpad,
                           preferred_element_type=jnp.float32)
            far, near = _select_rows(gate, t * blocks + c)
            sel_ref[0, h, c, :nblk, :] = far
            sel_ref[0, h, c, nblk:, :] = near
        dm_ref[0, c] = jnp.concatenate(rows + [piece[W_QA:]], axis=0)


def _proj(x, w_rm, w_dm_t):
    B, S, D = x.shape
    nt = S // PROJ_TOKENS
    cpt = PROJ_TOKENS // MOBA_BLOCK
    NB = S // MOBA_BLOCK
    return pl.pallas_call(
        _proj_kernel,
        grid=(B, nt),
        in_specs=[
            pl.BlockSpec((1, PROJ_TOKENS, D), lambda b, t: (b, t, 0)),
            pl.BlockSpec((D, RM_COLS), lambda b, t: (0, 0)),
            pl.BlockSpec((W_ROWS, D), lambda b, t: (0, 0)),
        ],
        out_specs=[
            pl.BlockSpec((1, PROJ_TOKENS, RM_COLS), lambda b, t: (b, t, 0)),
            pl.BlockSpec((1, cpt, DM_ROWS, MOBA_BLOCK), lambda b, t: (b, t, 0, 0)),
            pl.BlockSpec((1, B_HEADS, cpt, NB + SEL_PAD_ROWS, MOBA_BLOCK),
                         lambda b, t: (b, 0, t, 0, 0)),
        ],
        out_shape=[
            jax.ShapeDtypeStruct((B, S, RM_COLS), jnp.bfloat16),
            jax.ShapeDtypeStruct((B, NB, DM_ROWS, MOBA_BLOCK), jnp.bfloat16),
            jax.ShapeDtypeStruct((B, B_HEADS, NB, NB + SEL_PAD_ROWS, MOBA_BLOCK), jnp.float32),
        ],
        scratch_shapes=[pltpu.VMEM((NB, B_WIDTH), jnp.float32)],
        compiler_params=pltpu.CompilerParams(
            dimension_semantics=("parallel", "arbitrary"), vmem_limit_bytes=VMEM_LIMIT),
        name="proj",
    )(x, w_rm, w_dm_t)


def _swa_logits(t, qt_ref, kcur_ref, kprev_ref, bias_ref, s_ref):
    qt = qt_ref[0, 0]
    kcat = jnp.concatenate([kprev_ref[0], kcur_ref[0]], axis=0)
    kidx = lax.broadcasted_iota(jnp.int32, (2 * WINDOW, A_GROUP * WINDOW), 0)
    zeros_q = jnp.zeros((HEAD_DIM, A_GROUP * WINDOW), jnp.bfloat16)
    for w in range(SWA_TOKENS // WINDOW):
        k2 = kcat[w * WINDOW:(w + 2) * WINDOW, :]
        for g in range(A_KV_HEADS):
            q4 = jnp.concatenate(
                [qt[(g * A_GROUP + i) * HEAD_DIM:(g * A_GROUP + i + 1) * HEAD_DIM,
                    w * WINDOW:(w + 1) * WINDOW] for i in range(A_GROUP)], axis=1)
            qpad = jnp.concatenate([q4, zeros_q] if g == 0 else [zeros_q, q4], axis=0)
            s = jnp.dot(k2, qpad, preferred_element_type=jnp.float32) + bias_ref[g]
            if w == 0:
                s = jnp.where((t == 0) & (kidx < WINDOW), NEG, s)
            s_ref[w, g] = s


def _swa_unit(w, g, s_ref, vcat, sink_ref):
    s = s_ref[w, g]
    sink = sink_ref[g]
    m = jnp.maximum(jnp.max(s, axis=0, keepdims=True), sink)
    p = jnp.exp2(s - m)
    l = jnp.sum(p, axis=0, keepdims=True) + jnp.exp2(sink - m)
    v2 = vcat[g * HEAD_DIM:(g + 1) * HEAD_DIM, w * WINDOW:(w + 2) * WINDOW]
    o = jnp.dot(v2, p.astype(jnp.bfloat16), preferred_element_type=jnp.float32) / l
    return [o[:, i * WINDOW:(i + 1) * WINDOW] for i in range(A_GROUP)]


def _swa_assemble(units):
    return jnp.concatenate(
        [jnp.concatenate([o for g in range(A_KV_HEADS) for o in units[w][g]], axis=0).T
         for w in range(SWA_TOKENS // WINDOW)], axis=0)


def _far_schedule(nblk):
    items = [(n, c) for n in range(nblk) for c in range((max(n - 1, 0) + FAR_CHUNK - 1) // FAR_CHUNK)]
    assert len(items) % FAR_UNROLL == 0
    return (np.array([i[0] for i in items], np.int32), np.array([i[1] for i in items], np.int32))


def _moba_kernel(nof_ref, cof_ref, c31_ref, q_ref, k_ref, vt_ref, sel_ref, bnear_ref, o_ref,
                 sa_ref, sb_ref, m_ref, acc_ref):
    pr = pl.program_id(1)
    nblk = q_ref.shape[1]
    nitems = nof_ref.shape[0]
    heads = (0, 1)
    c31 = [c31_ref[2 * pr + r] for r in heads]
    ones_rows = jnp.ones((ONES_ROWS, MOBA_BLOCK), jnp.bfloat16)

    def qpad(n, r):
        return q_ref[0, n, r * HEAD_PAD:(r + 1) * HEAD_PAD, :]

    def k_blk(j, nb=1):
        return k_ref[0, pl.ds(pl.multiple_of(j * MOBA_BLOCK, MOBA_BLOCK), nb * MOBA_BLOCK), :]

    def vt_ext(blocks, r):
        return jnp.concatenate(
            [jnp.concatenate([vt_ref[0, j, r * HEAD_DIM:(r + 1) * HEAD_DIM, :], ones_rows], axis=0)
             for j in blocks], axis=1)

    def on_row(n, r, row):
        return sel_ref[0, r, n, pl.ds(row, 1), :] > 0.5

    def near_logits(n, buf_ref):
        jp = jnp.maximum(n - 1, 0)
        kc = jnp.concatenate([k_blk(jp), k_blk(n)], axis=0)
        out = []
        for r in heads:
            s = jnp.dot(kc, qpad(n, r), preferred_element_type=jnp.float32) + bnear_ref[r]
            buf_ref[r] = s
            out.append(jnp.maximum(
                jnp.max(s[MOBA_BLOCK:], axis=0, keepdims=True),
                jnp.where(on_row(n, r, nblk),
                          jnp.max(s[:MOBA_BLOCK], axis=0, keepdims=True), -jnp.inf)))
        return out

    def near_softmax(n, buf_ref, ms):
        jp = jnp.maximum(n - 1, 0)
        for r in heads:
            s = buf_ref[r]
            p_prev = jnp.exp2(s[:MOBA_BLOCK] - jnp.where(on_row(n, r, nblk), ms[r], -NEG))
            p_own = jnp.exp2(s[MOBA_BLOCK:] - ms[r])
            p = jnp.concatenate([p_prev, p_own], axis=0).astype(jnp.bfloat16)
            m_ref[n, r] = ms[r]
            acc_ref[n, r] = jnp.dot(vt_ext((jp, n), r), p, preferred_element_type=jnp.float32)

    def near_body(it, m_a):
        for u in range(NEAR_UNROLL // 2):
            n = NEAR_UNROLL * it + 2 * u
            m_b = near_logits(n + 1, sb_ref)
            near_softmax(n, sa_ref, m_a)
            m_a = near_logits(jnp.minimum(n + 2, nblk - 1), sa_ref)
            near_softmax(n + 1, sb_ref, m_b)
        return tuple(m_a)

    lax.fori_loop(0, nblk // NEAR_UNROLL, near_body, tuple(near_logits(0, sa_ref)))

    def far_logits(k, buf_ref):
        n, c = nof_ref[k], cof_ref[k]
        kc = k_blk(c * FAR_CHUNK, FAR_CHUNK)
        out = []
        for r in heads:
            s = jnp.dot(kc, qpad(n, r), preferred_element_type=jnp.float32)
            buf_ref[r] = s
            mc = jnp.full((1, MOBA_BLOCK), -jnp.inf, jnp.float32)
            for i in range(FAR_CHUNK):
                bm = jnp.max(s[i * MOBA_BLOCK:(i + 1) * MOBA_BLOCK], axis=0, keepdims=True)
                mc = jnp.maximum(mc, jnp.where(on_row(n, r, c * FAR_CHUNK + i), bm + c31[r], -jnp.inf))
            out.append(mc)
        return out

    def far_softmax(k, buf_ref, mcs):
        n, c = nof_ref[k], cof_ref[k]
        for r in heads:
            m = m_ref[n, r]
            m_new = jnp.maximum(m, mcs[r])
            alpha = jnp.exp2(m - m_new)
            shift = m_new - c31[r]
            s = buf_ref[r]
            ps = []
            for i in range(FAR_CHUNK):
                on = on_row(n, r, c * FAR_CHUNK + i)
                p = jnp.exp2(s[i * MOBA_BLOCK:(i + 1) * MOBA_BLOCK] - jnp.where(on, shift, -NEG))
                ps.append(p.astype(jnp.bfloat16))
            vt = vt_ext([c * FAR_CHUNK + i for i in range(FAR_CHUNK)], r)
            m_ref[n, r] = m_new
            acc_ref[n, r] = alpha * acc_ref[n, r] + jnp.dot(
                vt, jnp.concatenate(ps, axis=0), preferred_element_type=jnp.float32)

    def far_body(it, mc_a):
        for u in range(FAR_UNROLL // 2):
            k = FAR_UNROLL * it + 2 * u
            mc_b = far_logits(k + 1, sb_ref)
            far_softmax(k, sa_ref, mc_a)
            mc_a = far_logits(jnp.minimum(k + 2, nitems - 1), sa_ref)
            far_softmax(k + 1, sb_ref, mc_b)
        return tuple(mc_a)

    lax.fori_loop(0, nitems // FAR_UNROLL, far_body, tuple(far_logits(0, sa_ref)))

    def out_body(it, _):
        for u in range(2):
            n = 2 * it + u
            ot = jnp.concatenate([acc_ref[n, r, :HEAD_DIM, :] / acc_ref[n, r, HEAD_DIM:HEAD_DIM + 1, :]
                                  for r in heads], axis=0)
            o_ref[0, pl.ds(pl.multiple_of(n * MOBA_BLOCK, MOBA_BLOCK), MOBA_BLOCK), :] = (
                ot.T.astype(jnp.bfloat16))
        return 0

    lax.fori_loop(0, nblk // 2, out_body, 0)


def _moba(dm, krm, sel, bias_near, c31):
    B, NB = dm.shape[0], dm.shape[1]
    S = NB * MOBA_BLOCK
    pairs = B_HEADS // 2
    vb_blk = DM_VB // LANE
    kb_blk = A_KV_WIDTH // LANE
    n_of, c_of = _far_schedule(NB)
    smem = pl.BlockSpec(memory_space=pltpu.SMEM)
    return pl.pallas_call(
        _moba_kernel,
        grid=(B, pairs),
        in_specs=[
            smem, smem, smem,
            pl.BlockSpec((1, NB, 2 * HEAD_PAD, MOBA_BLOCK), lambda b, p: (b, 0, p, 0)),
            pl.BlockSpec((1, S, LANE), lambda b, p: (b, 0, kb_blk + p)),
            pl.BlockSpec((1, NB, LANE, MOBA_BLOCK), lambda b, p: (b, 0, vb_blk + p, 0)),
            pl.BlockSpec((1, 2, NB, NB + SEL_PAD_ROWS, MOBA_BLOCK), lambda b, p: (b, p, 0, 0, 0)),
            pl.BlockSpec((2, 2 * MOBA_BLOCK, MOBA_BLOCK), lambda b, p: (p, 0, 0)),
        ],
        out_specs=pl.BlockSpec((1, S, LANE), lambda b, p: (b, 0, p)),
        out_shape=jax.ShapeDtypeStruct((B, S, B_WIDTH), jnp.bfloat16),
        scratch_shapes=[
            pltpu.VMEM((2, FAR_CHUNK * MOBA_BLOCK, MOBA_BLOCK), jnp.float32),
            pltpu.VMEM((2, FAR_CHUNK * MOBA_BLOCK, MOBA_BLOCK), jnp.float32),
            pltpu.VMEM((NB, 2, 1, MOBA_BLOCK), jnp.float32),
            pltpu.VMEM((NB, 2, HEAD_DIM + ONES_ROWS, MOBA_BLOCK), jnp.float32),
        ],
        compiler_params=pltpu.CompilerParams(
            dimension_semantics=("parallel", "parallel"), vmem_limit_bytes=VMEM_LIMIT),
        name="moba",
    )(jnp.asarray(n_of), jnp.asarray(c_of), c31, dm, krm, dm, sel, bias_near)


def _final_kernel(x_ref, qt_ref, vcur_ref, vprev_ref, kcur_ref, kprev_ref, bias_ref, sink_ref,
                  ob_ref, wzg_ref, bg_ref, woa_ref, wob_ref, wout_ref, gamma_ref, beta_ref, y_ref,
                  oa_ref, s_ref, *, tiles_per_row, n_tiles):
    s = pl.program_id(0)

    @pl.when(s == 0)
    def _():
        oa_ref[...] = jnp.zeros(oa_ref.shape, jnp.float32)

    t_next = jnp.minimum(s, n_tiles - 1) % tiles_per_row
    _swa_logits(t_next, qt_ref, kcur_ref, kprev_ref, bias_ref, s_ref)
    oa = oa_ref[...]
    vcat = jnp.concatenate([vprev_ref[0, 0][:, WINDOW:], vcur_ref[0, 0]], axis=1)
    units = [[None] * A_KV_HEADS for _ in range(SWA_TOKENS // WINDOW)]
    x = x_ref[0]
    xb = x.astype(jnp.bfloat16)
    zw = A_WIDTH + B_WIDTH
    z = jnp.dot(xb, wzg_ref[:, :zw], preferred_element_type=jnp.float32)
    units[0][0] = _swa_unit(0, 0, s_ref, vcat, sink_ref)
    za, zb = z[:, :A_WIDTH], z[:, A_WIDTH:]
    ua = (oa * (za * jax.nn.sigmoid(za))).astype(jnp.bfloat16)
    ub = (ob_ref[0].astype(jnp.float32) * (zb * jax.nn.sigmoid(zb))).astype(jnp.bfloat16)
    width = D_MODEL // MERGE_SLICES
    mid_unit = {0: (0, 1), MERGE_SLICES - 1: (1, 1)}
    end_unit = {0: (1, 0)}
    merged = []
    for i in range(MERGE_SLICES):
        lo = i * width
        cols = slice(lo, lo + width)
        ga = jnp.dot(xb, wzg_ref[:, zw + lo:zw + lo + width],
                     preferred_element_type=jnp.float32) + bg_ref[:, cols]
        ya = jnp.dot(ua, woa_ref[:, cols], preferred_element_type=jnp.float32)
        part = jax.nn.sigmoid(ga) * ya
        if i in mid_unit:
            w, g = mid_unit[i]
            units[w][g] = _swa_unit(w, g, s_ref, vcat, sink_ref)
        gb = jnp.dot(xb, wzg_ref[:, zw + D_MODEL + lo:zw + D_MODEL + lo + width],
                     preferred_element_type=jnp.float32) + bg_ref[:, D_MODEL + lo:D_MODEL + lo + width]
        yb = jnp.dot(ub, wob_ref[:, cols], preferred_element_type=jnp.float32)
        merged.append((part + jax.nn.sigmoid(gb) * yb).astype(jnp.bfloat16))
        if i in end_unit:
            w, g = end_unit[i]
            units[w][g] = _swa_unit(w, g, s_ref, vcat, sink_ref)
    out = jnp.dot(jnp.concatenate(merged, axis=1), wout_ref[...],
                  preferred_element_type=jnp.float32)
    oa_ref[...] = _swa_assemble(units)
    r = DN_ALPHA * x + out
    mu = jnp.mean(r, axis=-1, keepdims=True)
    rc = r - mu
    var = jnp.mean(rc * rc, axis=-1, keepdims=True)
    y_ref[0] = rc * lax.rsqrt(var + LN_EPS) * gamma_ref[...] + beta_ref[...]


def _final(x, dm, krm, bias_a, sink_a, ob, w_zg, b_gate, w_oa, w_ob, w_out, gamma, beta):
    B, S, D = x.shape
    T = SWA_TOKENS
    nt = S // T
    n_tiles = B * nt
    qa_blk = DM_QA // A_WIDTH
    va_blk = DM_VA // A_KV_WIDTH

    def tile(s, lag):
        i = jnp.clip(s - lag, 0, n_tiles - 1)
        return i // nt, i % nt

    def swa_spec(shape, idx):
        return pl.BlockSpec(shape, lambda s: idx(*tile(s, 0)))

    def fin_spec(width, lag):
        return pl.BlockSpec((1, T, width), lambda s: tile(s, lag) + (0,))

    full = lambda a: pl.BlockSpec(a.shape, lambda s: (0,) * a.ndim)
    return pl.pallas_call(
        functools.partial(_final_kernel, tiles_per_row=nt, n_tiles=n_tiles),
        grid=(n_tiles + 1,),
        in_specs=[
            fin_spec(D, 1),
            swa_spec((1, 1, A_WIDTH, T), lambda b, t: (b, t, qa_blk, 0)),
            swa_spec((1, 1, A_KV_WIDTH, T), lambda b, t: (b, t, va_blk, 0)),
            swa_spec((1, 1, A_KV_WIDTH, T), lambda b, t: (b, jnp.maximum(t - 1, 0), va_blk, 0)),
            swa_spec((1, T, A_KV_WIDTH), lambda b, t: (b, t, 0)),
            swa_spec((1, WINDOW, A_KV_WIDTH), lambda b, t: (b, jnp.maximum(2 * t - 1, 0), 0)),
            full(bias_a), full(sink_a),
            fin_spec(B_WIDTH, 1),
            full(w_zg), full(b_gate), full(w_oa), full(w_ob), full(w_out), full(gamma), full(beta),
        ],
        out_specs=fin_spec(D, 1),
        out_shape=jax.ShapeDtypeStruct((B, S, D), jnp.float32),
        scratch_shapes=[
            pltpu.VMEM((T, A_WIDTH), jnp.float32),
            pltpu.VMEM((T // WINDOW, A_KV_HEADS, 2 * WINDOW, A_GROUP * WINDOW), jnp.float32),
        ],
        compiler_params=pltpu.CompilerParams(
            dimension_semantics=("arbitrary",), vmem_limit_bytes=VMEM_LIMIT),
        name="final",
    )(x, dm, dm, dm, krm, krm, bias_a, sink_a, ob, w_zg, b_gate, w_oa, w_ob, w_out, gamma, beta)


def _bias_matrix(table, dist, valid):
    onehot = np.zeros((N_BUCKETS, dist.size), np.float32)
    onehot[_t5_bucket_np(dist).reshape(-1), np.arange(dist.size)] = 1.0
    vals = jnp.dot(table.T, jnp.asarray(onehot, jnp.bfloat16).astype(jnp.float32),
                   precision=lax.Precision.HIGHEST)
    return jnp.where(valid.reshape(-1), vals, NEG).reshape((table.shape[1],) + dist.shape)


def _bias_tables(rel_table, sinks):
    table_a = rel_table[:, :A_HEADS].astype(jnp.float32) * LOG2E
    table_b = rel_table[:, A_HEADS:].astype(jnp.float32) * LOG2E
    sinks = sinks.astype(jnp.float32) * LOG2E
    d_a = WINDOW + np.arange(WINDOW)[None, :] - np.arange(2 * WINDOW)[:, None]
    ba = _bias_matrix(table_a, d_a, (d_a >= 0) & (d_a < WINDOW))
    ba = ba.reshape(A_KV_HEADS, A_GROUP, 2 * WINDOW, WINDOW)
    ba = ba.transpose(0, 2, 1, 3).reshape(A_KV_HEADS, 2 * WINDOW, A_GROUP * WINDOW)
    sink = jnp.broadcast_to(sinks.reshape(A_KV_HEADS, 1, A_GROUP, 1),
                            (A_KV_HEADS, 1, A_GROUP, WINDOW)).reshape(A_KV_HEADS, 1, A_GROUP * WINDOW)
    d_b = MOBA_BLOCK + np.arange(MOBA_BLOCK)[None, :] - np.arange(2 * MOBA_BLOCK)[:, None]
    b_near = _bias_matrix(table_b, d_b, d_b >= 0)
    c31 = table_b[N_BUCKETS - 1]
    return ba, sink, b_near, c31


def kernel(x, w_in, b_gate, sinks, rel_table, w_out_a, w_out_b, w_out, ln_gamma, ln_beta):
    B, S, D = x.shape
    assert (D, w_in.shape[0]) == (D_MODEL, DEPTH) and S % PROJ_TOKENS == 0
    w = w_in[0]
    sizes = (A_WIDTH, A_KV_WIDTH, A_KV_WIDTH, A_WIDTH, B_WIDTH, B_WIDTH, B_WIDTH, B_WIDTH,
             D_MODEL, D_MODEL)
    offs = np.concatenate([[0], np.cumsum(sizes)])
    w_qa, w_ka, w_va, w_za, w_qb, w_kb, w_vb, w_zb, w_ga, w_gb = [
        w[:, offs[i]:offs[i + 1]] for i in range(len(sizes))]
    bf = jnp.bfloat16
    q_scale = ATTN_SCALE * LOG2E
    w_rm = jnp.concatenate([w_ka, w_kb], axis=1).astype(bf)
    w_dm_t = jnp.concatenate([w_qb * q_scale, w_qa * q_scale, w_vb, w_va], axis=1).T.astype(bf)
    w_zg = jnp.concatenate([w_za, w_zb, w_ga, w_gb], axis=1).astype(bf)
    bias_a, sink_a, b_near, c31 = _bias_tables(rel_table, sinks[0])

    krm, dm, sel = _proj(x, w_rm, w_dm_t)
    ob = _moba(dm, krm, sel, b_near, c31)
    return _final(x, dm, krm, bias_a, sink_a, ob, w_zg, b_gate[0][None, :],
                  w_out_a[0].astype(bf), w_out_b[0].astype(bf), w_out[0].astype(bf),
                  ln_gamma[0][None, :], ln_beta[0][None, :])
```
